```python
import math
import jax, jax.numpy as jnp
from jax import lax
import numpy as np

D_MODEL = 1024
BATCH = 16
SEQ = 2048
DEPTH = 1

ATT_HEAD_DIM = 64
ATT_HEADS_PER_GROUP = 12
DILATED_PATTERNS = ((128, 1), (512, 4), (2048, 16))
N_ATT_GROUPS = 3
ATT_HEADS = N_ATT_GROUPS * ATT_HEADS_PER_GROUP
ATT_QKV = ATT_HEADS * ATT_HEAD_DIM
ATT_OUT = ATT_HEADS_PER_GROUP * ATT_HEAD_DIM
BAND_BLOCK = 128
NUM_BUCKETS = 32
MAX_DISTANCE = 2048
SSM_EXPAND = 2
D_INNER = SSM_EXPAND * D_MODEL
SSM_HEAD_DIM = 64
SSM_HEADS = D_INNER // SSM_HEAD_DIM
SSM_GROUPS = 4
D_STATE = 128
CONV_WIDTH = 4
CONV_DIM = D_INNER + 2 * SSM_GROUPS * D_STATE
SSD_CHUNK = 128
PLE_DIM = 256
ALPHA = (2.0 * DEPTH) ** 0.25
BETA = (8.0 * DEPTH) ** -0.25
LN_EPS = 1e-5
RMS_EPS = 1e-5
Q_END = ATT_QKV
K_END = Q_END + ATT_QKV
V_END = K_END + ATT_QKV
GATT_END = V_END + ATT_OUT
Z_END = GATT_END + D_INNER
XBC_END = Z_END + CONV_DIM
DT_END = XBC_END + SSM_HEADS
GMERGE_END = DT_END + 2 * D_MODEL
IN_COLS = GMERGE_END + D_MODEL
BRANCH_ROWS = ATT_OUT + D_INNER

kernel_name = "hybrid_dilated_attn_mamba2_deepnorm"


def _layer_norm(x, g, b):
    xf = x.astype(jnp.float32)
    mu = jnp.mean(xf, -1, keepdims=True)
    var = jnp.mean(jnp.square(xf - mu), -1, keepdims=True)
    return ((xf - mu) * lax.rsqrt(var + LN_EPS) * g.astype(jnp.float32) + b.astype(jnp.float32)).astype(x.dtype)


def _t5_bucket(dist):
    max_exact = NUM_BUCKETS // 2
    d_f = jnp.maximum(dist, 1).astype(jnp.float32)
    large = max_exact + (jnp.log(d_f / max_exact) / math.log(MAX_DISTANCE / max_exact)
                         * (NUM_BUCKETS - max_exact)).astype(jnp.int32)
    large = jnp.minimum(large, NUM_BUCKETS - 1)
    return jnp.where(dist < max_exact, dist, large)


def _dilated_group(q, k, v, bias_table, window, dilation):
    b, s, h, dh = q.shape
    span = dilation * BAND_BLOCK
    s_pad = -(-s // span) * span
    sub_len = s_pad // dilation
    nb = sub_len // BAND_BLOCK

    def to_blocks(t):
        t = jnp.pad(t, ((0, 0), (0, s_pad - s), (0, 0), (0, 0)))
        t = t.reshape(b, sub_len, dilation, h, dh).transpose(0, 2, 3, 1, 4)
        return t.reshape(b, dilation, h, nb, BAND_BLOCK, dh)

    def band(t):
        prev = jnp.pad(t, ((0, 0), (0, 0), (0, 0), (1, 0), (0, 0), (0, 0)))[:, :, :, :-1]
        return jnp.concatenate([prev, t], axis=4)

    qb = to_blocks(q)
    kk = band(to_blocks(k))
    vv = band(to_blocks(v))
    scores = jnp.einsum('brhnqd,brhnkd->brhnqk', qb, kk,
                        preferred_element_type=jnp.float32) * (dh ** -0.5)
    qi = jnp.arange(BAND_BLOCK)[:, None]
    kj = jnp.arange(2 * BAND_BLOCK)[None, :]
    delta = qi + BAND_BLOCK - kj
    blk = jnp.arange(nb)[:, None, None]
    valid = (delta >= 0) & (delta <= window // dilation) & (blk * BAND_BLOCK + kj - BAND_BLOCK >= 0)
    bucket = _t5_bucket(jnp.maximum(delta, 0) * dilation)
    bias = bias_table[bucket].astype(jnp.float32).transpose(2, 0, 1)
    scores = jnp.where(valid, scores + bias[:, None], -jnp.inf)
    m = jnp.max(scores, -1, keepdims=True)
    e = jnp.exp(scores - m)
    den = jnp.sum(e, -1)
    out = jnp.einsum('brhnqk,brhnkd->brhnqd', e, vv.astype(jnp.float32)) / den[..., None]
    lse = m[..., 0] + jnp.log(den)
    out = out.reshape(b, dilation, h, sub_len, dh).transpose(0, 3, 1, 2, 4).reshape(b, s_pad, h, dh)[:, :s]
    lse = lse.reshape(b, dilation, h, sub_len).transpose(0, 3, 1, 2).reshape(b, s_pad, h)[:, :s]
    return out, lse


def _causal_conv(u, w, bias):
    c = u.shape[-1]
    out = lax.conv_general_dilated(u, w[:, None, :].astype(u.dtype), window_strides=(1,),
                                   padding=[(CONV_WIDTH - 1, 0)],
                                   dimension_numbers=('NWC', 'WIO', 'NWC'),
                                   feature_group_count=c)
    return out + bias.astype(u.dtype)


def _ssd(xh, dt, a_head, bm, cm):
    b, s, h, p = xh.shape
    g, n = bm.shape[2], bm.shape[3]
    r = h // g
    nc = s // SSD_CHUNK
    lq = SSD_CHUNK
    a = (dt * a_head).reshape(b, nc, lq, h)
    a_cs = jnp.cumsum(a, axis=2)
    xdt = (xh * dt[..., None]).reshape(b, nc, lq, g, r, p)
    bc = bm.reshape(b, nc, lq, g, n)
    cc = cm.reshape(b, nc, lq, g, n)
    seg = a_cs[:, :, :, None, :] - a_cs[:, :, None, :, :]
    causal = jnp.tril(jnp.ones((lq, lq), dtype=bool))
    lmat = jnp.exp(jnp.where(causal[:, :, None], seg, -jnp.inf)).reshape(b, nc, lq, lq, g, r)
    cb = jnp.einsum('bclgn,bcsgn->bclsg', cc, bc)
    y_diag = jnp.einsum('bclsgr,bcsgrp->bclgrp', cb[..., None] * lmat, xdt)
    decay_to_end = jnp.exp(a_cs[:, :, -1:, :] - a_cs).reshape(b, nc, lq, g, r)
    states = jnp.einsum('bclgn,bclgrp->bcgrpn', bc, xdt * decay_to_end[..., None])
    chunk_decay = jnp.exp(a_cs[:, :, -1, :]).reshape(b, nc, g, r)

    def step(carry, inp):
        st, dec = inp
        return carry * dec[..., None, None] + st, carry

    init = jnp.zeros((b, g, r, p, n), jnp.float32)
    _, prev_states = lax.scan(step, init, (jnp.moveaxis(states, 1, 0), jnp.moveaxis(chunk_decay, 1, 0)))
    prev_states = jnp.moveaxis(prev_states, 0, 1)
    decay_from_start = jnp.exp(a_cs).reshape(b, nc, lq, g, r)
    y_off = jnp.einsum('bclgn,bcgrpn->bclgrp', cc, prev_states) * decay_from_start[..., None]
    return (y_diag + y_off).reshape(b, s, h, p)


def _hybrid_layer(x, p_i, w_in, b_gate, conv_w, conv_b, dt_bias, a_log, d_skip, ssm_norm_w,
                  w_branch, w_out, w_ple, ln_g, ln_b, rel_bias):
    b, s, _ = x.shape
    hcat = jnp.einsum('bsd,de->bse', x, w_in)
    q, k, v, g_att, z, xbc, dt_raw, g_merge, g_ple = jnp.split(
        hcat, [Q_END, K_END, V_END, GATT_END, Z_END, XBC_END, DT_END, GMERGE_END], axis=-1)

    q = q.reshape(b, s, N_ATT_GROUPS, ATT_HEADS_PER_GROUP, ATT_HEAD_DIM)
    k = k.reshape(b, s, N_ATT_GROUPS, ATT_HEADS_PER_GROUP, ATT_HEAD_DIM)
    v = v.reshape(b, s, N_ATT_GROUPS, ATT_HEADS_PER_GROUP, ATT_HEAD_DIM)
    outs, lses = [], []
    for gi, (win, dil) in enumerate(DILATED_PATTERNS):
        hs = slice(gi * ATT_HEADS_PER_GROUP, (gi + 1) * ATT_HEADS_PER_GROUP)
        o, l = _dilated_group(q[:, :, gi], k[:, :, gi], v[:, :, gi], rel_bias[:, hs], win, dil)
        outs.append(o)
        lses.append(l)
    wts = jax.nn.softmax(jnp.stack(lses), axis=0)
    o_att = jnp.sum(wts[..., None] * jnp.stack(outs), axis=0).reshape(b, s, ATT_OUT).astype(x.dtype)
    o_att = o_att * jax.nn.silu(g_att)

    xbc = jax.nn.silu(_causal_conv(xbc, conv_w, conv_b))
    xs, bm, cm = jnp.split(xbc, [D_INNER, D_INNER + SSM_GROUPS * D_STATE], axis=-1)
    xh = xs.astype(jnp.float32).reshape(b, s, SSM_HEADS, SSM_HEAD_DIM)
    bm = bm.astype(jnp.float32).reshape(b, s, SSM_GROUPS, D_STATE)
    cm = cm.astype(jnp.float32).reshape(b, s, SSM_GROUPS, D_STATE)
    dt = jax.nn.softplus(dt_raw.astype(jnp.float32) + dt_bias.astype(jnp.float32))
    a_head = -jnp.exp(a_log.astype(jnp.float32))
    y = _ssd(xh, dt, a_head, bm, cm) + d_skip.astype(jnp.float32)[:, None] * xh
    u = (y.reshape(b, s, D_INNER) * jax.nn.silu(z.astype(jnp.float32))).reshape(b, s, SSM_GROUPS, -1)
    u = u * lax.rsqrt(jnp.mean(jnp.square(u), -1, keepdims=True) + RMS_EPS)
    y_ssm = (u.reshape(b, s, D_INNER) * ssm_norm_w.astype(jnp.float32)).astype(x.dtype)

    y_a = jnp.einsum('bse,ed->bsd', o_att, w_branch[:ATT_OUT])
    y_b = jnp.einsum('bse,ed->bsd', y_ssm, w_branch[ATT_OUT:])
    g_a, g_b = jnp.split(g_merge, 2, axis=-1)
    merged = jax.nn.sigmoid(g_a + b_gate[0]) * y_a + jax.nn.sigmoid(g_b + b_gate[1]) * y_b
    mix = jnp.einsum('bsd,de->bse', merged, w_out)
    ple = jax.nn.sigmoid(g_ple + b_gate[2]) * jnp.einsum('bsq,qd->bsd', p_i, w_ple)
    return _layer_norm(ALPHA * x + mix + ple, ln_g, ln_b)


def setup_inputs(seed: int = 0) -> dict:
    key = jax.random.key(seed)
    ks = jax.random.split(key, 16)
    f32 = jnp.float32
    x = jax.random.normal(ks[0], (BATCH, SEQ, D_MODEL), f32)
    p = jax.random.normal(ks[1], (DEPTH, BATCH, SEQ, PLE_DIM), f32)
    col_scale = jnp.ones((IN_COLS,), f32).at[K_END:V_END].set(BETA).at[Z_END:Z_END + D_INNER].set(BETA)
    w_in = jax.random.normal(ks[2], (DEPTH, D_MODEL, IN_COLS), f32) * (D_MODEL ** -0.5) * col_scale
    b_gate = 0.1 * jax.random.normal(ks[3], (DEPTH, 3, D_MODEL), f32)
    conv_w = 0.5 * jax.random.normal(ks[4], (DEPTH, CONV_WIDTH, CONV_DIM), f32)
    conv_b = 0.05 * jax.random.normal(ks[5], (DEPTH, CONV_DIM), f32)
    dt0 = jnp.exp(jax.random.uniform(ks[6], (DEPTH, SSM_HEADS), f32, math.log(1e-3), math.log(1e-1)))
    dt_bias = dt0 + jnp.log(-jnp.expm1(-dt0))
    a_log = jnp.log(jax.random.uniform(ks[7], (DEPTH, SSM_HEADS), f32, 1.0, 16.0))
    d_skip = 1.0 + 0.1 * jax.random.normal(ks[8], (DEPTH, SSM_HEADS), f32)
    ssm_norm_w = 1.0 + 0.05 * jax.random.normal(ks[9], (DEPTH, D_INNER), f32)
    w_branch = jnp.concatenate([
        jax.random.normal(ks[10], (DEPTH, ATT_OUT, D_MODEL), f32) * (ATT_OUT ** -0.5),
        jax.random.normal(ks[11], (DEPTH, D_INNER, D_MODEL), f32) * (D_INNER ** -0.5)], axis=1) * BETA
    w_out = jax.random.normal(ks[12], (DEPTH, D_MODEL, D_MODEL), f32) * (D_MODEL ** -0.5) * BETA
    w_ple = jax.random.normal(ks[13], (DEPTH, PLE_DIM, D_MODEL), f32) * (PLE_DIM ** -0.5) * BETA
    kg, kb = jax.random.split(ks[14])
    ln_g = 1.0 + 0.05 * jax.random.normal(kg, (DEPTH, D_MODEL), f32)
    ln_b = 0.02 * jax.random.normal(kb, (DEPTH, D_MODEL), f32)
    rel_bias = 0.2 * jax.random.normal(ks[15], (NUM_BUCKETS, ATT_HEADS), f32)
    return {"x": x, "p": p, "w_in": w_in, "b_gate": b_gate, "conv_w": conv_w, "conv_b": conv_b,
            "dt_bias": dt_bias, "a_log": a_log, "d_skip": d_skip, "ssm_norm_w": ssm_norm_w,
            "w_branch": w_branch, "w_out": w_out, "w_ple": w_ple, "ln_g": ln_g, "ln_b": ln_b,
            "rel_bias": rel_bias}


def reference(x, p, w_in, b_gate, conv_w, conv_b, dt_bias, a_log, d_skip, ssm_norm_w,
              w_branch, w_out, w_ple, ln_g, ln_b, rel_bias):
    for i in range(DEPTH):
        x = _hybrid_layer(x, p[i], w_in[i], b_gate[i], conv_w[i], conv_b[i], dt_bias[i], a_log[i],
                          d_skip[i], ssm_norm_w[i], w_branch[i], w_out[i], w_ple[i], ln_g[i], ln_b[i],
                          rel_bias)
    return x
```

```python
import functools
import math

import numpy as np
import jax
import jax.numpy as jnp
from jax import lax
from jax.experimental import pallas as pl
from jax.experimental.pallas import tpu as pltpu

F32 = jnp.float32
BF16 = jnp.bfloat16

LANES = 128
SUBLANES = 8
V7X_VMEM_BYTES = 64 * 1024 * 1024
VMEM_LIMIT = (V7X_VMEM_BYTES * 3) // 4

D_MODEL = 1024
HEAD_DIM = 64
HEADS_PER_GROUP = 12
PATTERNS = ((128, 1), (512, 4), (2048, 16))
N_GROUPS = 3
ATT_HEADS = N_GROUPS * HEADS_PER_GROUP
ATT_QKV = ATT_HEADS * HEAD_DIM
ATT_OUT = HEADS_PER_GROUP * HEAD_DIM
BLK = 128
NUM_BUCKETS = 32
MAX_DISTANCE = 2048
D_INNER = 2048
SSM_HEADS = 32
SSM_GROUPS = 4
D_STATE = 128
CONV_WIDTH = 4
CONV_DIM = D_INNER + 2 * SSM_GROUPS * D_STATE
CHUNK = 128
PLE_DIM = 256
LN_EPS = 1e-5
RMS_EPS = 1e-5
Q_END = ATT_QKV
K_END = Q_END + ATT_QKV
V_END = K_END + ATT_QKV
GATT_END = V_END + ATT_OUT
Z_END = GATT_END + D_INNER
XBC_END = Z_END + CONV_DIM
DT_END = XBC_END + SSM_HEADS
GMERGE_END = DT_END + 2 * D_MODEL

HEAD_PAIRS = HEADS_PER_GROUP // 2
QKV_TILES = 3 * HEAD_PAIRS
GROUP_LANES = D_INNER // SSM_GROUPS
HEADS_PER_SSM_GROUP = SSM_HEADS // SSM_GROUPS
PROJ_ROWS = 512
OUT_ROWS = 256


def _sigmoid(v):
    return 1.0 / (1.0 + jnp.exp(-v))


def _split3(v):
    hi = v.astype(BF16)
    r1 = v - hi.astype(F32)
    mid = r1.astype(BF16)
    lo = (r1 - mid.astype(F32)).astype(BF16)
    return hi, mid, lo


def _dot(a, b):
    return jnp.dot(a, b, preferred_element_type=F32)


def _dot_nt(a, b):
    return lax.dot_general(a, b, (((1,), (1,)), ((), ())), preferred_element_type=F32)


def _dot_tn(a, b):
    return lax.dot_general(a, b, (((0,), (0,)), ((), ())), preferred_element_type=F32)


def _dot_exact_rhs(a_f32, b_bf16):
    hi, mid, lo = _split3(a_f32)
    return _dot(hi, b_bf16) + _dot(mid, b_bf16) + _dot(lo, b_bf16)


def _proj_kernel(x_ref, w_ref, o_ref, *, n_res, tiled_out):
    parts = [x_ref[0, :, r * D_MODEL:(r + 1) * D_MODEL].astype(BF16) for r in range(n_res)]
    lhs = parts[0] if n_res == 1 else jnp.concatenate(parts, axis=0)
    res = _dot(lhs, w_ref[...])
    if tiled_out:
        for c in range(o_ref.shape[1]):
            o_ref[0, c] = res[:, c * LANES:(c + 1) * LANES].astype(o_ref.dtype)
    else:
        o_ref[0] = res.astype(o_ref.dtype)


def _project(x, w, *, dilation, tiled_out, out_dtype, name):
    b, s, d = x.shape
    n = w.shape[1]
    sub = s // dilation
    xv = x.reshape(b, sub, dilation * d)
    rows = min(PROJ_ROWS, sub)
    n_res = PROJ_ROWS // rows
    steps = s // PROJ_ROWS
    if dilation == 1:
        x_spec = pl.BlockSpec((1, rows, d), lambda i, j: (i, j, 0))
    else:
        x_spec = pl.BlockSpec((1, rows, n_res * d), lambda i, j: (i, 0, j))
    if tiled_out:
        out_shape = jax.ShapeDtypeStruct((b, n // LANES, s, LANES), out_dtype)
        o_spec = pl.BlockSpec((1, n // LANES, PROJ_ROWS, LANES), lambda i, j: (i, 0, j, 0))
    else:
        out_shape = jax.ShapeDtypeStruct((b, s, n), out_dtype)
        o_spec = pl.BlockSpec((1, PROJ_ROWS, n), lambda i, j: (i, j, 0))
    return pl.pallas_call(
        functools.partial(_proj_kernel, n_res=n_res, tiled_out=tiled_out),
        grid=(b, steps),
        in_specs=[x_spec, pl.BlockSpec((d, n), lambda i, j: (0, 0))],
        out_specs=o_spec,
        out_shape=out_shape,
        compiler_params=pltpu.CompilerParams(
            dimension_semantics=("parallel", "parallel"), vmem_limit_bytes=VMEM_LIMIT),
        name=name,
    )(xv, w)


def _bucket_tiles():
    qi = np.arange(BLK)[:, None]
    kj = np.arange(2 * BLK)[None, :]
    delta = qi + BLK - kj
    max_exact = NUM_BUCKETS // 2
    out = []
    for window, dil in PATTERNS:
        valid = (delta >= 0) & (delta <= window // dil)
        dist = np.maximum(delta, 0) * dil
        d_f = np.maximum(dist, 1).astype(np.float32)
        large = max_exact + (np.log(d_f / max_exact) / np.float32(math.log(MAX_DISTANCE / max_exact))
                             * (NUM_BUCKETS - max_exact)).astype(np.int32)
        large = np.minimum(large, NUM_BUCKETS - 1)
        bucket = np.where(dist < max_exact, dist, large)
        out.append(np.where(valid, bucket, -1).astype(np.int32))
    return np.stack(out)


def _bias_kernel(tbl_ref, bucket_ref, o_ref):
    col = pl.program_id(0) * HEADS_PER_GROUP + pl.program_id(1)
    bucket = bucket_ref[0]
    acc = jnp.full(bucket.shape, -jnp.inf, F32)
    for bkt in range(NUM_BUCKETS):
        acc = jnp.where(bucket == bkt, tbl_ref[bkt, col], acc)
    o_ref[0, 0] = acc


def _bias_tiles(rel_bias):
    buckets = jnp.asarray(_bucket_tiles())
    return pl.pallas_call(
        _bias_kernel,
        grid=(N_GROUPS, HEADS_PER_GROUP),
        in_specs=[pl.BlockSpec(memory_space=pltpu.SMEM),
                  pl.BlockSpec((1, BLK, 2 * BLK), lambda g, h: (g, 0, 0))],
        out_specs=pl.BlockSpec((1, 1, BLK, 2 * BLK), lambda g, h: (g, h, 0, 0)),
        out_shape=jax.ShapeDtypeStruct((N_GROUPS, HEADS_PER_GROUP, BLK, 2 * BLK), F32),
        name="bias_tiles",
    )(rel_bias, buckets)


def _head_stack(t):
    lane = lax.broadcasted_iota(jnp.int32, t.shape, 1)
    zero = jnp.zeros_like(t)
    return jnp.concatenate([jnp.where(lane < HEAD_DIM, t, zero),
                            jnp.where(lane >= HEAD_DIM, t, zero)], axis=0)


def _attn_kernel(q1, k1, v1, q2, k2, v2, q3, k3, v3, bias_ref, gate_ref, o_ref,
                 m_scr, l_scr, acc_scr, *, seq):
    lane_o = lax.broadcasted_iota(jnp.int32, (BLK, LANES), 1)
    left = lane_o < HEAD_DIM

    def block(refs, g, row0, has_prev, dst, first):
        q_ref, k_ref, v_ref = refs
        q = q_ref[0, 0, pl.ds(row0, BLK), :]
        if has_prev:
            kk = k_ref[0, 0, pl.ds(row0 - BLK, 2 * BLK), :]
            vv = v_ref[0, 0, pl.ds(row0 - BLK, 2 * BLK), :]
        else:
            kk = k_ref[0, 0, pl.ds(row0, BLK), :]
            vv = v_ref[0, 0, pl.ds(row0, BLK), :]
        nk = kk.shape[0]
        s = _dot_nt(q, _head_stack(kk))
        es, ms, ls = [], [], []
        for h in range(2):
            bias = bias_ref[g, h] if has_prev else bias_ref[g, h, :, BLK:]
            sh = s[:, h * nk:(h + 1) * nk] + bias
            mh = jnp.max(sh, axis=-1, keepdims=True)
            eh = jnp.exp(sh - mh)
            es.append(eh.astype(BF16))
            ms.append(mh)
            ls.append(jnp.sum(eh, axis=-1, keepdims=True))
        pv = _dot(jnp.concatenate(es, axis=1), _head_stack(vv))
        m_new = jnp.where(left, ms[0], ms[1])
        l_new = jnp.where(left, ls[0], ls[1])
        if first:
            m_scr[dst, :] = m_new
            l_scr[dst, :] = l_new
            acc_scr[dst, :] = pv
        else:
            m_old = m_scr[dst, :]
            m_tot = jnp.maximum(m_old, m_new)
            a_old = jnp.exp(m_old - m_tot)
            a_new = jnp.exp(m_new - m_tot)
            m_scr[dst, :] = m_tot
            l_scr[dst, :] = a_old * l_scr[dst, :] + a_new * l_new
            acc_scr[dst, :] = a_old * acc_scr[dst, :] + a_new * pv

    group_refs = ((q1, k1, v1), (q2, k2, v2), (q3, k3, v3))
    for g, (_, dil) in enumerate(PATTERNS):
        refs = group_refs[g]
        sub = seq // dil
        nb = sub // BLK
        first = g == 0

        def residue(r, carry, refs=refs, g=g, dil=dil, sub=sub, nb=nb, first=first):
            def dst_rows(n):
                if dil == 1:
                    return pl.ds(pl.multiple_of(n * BLK, BLK), BLK)
                return pl.ds(n * BLK * dil + r, BLK, stride=dil)

            base = pl.multiple_of(r * sub, BLK)
            block(refs, g, base, False, dst_rows(0), first)
            if nb > 1:
                def band(n, c2):
                    block(refs, g, pl.multiple_of(base + n * BLK, BLK), True, dst_rows(n), first)
                    return c2
                lax.fori_loop(1, nb, band, 0)
            return carry

        lax.fori_loop(0, dil, residue, 0)

    gate = gate_ref[0]
    o = acc_scr[...] / l_scr[...]
    o_ref[0, 0] = (o * (gate * _sigmoid(gate))).astype(o_ref.dtype)


def _attention(qkv, bias, gate_dt):
    b, _, s, _ = qkv[0].shape
    in_specs, args = [], []
    for arr in qkv:
        for part in range(3):
            in_specs.append(pl.BlockSpec((1, 1, s, LANES),
                                         lambda i, j, part=part: (i, part * HEAD_PAIRS + j, 0, 0)))
            args.append(arr)
    in_specs.append(pl.BlockSpec((N_GROUPS, 2, BLK, 2 * BLK), lambda i, j: (0, j, 0, 0)))
    in_specs.append(pl.BlockSpec((1, s, LANES), lambda i, j: (i, 0, j)))
    return pl.pallas_call(
        functools.partial(_attn_kernel, seq=s),
        grid=(b, HEAD_PAIRS),
        in_specs=in_specs,
        out_specs=pl.BlockSpec((1, 1, s, LANES), lambda i, j: (i, j, 0, 0)),
        out_shape=jax.ShapeDtypeStruct((b, HEAD_PAIRS, s, LANES), BF16),
        scratch_shapes=[pltpu.VMEM((s, LANES), F32)] * 3,
        compiler_params=pltpu.CompilerParams(
            dimension_semantics=("parallel", "parallel"), vmem_limit_bytes=VMEM_LIMIT),
        name="dilated_attention",
    )(*args, bias, gate_dt)


def _ssd_kernel(xbc_ref, z_ref, dt_ref, convw_ref, convb_ref, dtb_ref, alog_ref, dskip_ref,
                normw_ref, expand_ref, o_ref, state_scr, xpad_scr):
    c = pl.program_id(1)
    halo = SUBLANES

    @pl.when(c == 0)
    def _():
        state_scr[...] = jnp.zeros_like(state_scr)
        xpad_scr[0:halo, :] = jnp.zeros((halo, CONV_DIM), F32)

    xpad_scr[halo:halo + CHUNK, :] = xbc_ref[0]
    conv = jnp.broadcast_to(convb_ref[...], (CHUNK, CONV_DIM))
    for k in range(CONV_WIDTH):
        off = halo - (CONV_WIDTH - 1) + k
        conv = conv + convw_ref[k:k + 1, :] * xpad_scr[off:off + CHUNK, :]
    xpad_scr[0:halo, :] = xbc_ref[0, CHUNK - halo:CHUNK, :]
    act = conv * _sigmoid(conv)
    xs = act[:, :D_INNER]
    bm = act[:, D_INNER:D_INNER + SSM_GROUPS * D_STATE]
    cm = act[:, D_INNER + SSM_GROUPS * D_STATE:]

    raw = dt_ref[0] + dtb_ref[...]
    dt = jnp.maximum(raw, 0.0) + jnp.log1p(jnp.exp(-jnp.abs(raw)))
    a = dt * (-jnp.exp(alog_ref[...]))
    row = lax.broadcasted_iota(jnp.int32, (CHUNK, CHUNK), 0)
    col = lax.broadcasted_iota(jnp.int32, (CHUNK, CHUNK), 1)
    causal = row >= col
    tril = jnp.where(causal, 1.0, 0.0).astype(BF16)
    hi, mid, lo = _split3(a)
    acs = _dot(tril, hi) + _dot(tril, mid) + _dot(tril, lo)
    acs_t = acs.T

    expand = expand_ref[...]
    dt_full = _dot_exact_rhs(dt, expand)
    acs_full = _dot_exact_rhs(acs, expand)
    last = acs_full[CHUNK - 1:CHUNK, :]
    decay_in = jnp.exp(acs_full)
    decay_out = jnp.exp(last - acs_full)
    chunk_decay = jnp.exp(last)
    xdt = xs * dt_full
    x_out = (xdt * decay_out).astype(BF16)
    xdt_b = xdt.astype(BF16)

    ys = []
    for g in range(SSM_GROUPS):
        gl = slice(g * GROUP_LANES, (g + 1) * GROUP_LANES)
        bg = bm[:, g * D_STATE:(g + 1) * D_STATE].astype(BF16)
        cg = cm[:, g * D_STATE:(g + 1) * D_STATE].astype(BF16)
        cb = _dot_nt(cg, bg)
        prev = state_scr[g]
        y_off = _dot(cg, prev.astype(BF16)) * decay_in[:, gl]
        state_scr[g] = prev * chunk_decay[:, gl] + _dot_tn(bg, x_out[:, gl])
        pairs = []
        for j in range(HEADS_PER_SSM_GROUP // 2):
            gs = []
            for hh in (g * HEADS_PER_SSM_GROUP + 2 * j, g * HEADS_PER_SSM_GROUP + 2 * j + 1):
                seg = acs[:, hh:hh + 1] - acs_t[hh:hh + 1, :]
                lmat = jnp.exp(jnp.where(causal, seg, -jnp.inf))
                gs.append((cb * lmat).astype(BF16))
            lanes = slice(g * GROUP_LANES + j * LANES, g * GROUP_LANES + (j + 1) * LANES)
            pairs.append(_dot(jnp.concatenate(gs, axis=1), _head_stack(xdt_b[:, lanes])))
        y_diag = jnp.concatenate(pairs, axis=1)
        ys.append(y_diag + y_off + dskip_ref[:, gl] * xs[:, gl])

    z = z_ref[0]
    outs = []
    for g in range(SSM_GROUPS):
        gl = slice(g * GROUP_LANES, (g + 1) * GROUP_LANES)
        zg = z[:, gl]
        u = ys[g] * (zg * _sigmoid(zg))
        ms = jnp.mean(jnp.square(u), axis=-1, keepdims=True)
        outs.append(u * lax.rsqrt(ms + RMS_EPS) * normw_ref[:, gl])
    o_ref[0] = jnp.concatenate(outs, axis=1).astype(o_ref.dtype)


def _ssd(xbc, z, gate_dt, conv_w, conv_b, dt_bias, a_log, d_skip, norm_w):
    b, s, _ = xbc.shape
    pad = LANES - SSM_HEADS
    dtb = jnp.pad(dt_bias, (0, pad)).reshape(1, LANES)
    alog = jnp.pad(a_log, (0, pad)).reshape(1, LANES)
    dskip = jnp.repeat(d_skip, D_INNER // SSM_HEADS).reshape(1, D_INNER)
    expand = np.zeros((LANES, D_INNER), np.float32)
    expand[np.arange(D_INNER) // (D_INNER // SSM_HEADS), np.arange(D_INNER)] = 1.0
    expand = jnp.asarray(expand, BF16)
    dt_tile = (GATT_END - V_END) // LANES

    def whole(shape):
        return pl.BlockSpec(shape, lambda i, j: (0,) * len(shape))

    return pl.pallas_call(
        _ssd_kernel,
        grid=(b, s // CHUNK),
        in_specs=[pl.BlockSpec((1, CHUNK, CONV_DIM), lambda i, j: (i, j, 0)),
                  pl.BlockSpec((1, CHUNK, D_INNER), lambda i, j: (i, j, 0)),
                  pl.BlockSpec((1, CHUNK, LANES), lambda i, j: (i, j, dt_tile)),
                  whole((CONV_WIDTH, CONV_DIM)), whole((1, CONV_DIM)),
                  whole((1, LANES)), whole((1, LANES)), whole((1, D_INNER)),
                  whole((1, D_INNER)), whole((LANES, D_INNER))],
        out_specs=pl.BlockSpec((1, CHUNK, D_INNER), lambda i, j: (i, j, 0)),
        out_shape=jax.ShapeDtypeStruct((b, s, D_INNER), BF16),
        scratch_shapes=[pltpu.VMEM((SSM_GROUPS, D_STATE, GROUP_LANES), F32),
                        pltpu.VMEM((SUBLANES + CHUNK, CONV_DIM), F32)],
        compiler_params=pltpu.CompilerParams(
            dimension_semantics=("parallel", "arbitrary"), vmem_limit_bytes=VMEM_LIMIT),
        name="ssd_mixer",
    )(xbc, z, gate_dt, conv_w, conv_b.reshape(1, CONV_DIM), dtb, alog, dskip,
      norm_w.reshape(1, D_INNER), expand)


def _out_kernel(att_ref, ssm_ref, gm_ref, gp_ref, p_ref, x_ref, wb_ref, wo_ref, wp_ref,
                bg_ref, lng_ref, lnb_ref, o_ref, *, alpha):
    o_att = jnp.concatenate([att_ref[0, j] for j in range(HEAD_PAIRS)], axis=1)
    y_a = _dot(o_att, wb_ref[:ATT_OUT, :])
    y_b = _dot(ssm_ref[0], wb_ref[ATT_OUT:, :])
    gm = gm_ref[0]
    merged = (_sigmoid(gm[:, :D_MODEL] + bg_ref[0:1, :]) * y_a
              + _sigmoid(gm[:, D_MODEL:] + bg_ref[1:2, :]) * y_b)
    mix = _dot(merged.astype(BF16), wo_ref[...])
    ple = _sigmoid(gp_ref[0] + bg_ref[2:3, :]) * _dot(p_ref[0].astype(BF16), wp_ref[...])
    h = alpha * x_ref[0] + mix + ple
    mu = jnp.mean(h, axis=-1, keepdims=True)
    var = jnp.mean(jnp.square(h - mu), axis=-1, keepdims=True)
    o_ref[0] = (h - mu) * lax.rsqrt(var + LN_EPS) * lng_ref[...] + lnb_ref[...]


def _output(o_att, y_ssm, g_merge, g_ple, p_i, x, w_branch, w_out, w_ple, b_gate, ln_g, ln_b, alpha):
    b, s, d = x.shape
    rows = OUT_ROWS

    def whole(shape):
        return pl.BlockSpec(shape, lambda i, j: (0,) * len(shape))

    def tile(n):
        return pl.BlockSpec((1, rows, n), lambda i, j: (i, j, 0))

    return pl.pallas_call(
        functools.partial(_out_kernel, alpha=alpha),
        grid=(b, s // rows),
        in_specs=[pl.BlockSpec((1, HEAD_PAIRS, rows, LANES), lambda i, j: (i, 0, j, 0)),
                  tile(D_INNER), tile(2 * D_MODEL), tile(D_MODEL), tile(PLE_DIM), tile(D_MODEL),
                  whole((ATT_OUT + D_INNER, D_MODEL)), whole((D_MODEL, D_MODEL)),
                  whole((PLE_DIM, D_MODEL)), whole((3, D_MODEL)), whole((1, D_MODEL)),
                  whole((1, D_MODEL))],
        out_specs=tile(D_MODEL),
        out_shape=jax.ShapeDtypeStruct((b, s, d), F32),
        compiler_params=pltpu.CompilerParams(
            dimension_semantics=("parallel", "parallel"), vmem_limit_bytes=VMEM_LIMIT),
        name="merge_out_ln",
    )(o_att, y_ssm, g_merge, g_ple, p_i, x, w_branch.astype(BF16), w_out.astype(BF16),
      w_ple.astype(BF16), b_gate, ln_g.reshape(1, d), ln_b.reshape(1, d))


def _layer(x, p_i, w_in, b_gate, conv_w, conv_b, dt_bias, a_log, d_skip, ssm_norm_w,
           w_branch, w_out, w_ple, ln_g, ln_b, bias, alpha):
    w = w_in.astype(BF16)
    scale = HEAD_DIM ** -0.5
    group_w = HEADS_PER_GROUP * HEAD_DIM
    qkv = []
    for g, (_, dil) in enumerate(PATTERNS):
        cols = slice(g * group_w, (g + 1) * group_w)
        wg = jnp.concatenate([w[:, :Q_END][:, cols] * scale, w[:, Q_END:K_END][:, cols],
                              w[:, K_END:V_END][:, cols]], axis=1)
        qkv.append(_project(x, wg, dilation=dil, tiled_out=True, out_dtype=BF16, name=f"proj_qkv{g}"))
    w_gate_dt = jnp.concatenate(
        [w[:, V_END:GATT_END], jnp.pad(w[:, XBC_END:DT_END], ((0, 0), (0, LANES - SSM_HEADS)))], axis=1)
    gate_dt = _project(x, w_gate_dt, dilation=1, tiled_out=False, out_dtype=F32, name="proj_gate_dt")
    z = _project(x, w[:, GATT_END:Z_END], dilation=1, tiled_out=False, out_dtype=F32, name="proj_z")
    xbc = _project(x, w[:, Z_END:XBC_END], dilation=1, tiled_out=False, out_dtype=F32, name="proj_xbc")
    g_merge = _project(x, w[:, DT_END:GMERGE_END], dilation=1, tiled_out=False, out_dtype=F32,
                       name="proj_gmerge")
    g_ple = _project(x, w[:, GMERGE_END:], dilation=1, tiled_out=False, out_dtype=F32, name="proj_gple")

    o_att = _attention(qkv, bias, gate_dt)
    y_ssm = _ssd(xbc, z, gate_dt, conv_w, conv_b, dt_bias, a_log, d_skip, ssm_norm_w)
    return _output(o_att, y_ssm, g_merge, g_ple, p_i, x, w_branch, w_out, w_ple, b_gate,
                   ln_g, ln_b, alpha)


def kernel(x, p, w_in, b_gate, conv_w, conv_b, dt_bias, a_log, d_skip, ssm_norm_w, w_branch, w_out,
           w_ple, ln_g, ln_b, rel_bias):
    depth = w_in.shape[0]
    alpha = (2.0 * depth) ** 0.25
    bias = _bias_tiles(rel_bias)
    for i in range(depth):
        x = _layer(x, p[i], w_in[i], b_gate[i], conv_w[i], conv_b[i], dt_bias[i], a_log[i],
                   d_skip[i], ssm_norm_w[i], w_branch[i], w_out[i], w_ple[i], ln_g[i], ln_b[i],
                   bias, alpha)
    return x
```

```python
import functools
import math

import numpy as np
import jax
import jax.numpy as jnp
from jax import lax
from jax.experimental import pallas as pl
from jax.experimental.pallas import tpu as pltpu

F32 = jnp.float32
BF16 = jnp.bfloat16

LANES = 128
SUBLANES = 8
V7X_VMEM_BYTES = 64 * 1024 * 1024
VMEM_LIMIT = (V7X_VMEM_BYTES * 3) // 4

D_MODEL = 1024
HEAD_DIM = 64
HEADS_PER_GROUP = 12
PATTERNS = ((128, 1), (512, 4), (2048, 16))
N_GROUPS = 3
ATT_HEADS = N_GROUPS * HEADS_PER_GROUP
ATT_QKV = ATT_HEADS * HEAD_DIM
ATT_OUT = HEADS_PER_GROUP * HEAD_DIM
BLK = 128
NUM_BUCKETS = 32
MAX_DISTANCE = 2048
D_INNER = 2048
SSM_HEADS = 32
SSM_GROUPS = 4
D_STATE = 128
CONV_WIDTH = 4
CONV_DIM = D_INNER + 2 * SSM_GROUPS * D_STATE
CHUNK = 128
PLE_DIM = 256
LN_EPS = 1e-5
RMS_EPS = 1e-5
Q_END = ATT_QKV
K_END = Q_END + ATT_QKV
V_END = K_END + ATT_QKV
GATT_END = V_END + ATT_OUT
Z_END = GATT_END + D_INNER
XBC_END = Z_END + CONV_DIM
DT_END = XBC_END + SSM_HEADS
GMERGE_END = DT_END + 2 * D_MODEL

HEAD_PAIRS = HEADS_PER_GROUP // 2
QKV_TILES = 3 * HEAD_PAIRS
GROUP_LANES = D_INNER // SSM_GROUPS
HEADS_PER_SSM_GROUP = SSM_HEADS // SSM_GROUPS
PROJ_ROWS = 512
OUT_ROWS = 256


def _sigmoid(v):
    return 1.0 / (1.0 + jnp.exp(-v))


def _split3(v):
    hi = v.astype(BF16)
    r1 = v - hi.astype(F32)
    mid = r1.astype(BF16)
    lo = (r1 - mid.astype(F32)).astype(BF16)
    return hi, mid, lo


def _dot(a, b):
    return jnp.dot(a, b, preferred_element_type=F32)


def _dot_nt(a, b):
    return lax.dot_general(a, b, (((1,), (1,)), ((), ())), preferred_element_type=F32)


def _dot_tn(a, b):
    return lax.dot_general(a, b, (((0,), (0,)), ((), ())), preferred_element_type=F32)


def _dot_exact_rhs(a_f32, b_bf16):
    hi, mid, lo = _split3(a_f32)
    return _dot(hi, b_bf16) + _dot(mid, b_bf16) + _dot(lo, b_bf16)


def _proj_kernel(x_ref, w_ref, o_ref, *, n_res, tiled_out):
    parts = [x_ref[0, :, r * D_MODEL:(r + 1) * D_MODEL].astype(BF16) for r in range(n_res)]
    lhs = parts[0] if n_res == 1 else jnp.concatenate(parts, axis=0)
    res = _dot(lhs, w_ref[...])
    if tiled_out:
        for c in range(o_ref.shape[1]):
            o_ref[0, c] = res[:, c * LANES:(c + 1) * LANES].astype(o_ref.dtype)
    else:
        o_ref[0] = res.astype(o_ref.dtype)


def _project(x, w, *, dilation, tiled_out, out_dtype, name):
    b, s, d = x.shape
    n = w.shape[1]
    sub = s // dilation
    xv = x.reshape(b, sub, dilation * d)
    rows = min(PROJ_ROWS, sub)
    n_res = PROJ_ROWS // rows
    steps = s // PROJ_ROWS
    if dilation == 1:
        x_spec = pl.BlockSpec((1, rows, d), lambda i, j: (i, j, 0))
    else:
        x_spec = pl.BlockSpec((1, rows, n_res * d), lambda i, j: (i, 0, j))
    if tiled_out:
        out_shape = jax.ShapeDtypeStruct((b, n // LANES, s, LANES), out_dtype)
        o_spec = pl.BlockSpec((1, n // LANES, PROJ_ROWS, LANES), lambda i, j: (i, 0, j, 0))
    else:
        out_shape = jax.ShapeDtypeStruct((b, s, n), out_dtype)
        o_spec = pl.BlockSpec((1, PROJ_ROWS, n), lambda i, j: (i, j, 0))
    return pl.pallas_call(
        functools.partial(_proj_kernel, n_res=n_res, tiled_out=tiled_out),
        grid=(b, steps),
        in_specs=[x_spec, pl.BlockSpec((d, n), lambda i, j: (0, 0))],
        out_specs=o_spec,
        out_shape=out_shape,
        compiler_params=pltpu.CompilerParams(
            dimension_semantics=("parallel", "parallel"), vmem_limit_bytes=VMEM_LIMIT),
        name=name,
    )(xv, w)


BIAS_VARIANTS = N_GROUPS + 1
FIRST_BLOCK_VARIANT = N_GROUPS


def _bucket_tiles():
    qi = np.arange(BLK)[:, None]
    kj = np.arange(2 * BLK)[None, :]
    delta = qi + BLK - kj
    max_exact = NUM_BUCKETS // 2
    out = []
    for window, dil in PATTERNS:
        valid = (delta >= 0) & (delta <= window // dil)
        dist = np.maximum(delta, 0) * dil
        d_f = np.maximum(dist, 1).astype(np.float32)
        large = max_exact + (np.log(d_f / max_exact) / np.float32(math.log(MAX_DISTANCE / max_exact))
                             * (NUM_BUCKETS - max_exact)).astype(np.int32)
        large = np.minimum(large, NUM_BUCKETS - 1)
        bucket = np.where(dist < max_exact, dist, large)
        out.append(np.where(valid, bucket, -1).astype(np.int32))
    out.append(np.where(kj < BLK, -1, out[0]).astype(np.int32))
    return np.stack(out)


def _bias_kernel(tbl_ref, bucket_ref, o_ref):
    col = (pl.program_id(0) % N_GROUPS) * HEADS_PER_GROUP + pl.program_id(1)
    bucket = bucket_ref[0]
    acc = jnp.full(bucket.shape, -jnp.inf, F32)
    for bkt in range(NUM_BUCKETS):
        acc = jnp.where(bucket == bkt, tbl_ref[bkt, col], acc)
    o_ref[0, 0] = acc


def _bias_tiles(rel_bias):
    buckets = jnp.asarray(_bucket_tiles())
    return pl.pallas_call(
        _bias_kernel,
        grid=(BIAS_VARIANTS, HEADS_PER_GROUP),
        in_specs=[pl.BlockSpec(memory_space=pltpu.SMEM),
                  pl.BlockSpec((1, BLK, 2 * BLK), lambda g, h: (g, 0, 0))],
        out_specs=pl.BlockSpec((1, 1, BLK, 2 * BLK), lambda g, h: (g, h, 0, 0)),
        out_shape=jax.ShapeDtypeStruct((BIAS_VARIANTS, HEADS_PER_GROUP, BLK, 2 * BLK), F32),
        name="bias_tiles",
    )(rel_bias, buckets)


ATT_UNROLL = 4


def _head_stack(t):
    lane = lax.broadcasted_iota(jnp.int32, t.shape, 1)
    zero = jnp.zeros_like(t)
    return jnp.concatenate([jnp.where(lane < HEAD_DIM, t, zero),
                            jnp.where(lane >= HEAD_DIM, t, zero)], axis=0)


def _attn_kernel(q1, k1, v1, q2, k2, v2, q3, k3, v3, bias_ref, gate_ref, o_ref,
                 m_scr, l_scr, acc_scr, *, seq):
    lane_o = lax.broadcasted_iota(jnp.int32, (BLK, LANES), 1)
    left = lane_o < HEAD_DIM

    def band_blocks(jobs):
        rows = lambda ref, r0: ref[0, 0, pl.ds(r0, BLK), :]
        scores = []
        for (q_ref, k_ref, _), row0, prev_row0, _ in jobs:
            kk = rows(k_ref, row0)
            if prev_row0 is not None:
                kk = jnp.concatenate([rows(k_ref, prev_row0), kk], axis=0)
            scores.append(_dot_nt(rows(q_ref, row0), _head_stack(kk)))
        probs, stats = [], []
        for s, (_, _, _, bias_of_head) in zip(scores, jobs):
            nk = s.shape[1] // 2
            es, ms, ls = [], [], []
            for h in range(2):
                sh = s[:, h * nk:(h + 1) * nk] + bias_of_head(h)
                mh = jnp.max(sh, axis=-1, keepdims=True)
                eh = jnp.exp(sh - mh)
                es.append(eh.astype(BF16))
                ms.append(mh)
                ls.append(jnp.sum(eh, axis=-1, keepdims=True))
            probs.append(jnp.concatenate(es, axis=1))
            stats.append((jnp.where(left, ms[0], ms[1]), jnp.where(left, ls[0], ls[1])))
        out = []
        for p, (m_new, l_new), ((_, _, v_ref), row0, prev_row0, _) in zip(probs, stats, jobs):
            vv = rows(v_ref, row0)
            if prev_row0 is not None:
                vv = jnp.concatenate([rows(v_ref, prev_row0), vv], axis=0)
            out.append((_dot(p, _head_stack(vv)), m_new, l_new))
        return out

    def merge(dst, first, pv, m_new, l_new):
        if first:
            m_scr[dst, :] = m_new
            l_scr[dst, :] = l_new
            acc_scr[dst, :] = pv
            return
        m_old = m_scr[dst, :]
        diff = m_old - m_new
        w = jnp.exp(-jnp.abs(diff))
        keep_old = diff >= 0.0
        a_old = jnp.where(keep_old, 1.0, w)
        a_new = jnp.where(keep_old, w, 1.0)
        m_scr[dst, :] = jnp.maximum(m_old, m_new)
        l_scr[dst, :] = a_old * l_scr[dst, :] + a_new * l_new
        acc_scr[dst, :] = a_old * acc_scr[dst, :] + a_new * pv

    refs0 = (q1, k1, v1)

    def natural_blocks(i, carry):
        jobs, dsts = [], []
        for u in range(ATT_UNROLL):
            row0 = pl.multiple_of((i * ATT_UNROLL + u) * BLK, BLK)
            if u == 0:
                prev = pl.multiple_of(jnp.maximum(row0 - BLK, 0), BLK)
                variant = jnp.where(i == 0, FIRST_BLOCK_VARIANT, 0)
                bias_of_head = lambda h, variant=variant: bias_ref[variant, h]
            else:
                prev = pl.multiple_of(row0 - BLK, BLK)
                bias_of_head = lambda h: bias_ref[0, h]
            jobs.append((refs0, row0, prev, bias_of_head))
            dsts.append(pl.ds(row0, BLK))
        for dst, res in zip(dsts, band_blocks(jobs)):
            merge(dst, True, *res)
        return carry

    lax.fori_loop(0, seq // (BLK * ATT_UNROLL), natural_blocks, 0)

    for g, refs in ((1, (q2, k2, v2)), (2, (q3, k3, v3))):
        dil = PATTERNS[g][1]
        sub = seq // dil
        nb = sub // BLK
        classes_per_iter = max(ATT_UNROLL // nb, 1)

        def dilated_blocks(i, carry, g=g, refs=refs, dil=dil, sub=sub, nb=nb,
                           classes_per_iter=classes_per_iter):
            jobs, dsts = [], []
            for cc in range(classes_per_iter):
                r = i * classes_per_iter + cc
                base = pl.multiple_of(r * sub, BLK)
                for n in range(nb):
                    row0 = pl.multiple_of(base + n * BLK, BLK)
                    if n == 0:
                        jobs.append((refs, row0, None, lambda h, g=g: bias_ref[g, h, :, BLK:]))
                    else:
                        jobs.append((refs, row0, pl.multiple_of(row0 - BLK, BLK),
                                     lambda h, g=g: bias_ref[g, h]))
                    dsts.append(pl.ds(n * BLK * dil + r, BLK, stride=dil))
            for dst, res in zip(dsts, band_blocks(jobs)):
                merge(dst, False, *res)
            return carry

        lax.fori_loop(0, dil // classes_per_iter, dilated_blocks, 0)

    gate = gate_ref[0]
    o = acc_scr[...] / l_scr[...]
    o_ref[0, 0] = (o * (gate * _sigmoid(gate))).astype(o_ref.dtype)


def _attention(qkv, bias, gate_dt):
    b, _, s, _ = qkv[0].shape
    in_specs, args = [], []
    for arr in qkv:
        for part in range(3):
            in_specs.append(pl.BlockSpec((1, 1, s, LANES),
                                         lambda i, j, part=part: (i, part * HEAD_PAIRS + j, 0, 0)))
            args.append(arr)
    in_specs.append(pl.BlockSpec((BIAS_VARIANTS, 2, BLK, 2 * BLK), lambda i, j: (0, j, 0, 0)))
    in_specs.append(pl.BlockSpec((1, s, LANES), lambda i, j: (i, 0, j)))
    return pl.pallas_call(
        functools.partial(_attn_kernel, seq=s),
        grid=(b, HEAD_PAIRS),
        in_specs=in_specs,
        out_specs=pl.BlockSpec((1, 1, s, LANES), lambda i, j: (i, j, 0, 0)),
        out_shape=jax.ShapeDtypeStruct((b, HEAD_PAIRS, s, LANES), BF16),
        scratch_shapes=[pltpu.VMEM((s, LANES), F32)] * 3,
        compiler_params=pltpu.CompilerParams(
            dimension_semantics=("parallel", "parallel"), vmem_limit_bytes=VMEM_LIMIT),
        name="dilated_attention",
    )(*args, bias, gate_dt)


def _ssd_kernel(xbc_ref, z_ref, dt_ref, convw_ref, convb_ref, dtb_ref, alog_ref, dskip_ref,
                normw_ref, expand_ref, o_ref, state_scr, xpad_scr):
    c = pl.program_id(1)
    halo = SUBLANES

    @pl.when(c == 0)
    def _():
        state_scr[...] = jnp.zeros_like(state_scr)
        xpad_scr[0:halo, :] = jnp.zeros((halo, CONV_DIM), F32)

    xpad_scr[halo:halo + CHUNK, :] = xbc_ref[0]
    conv = jnp.broadcast_to(convb_ref[...], (CHUNK, CONV_DIM))
    for k in range(CONV_WIDTH):
        off = halo - (CONV_WIDTH - 1) + k
        conv = conv + convw_ref[k:k + 1, :] * xpad_scr[off:off + CHUNK, :]
    xpad_scr[0:halo, :] = xbc_ref[0, CHUNK - halo:CHUNK, :]
    act = conv * _sigmoid(conv)
    xs = act[:, :D_INNER]
    bm = act[:, D_INNER:D_INNER + SSM_GROUPS * D_STATE]
    cm = act[:, D_INNER + SSM_GROUPS * D_STATE:]

    raw = dt_ref[0] + dtb_ref[...]
    dt = jnp.maximum(raw, 0.0) + jnp.log1p(jnp.exp(-jnp.abs(raw)))
    a = dt * (-jnp.exp(alog_ref[...]))
    row = lax.broadcasted_iota(jnp.int32, (CHUNK, CHUNK), 0)
    col = lax.broadcasted_iota(jnp.int32, (CHUNK, CHUNK), 1)
    causal = row >= col
    tril = jnp.where(causal, 1.0, 0.0).astype(BF16)
    hi, mid, lo = _split3(a)
    acs = _dot(tril, hi) + _dot(tril, mid) + _dot(tril, lo)
    acs_t = acs.T

    expand = expand_ref[...]
    dt_full = _dot_exact_rhs(dt, expand)
    acs_full = _dot_exact_rhs(acs, expand)
    last = acs_full[CHUNK - 1:CHUNK, :]
    decay_in = jnp.exp(acs_full)
    decay_out = jnp.exp(last - acs_full)
    chunk_decay = jnp.exp(last)
    xdt = xs * dt_full
    x_out = (xdt * decay_out).astype(BF16)
    xdt_b = xdt.astype(BF16)

    ys = []
    for g in range(SSM_GROUPS):
        gl = slice(g * GROUP_LANES, (g + 1) * GROUP_LANES)
        bg = bm[:, g * D_STATE:(g + 1) * D_STATE].astype(BF16)
        cg = cm[:, g * D_STATE:(g + 1) * D_STATE].astype(BF16)
        cb = _dot_nt(cg, bg)
        prev = state_scr[g]
        y_off = _dot(cg, prev.astype(BF16)) * decay_in[:, gl]
        state_scr[g] = prev * chunk_decay[:, gl] + _dot_tn(bg, x_out[:, gl])
        pairs = []
        for j in range(HEADS_PER_SSM_GROUP // 2):
            gs = []
            for hh in (g * HEADS_PER_SSM_GROUP + 2 * j, g * HEADS_PER_SSM_GROUP + 2 * j + 1):
                seg = acs[:, hh:hh + 1] - acs_t[hh:hh + 1, :]
                lmat = jnp.exp(jnp.where(causal, seg, -jnp.inf))
                gs.append((cb * lmat).astype(BF16))
            lanes = slice(g * GROUP_LANES + j * LANES, g * GROUP_LANES + (j + 1) * LANES)
            pairs.append(_dot(jnp.concatenate(gs, axis=1), _head_stack(xdt_b[:, lanes])))
        y_diag = jnp.concatenate(pairs, axis=1)
        ys.append(y_diag + y_off + dskip_ref[:, gl] * xs[:, gl])

    z = z_ref[0]
    outs = []
    for g in range(SSM_GROUPS):
        gl = slice(g * GROUP_LANES, (g + 1) * GROUP_LANES)
        zg = z[:, gl]
        u = ys[g] * (zg * _sigmoid(zg))
        ms = jnp.mean(jnp.square(u), axis=-1, keepdims=True)
        outs.append(u * lax.rsqrt(ms + RMS_EPS) * normw_ref[:, gl])
    o_ref[0] = jnp.concatenate(outs, axis=1).astype(o_ref.dtype)


def _ssd(xbc, z, gate_dt, conv_w, conv_b, dt_bias, a_log, d_skip, norm_w):
    b, s, _ = xbc.shape
    pad = LANES - SSM_HEADS
    dtb = jnp.pad(dt_bias, (0, pad)).reshape(1, LANES)
    alog = jnp.pad(a_log, (0, pad)).reshape(1, LANES)
    dskip = jnp.repeat(d_skip, D_INNER // SSM_HEADS).reshape(1, D_INNER)
    expand = np.zeros((LANES, D_INNER), np.float32)
    expand[np.arange(D_INNER) // (D_INNER // SSM_HEADS), np.arange(D_INNER)] = 1.0
    expand = jnp.asarray(expand, BF16)
    dt_tile = (GATT_END - V_END) // LANES

    def whole(shape):
        return pl.BlockSpec(shape, lambda i, j: (0,) * len(shape))

    return pl.pallas_call(
        _ssd_kernel,
        grid=(b, s // CHUNK),
        in_specs=[pl.BlockSpec((1, CHUNK, CONV_DIM), lambda i, j: (i, j, 0)),
                  pl.BlockSpec((1, CHUNK, D_INNER), lambda i, j: (i, j, 0)),
                  pl.BlockSpec((1, CHUNK, LANES), lambda i, j: (i, j, dt_tile)),
                  whole((CONV_WIDTH, CONV_DIM)), whole((1, CONV_DIM)),
                  whole((1, LANES)), whole((1, LANES)), whole((1, D_INNER)),
                  whole((1, D_INNER)), whole((LANES, D_INNER))],
        out_specs=pl.BlockSpec((1, CHUNK, D_INNER), lambda i, j: (i, j, 0)),
        out_shape=jax.ShapeDtypeStruct((b, s, D_INNER), BF16),
        scratch_shapes=[pltpu.VMEM((SSM_GROUPS, D_STATE, GROUP_LANES), F32),
                        pltpu.VMEM((SUBLANES + CHUNK, CONV_DIM), F32)],
        compiler_params=pltpu.CompilerParams(
            dimension_semantics=("parallel", "arbitrary"), vmem_limit_bytes=VMEM_LIMIT),
        name="ssd_mixer",
    )(xbc, z, gate_dt, conv_w, conv_b.reshape(1, CONV_DIM), dtb, alog, dskip,
      norm_w.reshape(1, D_INNER), expand)


def _out_kernel(att_ref, ssm_ref, gm_ref, gp_ref, p_ref, x_ref, wb_ref, wo_ref, wp_ref,
                bg_ref, lng_ref, lnb_ref, o_ref, *, alpha):
    o_att = jnp.concatenate([att_ref[0, j] for j in range(HEAD_PAIRS)], axis=1)
    y_a = _dot(o_att, wb_ref[:ATT_OUT, :])
    y_b = _dot(ssm_ref[0], wb_ref[ATT_OUT:, :])
    gm = gm_ref[0]
    merged = (_sigmoid(gm[:, :D_MODEL] + bg_ref[0:1, :]) * y_a
              + _sigmoid(gm[:, D_MODEL:] + bg_ref[1:2, :]) * y_b)
    mix = _dot(merged.astype(BF16), wo_ref[...])
    ple = _sigmoid(gp_ref[0] + bg_ref[2:3, :]) * _dot(p_ref[0].astype(BF16), wp_ref[...])
    h = alpha * x_ref[0] + mix + ple
    mu = jnp.mean(h, axis=-1, keepdims=True)
    var = jnp.mean(jnp.square(h - mu), axis=-1, keepdims=True)
    o_ref[0] = (h - mu) * lax.rsqrt(var + LN_EPS) * lng_ref[...] + lnb_ref[...]


def _output(o_att, y_ssm, g_merge, g_ple, p_i, x, w_branch, w_out, w_ple, b_gate, ln_g, ln_b, alpha):
    b, s, d = x.shape
    rows = OUT_ROWS

    def whole(shape):
        return pl.BlockSpec(shape, lambda i, j: (0,) * len(shape))

    def tile(n):
        return pl.BlockSpec((1, rows, n), lambda i, j: (i, j, 0))

    return pl.pallas_call(
        functools.partial(_out_kernel, alpha=alpha),
        grid=(b, s // rows),
        in_specs=[pl.BlockSpec((1, HEAD_PAIRS, rows, LANES), lambda i, j: (i, 0, j, 0)),
                  tile(D_INNER), tile(2 * D_MODEL), tile(D_MODEL), tile(PLE_DIM), tile(D_MODEL),
                  whole((ATT_OUT + D_INNER, D_MODEL)), whole((D_MODEL, D_MODEL)),
                  whole((PLE_DIM, D_MODEL)), whole((3, D_MODEL)), whole((1, D_MODEL)),
                  whole((1, D_MODEL))],
        out_specs=tile(D_MODEL),
        out_shape=jax.ShapeDtypeStruct((b, s, d), F32),
        compiler_params=pltpu.CompilerParams(
            dimension_semantics=("parallel", "parallel"), vmem_limit_bytes=VMEM_LIMIT),
        name="merge_out_ln",
    )(o_att, y_ssm, g_merge, g_ple, p_i, x, w_branch.astype(BF16), w_out.astype(BF16),
      w_ple.astype(BF16), b_gate, ln_g.reshape(1, d), ln_b.reshape(1, d))


def _layer(x, p_i, w_in, b_gate, conv_w, conv_b, dt_bias, a_log, d_skip, ssm_norm_w,
           w_branch, w_out, w_ple, ln_g, ln_b, bias, alpha):
    w = w_in.astype(BF16)
    scale = HEAD_DIM ** -0.5
    group_w = HEADS_PER_GROUP * HEAD_DIM
    qkv = []
    for g, (_, dil) in enumerate(PATTERNS):
        cols = slice(g * group_w, (g + 1) * group_w)
        wg = jnp.concatenate([w[:, :Q_END][:, cols] * scale, w[:, Q_END:K_END][:, cols],
                              w[:, K_END:V_END][:, cols]], axis=1)
        qkv.append(_project(x, wg, dilation=dil, tiled_out=True, out_dtype=BF16, name=f"proj_qkv{g}"))
    w_gate_dt = jnp.concatenate(
        [w[:, V_END:GATT_END], jnp.pad(w[:, XBC_END:DT_END], ((0, 0), (0, LANES - SSM_HEADS)))], axis=1)
    gate_dt = _project(x, w_gate_dt, dilation=1, tiled_out=False, out_dtype=F32, name="proj_gate_dt")
    z = _project(x, w[:, GATT_END:Z_END], dilation=1, tiled_out=False, out_dtype=F32, name="proj_z")
    xbc = _project(x, w[:, Z_END:XBC_END], dilation=1, tiled_out=False, out_dtype=F32, name="proj_xbc")
    g_merge = _project(x, w[:, DT_END:GMERGE_END], dilation=1, tiled_out=False, out_dtype=F32,
                       name="proj_gmerge")
    g_ple = _project(x, w[:, GMERGE_END:], dilation=1, tiled_out=False, out_dtype=F32, name="proj_gple")

    o_att = _attention(qkv, bias, gate_dt)
    y_ssm = _ssd(xbc, z, gate_dt, conv_w, conv_b, dt_bias, a_log, d_skip, ssm_norm_w)
    return _output(o_att, y_ssm, g_merge, g_ple, p_i, x, w_branch, w_out, w_ple, b_gate,
                   ln_g, ln_b, alpha)


def kernel(x, p, w_in, b_gate, conv_w, conv_b, dt_bias, a_log, d_skip, ssm_norm_w, w_branch, w_out,
           w_ple, ln_g, ln_b, rel_bias):
    depth = w_in.shape[0]
    alpha = (2.0 * depth) ** 0.25
    bias = _bias_tiles(rel_bias)
    for i in range(depth):
        x = _layer(x, p[i], w_in[i], b_gate[i], conv_w[i], conv_b[i], dt_bias[i], a_log[i],
                   d_skip[i], ssm_norm_w[i], w_branch[i], w_out[i], w_ple[i], ln_g[i], ln_b[i],
                   bias, alpha)
    return x
```

```python
import functools
import math

import numpy as np
import jax
import jax.numpy as jnp
from jax import lax
from jax.experimental import pallas as pl
from jax.experimental.pallas import tpu as pltpu

F32 = jnp.float32
BF16 = jnp.bfloat16

LANES = 128
SUBLANES = 8
V7X_VMEM_BYTES = 64 * 1024 * 1024
VMEM_LIMIT = (V7X_VMEM_BYTES * 3) // 4

D_MODEL = 1024
HEAD_DIM = 64
HEADS_PER_GROUP = 12
PATTERNS = ((128, 1), (512, 4), (2048, 16))
N_GROUPS = 3
ATT_HEADS = N_GROUPS * HEADS_PER_GROUP
ATT_QKV = ATT_HEADS * HEAD_DIM
ATT_OUT = HEADS_PER_GROUP * HEAD_DIM
BLK = 128
NUM_BUCKETS = 32
MAX_DISTANCE = 2048
D_INNER = 2048
SSM_HEADS = 32
SSM_GROUPS = 4
D_STATE = 128
CONV_WIDTH = 4
CONV_DIM = D_INNER + 2 * SSM_GROUPS * D_STATE
CHUNK = 128
PLE_DIM = 256
LN_EPS = 1e-5
RMS_EPS = 1e-5
Q_END = ATT_QKV
K_END = Q_END + ATT_QKV
V_END = K_END + ATT_QKV
GATT_END = V_END + ATT_OUT
Z_END = GATT_END + D_INNER
XBC_END = Z_END + CONV_DIM
DT_END = XBC_END + SSM_HEADS
GMERGE_END = DT_END + 2 * D_MODEL

HEAD_PAIRS = HEADS_PER_GROUP // 2
QKV_TILES = 3 * HEAD_PAIRS
GROUP_LANES = D_INNER // SSM_GROUPS
HEADS_PER_SSM_GROUP = SSM_HEADS // SSM_GROUPS
PROJ_ROWS = 512
OUT_ROWS = 256


def _sigmoid(v):
    return 1.0 / (1.0 + jnp.exp(-v))


def _split3(v):
    hi = v.astype(BF16)
    r1 = v - hi.astype(F32)
    mid = r1.astype(BF16)
    lo = (r1 - mid.astype(F32)).astype(BF16)
    return hi, mid, lo


def _dot(a, b):
    return jnp.dot(a, b, preferred_element_type=F32)


def _dot_nt(a, b):
    return lax.dot_general(a, b, (((1,), (1,)), ((), ())), preferred_element_type=F32)


def _dot_tn(a, b):
    return lax.dot_general(a, b, (((0,), (0,)), ((), ())), preferred_element_type=F32)


def _dot_exact_rhs(a_f32, b_bf16):
    hi, mid, lo = _split3(a_f32)
    return _dot(hi, b_bf16) + _dot(mid, b_bf16) + _dot(lo, b_bf16)


def _proj_kernel(*refs, permute, tiled_out):
    if permute:
        x_ref, perm_ref, w_ref, o_ref = refs
    else:
        x_ref, w_ref, o_ref = refs
    lhs = x_ref[0].astype(BF16)
    if permute:
        lhs = _dot(perm_ref[...], lhs).astype(BF16)
    res = _dot(lhs, w_ref[...])
    if tiled_out:
        _, n_tiles, n_classes, _, sub, _ = o_ref.shape
        for c in range(n_tiles):
            tile = res[:, c * LANES:(c + 1) * LANES].reshape(n_classes, sub, LANES)
            o_ref[0, c, :, 0] = tile.astype(o_ref.dtype)
    else:
        o_ref[0] = res.astype(o_ref.dtype)


def _class_permutation(dilation):
    sub = PROJ_ROWS // dilation
    dst = np.arange(PROJ_ROWS)
    src = dilation * (dst % sub) + dst // sub
    perm = np.zeros((PROJ_ROWS, PROJ_ROWS), np.float32)
    perm[dst, src] = 1.0
    return jnp.asarray(perm, BF16)


def _project(x, w, *, dilation, tiled_out, out_dtype, name):
    b, s, d = x.shape
    n = w.shape[1]
    steps = s // PROJ_ROWS
    permute = dilation > 1
    in_specs = [pl.BlockSpec((1, PROJ_ROWS, d), lambda i, j: (i, j, 0))]
    args = [x]
    if permute:
        in_specs.append(pl.BlockSpec((PROJ_ROWS, PROJ_ROWS), lambda i, j: (0, 0)))
        args.append(_class_permutation(dilation))
    in_specs.append(pl.BlockSpec((d, n), lambda i, j: (0, 0)))
    args.append(w)
    if tiled_out:
        sub = PROJ_ROWS // dilation
        out_shape = jax.ShapeDtypeStruct((b, n // LANES, dilation, steps, sub, LANES), out_dtype)
        o_spec = pl.BlockSpec((1, n // LANES, dilation, 1, sub, LANES), lambda i, j: (i, 0, 0, j, 0, 0))
    else:
        assert not permute
        out_shape = jax.ShapeDtypeStruct((b, s, n), out_dtype)
        o_spec = pl.BlockSpec((1, PROJ_ROWS, n), lambda i, j: (i, j, 0))
    out = pl.pallas_call(
        functools.partial(_proj_kernel, permute=permute, tiled_out=tiled_out),
        grid=(b, steps),
        in_specs=in_specs,
        out_specs=o_spec,
        out_shape=out_shape,
        compiler_params=pltpu.CompilerParams(
            dimension_semantics=("parallel", "parallel"), vmem_limit_bytes=VMEM_LIMIT),
        name=name,
    )(*args)
    return out.reshape(b, n // LANES, s, LANES) if tiled_out else out


def _bucket_tiles():
    qi = np.arange(BLK)[:, None]
    kj = np.arange(2 * BLK)[None, :]
    delta = qi + BLK - kj
    max_exact = NUM_BUCKETS // 2
    out = []
    for window, dil in PATTERNS:
        valid = (delta >= 0) & (delta <= window // dil)
        dist = np.maximum(delta, 0) * dil
        d_f = np.maximum(dist, 1).astype(np.float32)
        large = max_exact + (np.log(d_f / max_exact) / np.float32(math.log(MAX_DISTANCE / max_exact))
                             * (NUM_BUCKETS - max_exact)).astype(np.int32)
        large = np.minimum(large, NUM_BUCKETS - 1)
        bucket = np.where(dist < max_exact, dist, large)
        out.append(np.where(valid, bucket, -1).astype(np.int32))
    return np.stack(out)


def _bias_kernel(tbl_ref, bucket_ref, o_ref):
    col = pl.program_id(0) * HEADS_PER_GROUP + pl.program_id(1)
    bucket = bucket_ref[0]
    acc = jnp.full(bucket.shape, -jnp.inf, F32)
    for bkt in range(NUM_BUCKETS):
        acc = jnp.where(bucket == bkt, tbl_ref[bkt, col], acc)
    o_ref[0, 0] = acc


def _bias_tiles(rel_bias):
    buckets = jnp.asarray(_bucket_tiles())
    return pl.pallas_call(
        _bias_kernel,
        grid=(N_GROUPS, HEADS_PER_GROUP),
        in_specs=[pl.BlockSpec(memory_space=pltpu.SMEM),
                  pl.BlockSpec((1, BLK, 2 * BLK), lambda g, h: (g, 0, 0))],
        out_specs=pl.BlockSpec((1, 1, BLK, 2 * BLK), lambda g, h: (g, h, 0, 0)),
        out_shape=jax.ShapeDtypeStruct((N_GROUPS, HEADS_PER_GROUP, BLK, 2 * BLK), F32),
        name="bias_tiles",
    )(rel_bias, buckets)


ATT_DEPTH = 3


def _head_stack(t):
    lane = lax.broadcasted_iota(jnp.int32, t.shape, 1)
    zero = jnp.zeros_like(t)
    return jnp.concatenate([jnp.where(lane < HEAD_DIM, t, zero),
                            jnp.where(lane >= HEAD_DIM, t, zero)], axis=0)


def _attn_kernel(q1, k1, v1, q2, k2, v2, q3, k3, v3, bias_ref, gate_ref, o_ref,
                 m_scr, l_scr, acc_scr, *, seq):
    left = lax.broadcasted_iota(jnp.int32, (BLK, LANES), 1) < HEAD_DIM
    ones_rows = jnp.ones((2 * BLK, LANES), BF16)
    r_idx = lax.broadcasted_iota(jnp.int32, (2 * BLK, LANES), 0)
    c_idx = lax.broadcasted_iota(jnp.int32, (2 * BLK, LANES), 1)
    ones_heads = jnp.where((r_idx < BLK) == (c_idx < HEAD_DIM), 1.0, 0.0).astype(BF16)

    jobs = []
    for g, (_, dil) in enumerate(PATTERNS):
        sub = seq // dil
        for r in range(dil):
            for n in range(sub // BLK):
                dst = pl.ds(n * BLK * dil + r, BLK, stride=dil) if dil > 1 else pl.ds(n * BLK, BLK)
                jobs.append((g, r * sub + n * BLK, n > 0, dst))
    group_refs = ((q1, k1, v1), (q2, k2, v2), (q3, k3, v3))

    def key_rows(ref, row0, has_prev):
        if has_prev:
            return ref[0, 0, row0 - BLK:row0 + BLK, :]
        return ref[0, 0, row0:row0 + BLK, :]

    def scores_of(job):
        g, row0, has_prev, _ = job
        q_ref, k_ref, _ = group_refs[g]
        q = q_ref[0, 0, row0:row0 + BLK, :]
        kk = key_rows(k_ref, row0, has_prev)
        if has_prev:
            return _dot_nt(_head_stack(q), kk)
        return _dot_nt(q, _head_stack(kk))

    def softmax_of(job, s):
        g, _, has_prev, _ = job
        es, ms = [], []
        for h in range(2):
            if has_prev:
                sh = s[h * BLK:(h + 1) * BLK, :] + bias_ref[g, h]
            else:
                sh = s[:, h * BLK:(h + 1) * BLK] + bias_ref[g, h, :, BLK:]
            mh = jnp.max(sh, axis=-1, keepdims=True)
            es.append(jnp.exp(sh - mh).astype(BF16))
            ms.append(mh)
        return jnp.concatenate(es, axis=0 if has_prev else 1), jnp.where(left, ms[0], ms[1])

    def output_of(job, p, m_new):
        g, row0, has_prev, dst = job
        vv = key_rows(group_refs[g][2], row0, has_prev)
        m_scr[g, dst, :] = m_new
        if has_prev:
            pv = _dot(p, jnp.concatenate([vv, ones_rows], axis=1))
            acc_scr[g, dst, :] = jnp.where(left, pv[:BLK, :LANES], pv[BLK:, :LANES])
            l_scr[g, dst, :] = jnp.where(left, pv[:BLK, LANES:], pv[BLK:, LANES:])
        else:
            pv = _dot(p, jnp.concatenate([_head_stack(vv), ones_heads], axis=1))
            acc_scr[g, dst, :] = pv[:, :LANES]
            l_scr[g, dst, :] = pv[:, LANES:]

    pending = {u: scores_of(jobs[u]) for u in range(ATT_DEPTH)}
    for u, job in enumerate(jobs):
        p, m_new = softmax_of(job, pending.pop(u))
        if u + ATT_DEPTH < len(jobs):
            pending[u + ATT_DEPTH] = scores_of(jobs[u + ATT_DEPTH])
        output_of(job, p, m_new)

    def combine(n, carry):
        rws = pl.ds(pl.multiple_of(n * BLK, BLK), BLK)
        ms = [m_scr[g, rws, :] for g in range(N_GROUPS)]
        m_all = jnp.maximum(jnp.maximum(ms[0], ms[1]), ms[2])
        ws = [jnp.exp(m - m_all) for m in ms]
        den = sum(w * l_scr[g, rws, :] for g, w in enumerate(ws))
        num = sum(w * acc_scr[g, rws, :] for g, w in enumerate(ws))
        o_ref[0, 0, rws, :] = (num / den * _silu(gate_ref[0, rws, :])).astype(o_ref.dtype)
        return carry

    lax.fori_loop(0, seq // BLK, combine, 0)


def _attention(qkv, bias, gate_dt):
    b, _, s, _ = qkv[0].shape
    in_specs, args = [], []
    for arr in qkv:
        for part in range(3):
            in_specs.append(pl.BlockSpec((1, 1, s, LANES),
                                         lambda i, j, part=part: (i, part * HEAD_PAIRS + j, 0, 0)))
            args.append(arr)
    in_specs.append(pl.BlockSpec((N_GROUPS, 2, BLK, 2 * BLK), lambda i, j: (0, j, 0, 0)))
    in_specs.append(pl.BlockSpec((1, s, LANES), lambda i, j: (i, 0, j)))
    return pl.pallas_call(
        functools.partial(_attn_kernel, seq=s),
        grid=(b, HEAD_PAIRS),
        in_specs=in_specs,
        out_specs=pl.BlockSpec((1, 1, s, LANES), lambda i, j: (i, j, 0, 0)),
        out_shape=jax.ShapeDtypeStruct((b, HEAD_PAIRS, s, LANES), BF16),
        scratch_shapes=[pltpu.VMEM((N_GROUPS, s, LANES), F32)] * 3,
        compiler_params=pltpu.CompilerParams(
            dimension_semantics=("parallel", "parallel"), vmem_limit_bytes=VMEM_LIMIT),
        name="dilated_attention",
    )(*args, bias, gate_dt)


TIME_STRIDE = 4
TIME_SPAN = TIME_STRIDE * SUBLANES
HALO = SUBLANES


def _strided_rows(ref, base):
    pieces = [ref[pl.ds(base + TIME_SPAN * grp + i, SUBLANES, stride=TIME_STRIDE), :]
              for grp in range(CHUNK // TIME_SPAN) for i in range(TIME_STRIDE)]
    return jnp.concatenate(pieces, axis=0)


def _strided_time_of(idx):
    q = idx // SUBLANES
    return TIME_SPAN * (q // TIME_STRIDE) + q % TIME_STRIDE + TIME_STRIDE * (idx % SUBLANES)


def _silu(v):
    h = 0.5 * v
    return h * jnp.tanh(h) + h


def _dot_split2(a_f32, b_bf16):
    hi = a_f32.astype(BF16)
    lo = (a_f32 - hi.astype(F32)).astype(BF16)
    return _dot(hi, b_bf16) + _dot(lo, b_bf16)


def _ssd_kernel(xbc_ref, z_ref, dt_ref, convw_ref, convb_ref, dtb_ref, alog_ref, dskip_ref,
                normw_ref, expand_ref, o_ref, state_scr, xpad_scr, y_scr):
    c = pl.program_id(1)
    n_slabs = CONV_DIM // LANES
    x_slabs = D_INNER // LANES

    @pl.when(c == 0)
    def _():
        state_scr[...] = jnp.zeros_like(state_scr)
        xpad_scr[:, 0:HALO, :] = jnp.zeros((n_slabs, HALO, LANES), F32)

    acts = []
    for j in range(n_slabs):
        lanes = slice(j * LANES, (j + 1) * LANES)
        xpad_scr[j, HALO:HALO + CHUNK, :] = xbc_ref[0, :, lanes]
        conv = convb_ref[:, lanes]
        for k in range(CONV_WIDTH):
            tap = _strided_rows(xpad_scr.at[j], HALO - (CONV_WIDTH - 1) + k)
            conv = conv + convw_ref[k:k + 1, lanes] * tap
        xpad_scr[j, 0:HALO, :] = xbc_ref[0, CHUNK - HALO:CHUNK, lanes]
        acts.append(_silu(conv))
    xs = jnp.concatenate(acts[:x_slabs], axis=1)

    raw = _strided_rows(dt_ref.at[0], 0) + dtb_ref[...]
    dt = jnp.maximum(raw, 0.0) + jnp.log1p(jnp.exp(-jnp.abs(raw)))
    a = dt * (-jnp.exp(alog_ref[...]))
    t_row = _strided_time_of(lax.broadcasted_iota(jnp.int32, (CHUNK, CHUNK), 0))
    t_col = _strided_time_of(lax.broadcasted_iota(jnp.int32, (CHUNK, CHUNK), 1))
    causal = t_row >= t_col
    tril = jnp.where(causal, 1.0, 0.0).astype(BF16)
    hi, mid, lo = _split3(a)
    acs = _dot(tril, hi) + _dot(tril, mid) + _dot(tril, lo)
    acs_t = acs.T
    acs_end = acs[CHUNK - 1:CHUNK, :]

    expand = expand_ref[...]
    dt_full = _dot_split2(dt, expand)
    decay_in = _dot_split2(jnp.exp(acs), expand)
    w_out = _dot_split2(dt * jnp.exp(acs_end - acs), expand)
    chunk_decay = decay_in[CHUNK - 1:CHUNK, :]
    xdt_b = (xs * dt_full).astype(BF16)
    x_out = (xs * w_out).astype(BF16)

    for g in range(SSM_GROUPS):
        gl = slice(g * GROUP_LANES, (g + 1) * GROUP_LANES)
        bg32 = acts[x_slabs + g]
        bg = bg32.astype(BF16)
        cg = acts[x_slabs + SSM_GROUPS + g].astype(BF16)
        cb = _dot_nt(cg, bg)
        prev = state_scr[g]
        y_off = _dot(cg, prev.astype(BF16)) * decay_in[:, gl]
        state_scr[g] = prev * chunk_decay[:, gl] + _dot(bg32.T.astype(BF16), x_out[:, gl])
        for j in range(HEADS_PER_SSM_GROUP // 2):
            gs = []
            for hh in (g * HEADS_PER_SSM_GROUP + 2 * j, g * HEADS_PER_SSM_GROUP + 2 * j + 1):
                seg = acs[:, hh:hh + 1] - acs_t[hh:hh + 1, :]
                lmat = jnp.exp(jnp.where(causal, seg, -jnp.inf))
                gs.append((cb * lmat).astype(BF16))
            slab = g * (GROUP_LANES // LANES) + j
            lanes = slice(slab * LANES, (slab + 1) * LANES)
            y_diag = _dot(jnp.concatenate(gs, axis=1), _head_stack(xdt_b[:, lanes]))
            y = y_diag + y_off[:, j * LANES:(j + 1) * LANES] + dskip_ref[:, lanes] * xs[:, lanes]
            for q in range(CHUNK // SUBLANES):
                start = TIME_SPAN * (q // TIME_STRIDE) + q % TIME_STRIDE
                y_scr[slab, pl.ds(start, SUBLANES, stride=TIME_STRIDE), :] = (
                    y[q * SUBLANES:(q + 1) * SUBLANES, :])

    slabs_per_group = GROUP_LANES // LANES
    for g in range(SSM_GROUPS):
        gl = slice(g * GROUP_LANES, (g + 1) * GROUP_LANES)
        yg = jnp.concatenate([y_scr[g * slabs_per_group + j] for j in range(slabs_per_group)], axis=1)
        u = yg * _silu(z_ref[0, :, gl])
        ms = jnp.mean(jnp.square(u), axis=-1, keepdims=True)
        o_ref[0, :, gl] = (u * lax.rsqrt(ms + RMS_EPS) * normw_ref[:, gl]).astype(o_ref.dtype)


def _ssd(xbc, z, gate_dt, conv_w, conv_b, dt_bias, a_log, d_skip, norm_w):
    b, s, _ = xbc.shape
    pad = LANES - SSM_HEADS
    dtb = jnp.pad(dt_bias, (0, pad)).reshape(1, LANES)
    alog = jnp.pad(a_log, (0, pad)).reshape(1, LANES)
    dskip = jnp.repeat(d_skip, D_INNER // SSM_HEADS).reshape(1, D_INNER)
    expand = np.zeros((LANES, D_INNER), np.float32)
    expand[np.arange(D_INNER) // (D_INNER // SSM_HEADS), np.arange(D_INNER)] = 1.0
    expand = jnp.asarray(expand, BF16)
    dt_tile = (GATT_END - V_END) // LANES

    def whole(shape):
        return pl.BlockSpec(shape, lambda i, j: (0,) * len(shape))

    return pl.pallas_call(
        _ssd_kernel,
        grid=(b, s // CHUNK),
        in_specs=[pl.BlockSpec((1, CHUNK, CONV_DIM), lambda i, j: (i, j, 0)),
                  pl.BlockSpec((1, CHUNK, D_INNER), lambda i, j: (i, j, 0)),
                  pl.BlockSpec((1, CHUNK, LANES), lambda i, j: (i, j, dt_tile)),
                  whole((CONV_WIDTH, CONV_DIM)), whole((1, CONV_DIM)),
                  whole((1, LANES)), whole((1, LANES)), whole((1, D_INNER)),
                  whole((1, D_INNER)), whole((LANES, D_INNER))],
        out_specs=pl.BlockSpec((1, CHUNK, D_INNER), lambda i, j: (i, j, 0)),
        out_shape=jax.ShapeDtypeStruct((b, s, D_INNER), BF16),
        scratch_shapes=[pltpu.VMEM((SSM_GROUPS, D_STATE, GROUP_LANES), F32),
                        pltpu.VMEM((CONV_DIM // LANES, HALO + CHUNK, LANES), F32),
                        pltpu.VMEM((D_INNER // LANES, CHUNK, LANES), F32)],
        compiler_params=pltpu.CompilerParams(
            dimension_semantics=("parallel", "arbitrary"), vmem_limit_bytes=VMEM_LIMIT),
        name="ssd_mixer",
    )(xbc, z, gate_dt, conv_w, conv_b.reshape(1, CONV_DIM), dtb, alog, dskip,
      norm_w.reshape(1, D_INNER), expand)


def _out_kernel(att_ref, ssm_ref, gm_ref, gp_ref, p_ref, x_ref, wb_ref, wo_ref, wp_ref,
                bg_ref, lng_ref, lnb_ref, o_ref, *, alpha):
    o_att = jnp.concatenate([att_ref[0, j] for j in range(HEAD_PAIRS)], axis=1)
    y_a = _dot(o_att, wb_ref[:ATT_OUT, :])
    y_b = _dot(ssm_ref[0], wb_ref[ATT_OUT:, :])
    gm = gm_ref[0]
    merged = (_sigmoid(gm[:, :D_MODEL] + bg_ref[0:1, :]) * y_a
              + _sigmoid(gm[:, D_MODEL:] + bg_ref[1:2, :]) * y_b)
    mix = _dot(merged.astype(BF16), wo_ref[...])
    ple = _sigmoid(gp_ref[0] + bg_ref[2:3, :]) * _dot(p_ref[0].astype(BF16), wp_ref[...])
    h = alpha * x_ref[0] + mix + ple
    mu = jnp.mean(h, axis=-1, keepdims=True)
    var = jnp.mean(jnp.square(h - mu), axis=-1, keepdims=True)
    o_ref[0] = (h - mu) * lax.rsqrt(var + LN_EPS) * lng_ref[...] + lnb_ref[...]


def _output(o_att, y_ssm, g_merge, g_ple, p_i, x, w_branch, w_out, w_ple, b_gate, ln_g, ln_b, alpha):
    b, s, d = x.shape
    rows = OUT_ROWS

    def whole(shape):
        return pl.BlockSpec(shape, lambda i, j: (0,) * len(shape))

    def tile(n):
        return pl.BlockSpec((1, rows, n), lambda i, j: (i, j, 0))

    return pl.pallas_call(
        functools.partial(_out_kernel, alpha=alpha),
        grid=(b, s // rows),
        in_specs=[pl.BlockSpec((1, HEAD_PAIRS, rows, LANES), lambda i, j: (i, 0, j, 0)),
                  tile(D_INNER), tile(2 * D_MODEL), tile(D_MODEL), tile(PLE_DIM), tile(D_MODEL),
                  whole((ATT_OUT + D_INNER, D_MODEL)), whole((D_MODEL, D_MODEL)),
                  whole((PLE_DIM, D_MODEL)), whole((3, D_MODEL)), whole((1, D_MODEL)),
                  whole((1, D_MODEL))],
        out_specs=tile(D_MODEL),
        out_shape=jax.ShapeDtypeStruct((b, s, d), F32),
        compiler_params=pltpu.CompilerParams(
            dimension_semantics=("parallel", "parallel"), vmem_limit_bytes=VMEM_LIMIT),
        name="merge_out_ln",
    )(o_att, y_ssm, g_merge, g_ple, p_i, x, w_branch.astype(BF16), w_out.astype(BF16),
      w_ple.astype(BF16), b_gate, ln_g.reshape(1, d), ln_b.reshape(1, d))


def _layer(x, p_i, w_in, b_gate, conv_w, conv_b, dt_bias, a_log, d_skip, ssm_norm_w,
           w_branch, w_out, w_ple, ln_g, ln_b, bias, alpha):
    w = w_in.astype(BF16)
    scale = HEAD_DIM ** -0.5
    group_w = HEADS_PER_GROUP * HEAD_DIM
    qkv = []
    for g, (_, dil) in enumerate(PATTERNS):
        cols = slice(g * group_w, (g + 1) * group_w)
        wg = jnp.concatenate([w[:, :Q_END][:, cols] * scale, w[:, Q_END:K_END][:, cols],
                              w[:, K_END:V_END][:, cols]], axis=1)
        qkv.append(_project(x, wg, dilation=dil, tiled_out=True, out_dtype=BF16, name=f"proj_qkv{g}"))
    w_gate_dt = jnp.concatenate(
        [w[:, V_END:GATT_END], jnp.pad(w[:, XBC_END:DT_END], ((0, 0), (0, LANES - SSM_HEADS)))], axis=1)
    gate_dt = _project(x, w_gate_dt, dilation=1, tiled_out=False, out_dtype=F32, name="proj_gate_dt")
    z = _project(x, w[:, GATT_END:Z_END], dilation=1, tiled_out=False, out_dtype=F32, name="proj_z")
    xbc = _project(x, w[:, Z_END:XBC_END], dilation=1, tiled_out=False, out_dtype=F32, name="proj_xbc")
    g_merge = _project(x, w[:, DT_END:GMERGE_END], dilation=1, tiled_out=False, out_dtype=F32,
                       name="proj_gmerge")
    g_ple = _project(x, w[:, GMERGE_END:], dilation=1, tiled_out=False, out_dtype=F32, name="proj_gple")

    o_att = _attention(qkv, bias, gate_dt)
    y_ssm = _ssd(xbc, z, gate_dt, conv_w, conv_b, dt_bias, a_log, d_skip, ssm_norm_w)
    return _output(o_att, y_ssm, g_merge, g_ple, p_i, x, w_branch, w_out, w_ple, b_gate,
                   ln_g, ln_b, alpha)


def kernel(x, p, w_in, b_gate, conv_w, conv_b, dt_bias, a_log, d_skip, ssm_norm_w, w_branch, w_out,
           w_ple, ln_g, ln_b, rel_bias):
    depth = w_in.shape[0]
    alpha = (2.0 * depth) ** 0.25
    bias = _bias_tiles(rel_bias)
    for i in range(depth):
        x = _layer(x, p[i], w_in[i], b_gate[i], conv_w[i], conv_b[i], dt_bias[i], a_log[i],
                   d_skip[i], ssm_norm_w[i], w_branch[i], w_out[i], w_ple[i], ln_g[i], ln_b[i],
                   bias, alpha)
    return x
```

```python
import functools
import math

import numpy as np
import jax
import jax.numpy as jnp
from jax import lax
from jax.experimental import pallas as pl
from jax.experimental.pallas import tpu as pltpu

F32 = jnp.float32
BF16 = jnp.bfloat16

LANES = 128
SUBLANES = 8
V7X_VMEM_BYTES = 64 * 1024 * 1024
VMEM_LIMIT = (V7X_VMEM_BYTES * 3) // 4

D_MODEL = 1024
HEAD_DIM = 64
HEADS_PER_GROUP = 12
PATTERNS = ((128, 1), (512, 4), (2048, 16))
N_GROUPS = 3
ATT_HEADS = N_GROUPS * HEADS_PER_GROUP
ATT_QKV = ATT_HEADS * HEAD_DIM
ATT_OUT = HEADS_PER_GROUP * HEAD_DIM
BLK = 128
NUM_BUCKETS = 32
MAX_DISTANCE = 2048
D_INNER = 2048
SSM_HEADS = 32
SSM_GROUPS = 4
D_STATE = 128
CONV_WIDTH = 4
CONV_DIM = D_INNER + 2 * SSM_GROUPS * D_STATE
CHUNK = 128
PLE_DIM = 256
LN_EPS = 1e-5
RMS_EPS = 1e-5
Q_END = ATT_QKV
K_END = Q_END + ATT_QKV
V_END = K_END + ATT_QKV
GATT_END = V_END + ATT_OUT
Z_END = GATT_END + D_INNER
XBC_END = Z_END + CONV_DIM
DT_END = XBC_END + SSM_HEADS
GMERGE_END = DT_END + 2 * D_MODEL

HEAD_PAIRS = HEADS_PER_GROUP // 2
QKV_TILES = 3 * HEAD_PAIRS
GROUP_LANES = D_INNER // SSM_GROUPS
HEADS_PER_SSM_GROUP = SSM_HEADS // SSM_GROUPS
PROJ_ROWS = 512
OUT_ROWS = 512
OUT_SUBROWS = 256


def _sigmoid_t(v):
    return 0.5 * jnp.tanh(0.5 * v) + 0.5


def _split3(v):
    hi = v.astype(BF16)
    r1 = v - hi.astype(F32)
    mid = r1.astype(BF16)
    lo = (r1 - mid.astype(F32)).astype(BF16)
    return hi, mid, lo


def _dot(a, b):
    return jnp.dot(a, b, preferred_element_type=F32)


def _dot_nt(a, b):
    return lax.dot_general(a, b, (((1,), (1,)), ((), ())), preferred_element_type=F32)


def _dot_tn(a, b):
    return lax.dot_general(a, b, (((0,), (0,)), ((), ())), preferred_element_type=F32)


def _dot_exact_rhs(a_f32, b_bf16):
    hi, mid, lo = _split3(a_f32)
    return _dot(hi, b_bf16) + _dot(mid, b_bf16) + _dot(lo, b_bf16)


def _proj_kernel(*refs, permute, tiled_out, silu_widths):
    if permute:
        x_ref, perm_ref, w_ref, *o_refs = refs
    else:
        x_ref, w_ref, *o_refs = refs
    lhs = x_ref[0].astype(BF16)
    if permute:
        lhs = _dot(perm_ref[...], lhs).astype(BF16)
    res = _dot(lhs, w_ref[...])
    if tiled_out:
        (o_ref,) = o_refs
        _, n_tiles, n_classes, _, sub, _ = o_ref.shape
        for c in range(n_tiles):
            tile = res[:, c * LANES:(c + 1) * LANES].reshape(n_classes, sub, LANES)
            o_ref[0, c, :, 0] = tile.astype(o_ref.dtype)
    else:
        off = 0
        for o_ref, n_silu in zip(o_refs, silu_widths):
            width = o_ref.shape[-1]
            if n_silu:
                o_ref[0, :, :n_silu] = _silu(res[:, off:off + n_silu]).astype(o_ref.dtype)
            if n_silu < width:
                o_ref[0, :, n_silu:] = res[:, off + n_silu:off + width].astype(o_ref.dtype)
            off += width


def _class_permutation(dilation):
    sub = PROJ_ROWS // dilation
    dst = np.arange(PROJ_ROWS)
    src = dilation * (dst % sub) + dst // sub
    perm = np.zeros((PROJ_ROWS, PROJ_ROWS), np.float32)
    perm[dst, src] = 1.0
    return jnp.asarray(perm, BF16)


def _project(x, w, *, name, dilation=1, tiled_out=False, widths=None, silu_widths=None,
             out_dtype=F32):
    b, s, d = x.shape
    n = w.shape[1]
    steps = s // PROJ_ROWS
    permute = dilation > 1
    in_specs = [pl.BlockSpec((1, PROJ_ROWS, d), lambda i, j: (i, j, 0))]
    args = [x]
    if permute:
        in_specs.append(pl.BlockSpec((PROJ_ROWS, PROJ_ROWS), lambda i, j: (0, 0)))
        args.append(_class_permutation(dilation))
    in_specs.append(pl.BlockSpec((d, n), lambda i, j: (0, 0)))
    args.append(w)
    if tiled_out:
        sub = PROJ_ROWS // dilation
        out_shape = jax.ShapeDtypeStruct((b, n // LANES, dilation, steps, sub, LANES), out_dtype)
        o_spec = pl.BlockSpec((1, n // LANES, dilation, 1, sub, LANES), lambda i, j: (i, 0, 0, j, 0, 0))
    else:
        assert not permute and sum(widths) == n
        out_shape = tuple(jax.ShapeDtypeStruct((b, s, wd), out_dtype) for wd in widths)
        o_spec = tuple(pl.BlockSpec((1, PROJ_ROWS, wd), lambda i, j: (i, j, 0)) for wd in widths)
    out = pl.pallas_call(
        functools.partial(_proj_kernel, permute=permute, tiled_out=tiled_out,
                          silu_widths=silu_widths or (0,) * len(widths or ())),
        grid=(b, steps),
        in_specs=in_specs,
        out_specs=o_spec,
        out_shape=out_shape,
        compiler_params=pltpu.CompilerParams(
            dimension_semantics=("parallel", "parallel"), vmem_limit_bytes=VMEM_LIMIT),
        name=name,
    )(*args)
    return out.reshape(b, n // LANES, s, LANES) if tiled_out else out


def _bucket_tiles():
    qi = np.arange(BLK)[:, None]
    kj = np.arange(2 * BLK)[None, :]
    delta = qi + BLK - kj
    max_exact = NUM_BUCKETS // 2
    out = []
    for window, dil in PATTERNS:
        valid = (delta >= 0) & (delta <= window // dil)
        dist = np.maximum(delta, 0) * dil
        d_f = np.maximum(dist, 1).astype(np.float32)
        large = max_exact + (np.log(d_f / max_exact) / np.float32(math.log(MAX_DISTANCE / max_exact))
                             * (NUM_BUCKETS - max_exact)).astype(np.int32)
        large = np.minimum(large, NUM_BUCKETS - 1)
        bucket = np.where(dist < max_exact, dist, large)
        out.append(np.where(valid, bucket, -1).astype(np.int32))
    return np.stack(out)


def _bias_kernel(tbl_ref, bucket_ref, o_ref):
    col = pl.program_id(0) * HEADS_PER_GROUP + pl.program_id(1)
    bucket = bucket_ref[0]
    acc = jnp.full(bucket.shape, -jnp.inf, F32)
    for bkt in range(NUM_BUCKETS):
        acc = jnp.where(bucket == bkt, tbl_ref[bkt, col], acc)
    o_ref[0, 0] = acc


def _bias_tiles(rel_bias):
    buckets = jnp.asarray(_bucket_tiles())
    return pl.pallas_call(
        _bias_kernel,
        grid=(N_GROUPS, HEADS_PER_GROUP),
        in_specs=[pl.BlockSpec(memory_space=pltpu.SMEM),
                  pl.BlockSpec((1, BLK, 2 * BLK), lambda g, h: (g, 0, 0))],
        out_specs=pl.BlockSpec((1, 1, BLK, 2 * BLK), lambda g, h: (g, h, 0, 0)),
        out_shape=jax.ShapeDtypeStruct((N_GROUPS, HEADS_PER_GROUP, BLK, 2 * BLK), F32),
        name="bias_tiles",
    )(rel_bias, buckets)


ATT_DEPTH = 3


def _head_stack(t):
    lane = lax.broadcasted_iota(jnp.int32, t.shape, 1)
    zero = jnp.zeros_like(t)
    return jnp.concatenate([jnp.where(lane < HEAD_DIM, t, zero),
                            jnp.where(lane >= HEAD_DIM, t, zero)], axis=0)


def _attn_kernel(q1, k1, v1, q2, k2, v2, q3, k3, v3, bias_ref, gate_ref, o_ref,
                 m_scr, l_scr, acc_scr, *, seq):
    left = lax.broadcasted_iota(jnp.int32, (BLK, LANES), 1) < HEAD_DIM
    ones_rows = jnp.ones((2 * BLK, LANES), BF16)
    r_idx = lax.broadcasted_iota(jnp.int32, (2 * BLK, LANES), 0)
    c_idx = lax.broadcasted_iota(jnp.int32, (2 * BLK, LANES), 1)
    ones_heads = jnp.where((r_idx < BLK) == (c_idx < HEAD_DIM), 1.0, 0.0).astype(BF16)

    jobs = []
    for g, (_, dil) in enumerate(PATTERNS):
        sub = seq // dil
        for r in range(dil):
            for n in range(sub // BLK):
                dst = pl.ds(n * BLK * dil + r, BLK, stride=dil) if dil > 1 else pl.ds(n * BLK, BLK)
                jobs.append((g, r * sub + n * BLK, n > 0, dst))
    group_refs = ((q1, k1, v1), (q2, k2, v2), (q3, k3, v3))

    def key_rows(ref, row0, has_prev):
        if has_prev:
            return ref[0, 0, row0 - BLK:row0 + BLK, :]
        return ref[0, 0, row0:row0 + BLK, :]

    def scores_of(job):
        g, row0, has_prev, _ = job
        q_ref, k_ref, _ = group_refs[g]
        q = q_ref[0, 0, row0:row0 + BLK, :]
        kk = key_rows(k_ref, row0, has_prev)
        if has_prev:
            return _dot_nt(_head_stack(q), kk)
        return _dot_nt(q, _head_stack(kk))

    def softmax_of(job, s):
        g, _, has_prev, _ = job
        es, ms = [], []
        for h in range(2):
            if has_prev:
                sh = s[h * BLK:(h + 1) * BLK, :] + bias_ref[g, h]
            else:
                sh = s[:, h * BLK:(h + 1) * BLK] + bias_ref[g, h, :, BLK:]
            mh = jnp.max(sh, axis=-1, keepdims=True)
            es.append(jnp.exp(sh - mh).astype(BF16))
            ms.append(mh)
        return jnp.concatenate(es, axis=0 if has_prev else 1), jnp.where(left, ms[0], ms[1])

    def output_of(job, p, m_new):
        g, row0, has_prev, dst = job
        vv = key_rows(group_refs[g][2], row0, has_prev)
        m_scr[g, dst, :] = m_new
        if has_prev:
            pv = _dot(p, jnp.concatenate([vv, ones_rows], axis=1))
            acc_scr[g, dst, :] = jnp.where(left, pv[:BLK, :LANES], pv[BLK:, :LANES])
            l_scr[g, dst, :] = jnp.where(left, pv[:BLK, LANES:], pv[BLK:, LANES:])
        else:
            pv = _dot(p, jnp.concatenate([_head_stack(vv), ones_heads], axis=1))
            acc_scr[g, dst, :] = pv[:, :LANES]
            l_scr[g, dst, :] = pv[:, LANES:]

    pending = {u: scores_of(jobs[u]) for u in range(ATT_DEPTH)}
    for u, job in enumerate(jobs):
        p, m_new = softmax_of(job, pending.pop(u))
        if u + ATT_DEPTH < len(jobs):
            pending[u + ATT_DEPTH] = scores_of(jobs[u + ATT_DEPTH])
        output_of(job, p, m_new)

    def combine(n, carry):
        rws = pl.ds(pl.multiple_of(n * BLK, BLK), BLK)
        ms = [m_scr[g, rws, :] for g in range(N_GROUPS)]
        m_all = jnp.maximum(jnp.maximum(ms[0], ms[1]), ms[2])
        ws = [jnp.exp(m - m_all) for m in ms]
        den = sum(w * l_scr[g, rws, :] for g, w in enumerate(ws))
        num = sum(w * acc_scr[g, rws, :] for g, w in enumerate(ws))
        o_ref[0, 0, rws, :] = (num / den * gate_ref[0, rws, :]).astype(o_ref.dtype)
        return carry

    lax.fori_loop(0, seq // BLK, combine, 0)


def _attention(qkv, bias, gate_dt):
    b, _, s, _ = qkv[0].shape
    in_specs, args = [], []
    for arr in qkv:
        for part in range(3):
            in_specs.append(pl.BlockSpec((1, 1, s, LANES),
                                         lambda i, j, part=part: (i, part * HEAD_PAIRS + j, 0, 0)))
            args.append(arr)
    in_specs.append(pl.BlockSpec((N_GROUPS, 2, BLK, 2 * BLK), lambda i, j: (0, j, 0, 0)))
    in_specs.append(pl.BlockSpec((1, s, LANES), lambda i, j: (i, 0, j)))
    return pl.pallas_call(
        functools.partial(_attn_kernel, seq=s),
        grid=(b, HEAD_PAIRS),
        in_specs=in_specs,
        out_specs=pl.BlockSpec((1, 1, s, LANES), lambda i, j: (i, j, 0, 0)),
        out_shape=jax.ShapeDtypeStruct((b, HEAD_PAIRS, s, LANES), BF16),
        scratch_shapes=[pltpu.VMEM((N_GROUPS, s, LANES), F32)] * 3,
        compiler_params=pltpu.CompilerParams(
            dimension_semantics=("parallel", "parallel"), vmem_limit_bytes=VMEM_LIMIT),
        name="dilated_attention",
    )(*args, bias, gate_dt)


TIME_STRIDE = 4
TIME_SPAN = TIME_STRIDE * SUBLANES
HALO = SUBLANES


def _strided_rows(ref, base):
    pieces = [ref[pl.ds(base + TIME_SPAN * grp + i, SUBLANES, stride=TIME_STRIDE), :]
              for grp in range(CHUNK // TIME_SPAN) for i in range(TIME_STRIDE)]
    return jnp.concatenate(pieces, axis=0)


def _strided_time_of(idx):
    q = idx // SUBLANES
    return TIME_SPAN * (q // TIME_STRIDE) + q % TIME_STRIDE + TIME_STRIDE * (idx % SUBLANES)


def _silu(v):
    h = 0.5 * v
    return h * jnp.tanh(h) + h


def _dot_split2(a_f32, b_bf16):
    hi = a_f32.astype(BF16)
    lo = (a_f32 - hi.astype(F32)).astype(BF16)
    return _dot(hi, b_bf16) + _dot(lo, b_bf16)


def _ssd_kernel(xbc_ref, z_ref, dt_ref, convw_ref, convb_ref, dtb_ref, alog_ref, dskip_ref,
                normw_ref, expand_ref, o_ref, state_scr, xpad_scr, y_scr):
    c = pl.program_id(1)
    n_slabs = CONV_DIM // LANES
    x_slabs = D_INNER // LANES

    @pl.when(c == 0)
    def _():
        state_scr[...] = jnp.zeros_like(state_scr)
        xpad_scr[:, 0:HALO, :] = jnp.zeros((n_slabs, HALO, LANES), F32)

    def conv_silu(j):
        lanes = slice(j * LANES, (j + 1) * LANES)
        xpad_scr[j, HALO:HALO + CHUNK, :] = xbc_ref[0, :, lanes]
        conv = convb_ref[:, lanes]
        for k in range(CONV_WIDTH):
            tap = _strided_rows(xpad_scr.at[j], HALO - (CONV_WIDTH - 1) + k)
            conv = conv + convw_ref[k:k + 1, lanes] * tap
        xpad_scr[j, 0:HALO, :] = xbc_ref[0, CHUNK - HALO:CHUNK, lanes]
        return _silu(conv)

    raw = _strided_rows(dt_ref.at[0], 0) + dtb_ref[...]
    dt = jnp.maximum(raw, 0.0) + jnp.log1p(jnp.exp(-jnp.abs(raw)))
    a = dt * (-jnp.exp(alog_ref[...]))
    t_row = _strided_time_of(lax.broadcasted_iota(jnp.int32, (CHUNK, CHUNK), 0))
    t_col = _strided_time_of(lax.broadcasted_iota(jnp.int32, (CHUNK, CHUNK), 1))
    causal = t_row >= t_col
    tril = jnp.where(causal, 1.0, 0.0).astype(BF16)
    hi, mid, lo = _split3(a)
    acs = _dot(tril, hi) + _dot(tril, mid) + _dot(tril, lo)
    acs_t = acs.T
    acs_end = acs[CHUNK - 1:CHUNK, :]

    def two_pieces(v):
        hi2 = v.astype(BF16)
        return hi2, (v - hi2.astype(F32)).astype(BF16)

    dt_p = two_pieces(dt)
    decay_in_p = two_pieces(jnp.exp(acs))
    w_out_p = two_pieces(dt * jnp.exp(acs_end - acs))

    slabs_per_group = GROUP_LANES // LANES
    for g in range(SSM_GROUPS):
        gl = slice(g * GROUP_LANES, (g + 1) * GROUP_LANES)
        expand = expand_ref[:, gl]
        dt_full, decay_in, w_out = (_dot(hi2, expand) + _dot(lo2, expand)
                                    for hi2, lo2 in (dt_p, decay_in_p, w_out_p))
        chunk_decay = decay_in[CHUNK - 1:CHUNK, :]
        xs = jnp.concatenate([conv_silu(g * slabs_per_group + j) for j in range(slabs_per_group)],
                             axis=1)
        bg32 = conv_silu(x_slabs + g)
        bg = bg32.astype(BF16)
        cg = conv_silu(x_slabs + SSM_GROUPS + g).astype(BF16)
        xdt_b = (xs * dt_full).astype(BF16)
        x_out = (xs * w_out).astype(BF16)
        cb = _dot_nt(cg, bg)
        prev = state_scr[g]
        y_off = _dot(cg, prev.astype(BF16)) * decay_in
        state_scr[g] = prev * chunk_decay + _dot(bg32.T.astype(BF16), x_out)
        for j in range(HEADS_PER_SSM_GROUP // 2):
            gs = []
            for hh in (g * HEADS_PER_SSM_GROUP + 2 * j, g * HEADS_PER_SSM_GROUP + 2 * j + 1):
                seg = acs[:, hh:hh + 1] - acs_t[hh:hh + 1, :]
                lmat = jnp.exp(jnp.where(causal, seg, -jnp.inf))
                gs.append((cb * lmat).astype(BF16))
            slab = g * slabs_per_group + j
            pair = slice(j * LANES, (j + 1) * LANES)
            y_diag = _dot(jnp.concatenate(gs, axis=1), _head_stack(xdt_b[:, pair]))
            y = (y_diag + y_off[:, pair]
                 + dskip_ref[:, slab * LANES:(slab + 1) * LANES] * xs[:, pair])
            for q in range(CHUNK // SUBLANES):
                start = TIME_SPAN * (q // TIME_STRIDE) + q % TIME_STRIDE
                y_scr[slab, pl.ds(start, SUBLANES, stride=TIME_STRIDE), :] = (
                    y[q * SUBLANES:(q + 1) * SUBLANES, :])

        yg = jnp.concatenate([y_scr[g * slabs_per_group + j] for j in range(slabs_per_group)], axis=1)
        u = yg * z_ref[0, :, gl]
        ms = jnp.mean(jnp.square(u), axis=-1, keepdims=True)
        o_ref[0, :, gl] = (u * lax.rsqrt(ms + RMS_EPS) * normw_ref[:, gl]).astype(o_ref.dtype)


def _ssd(xbc, z, gate_dt, conv_w, conv_b, dt_bias, a_log, d_skip, norm_w):
    b, s, _ = xbc.shape
    pad = LANES - SSM_HEADS
    dtb = jnp.pad(dt_bias, (0, pad)).reshape(1, LANES)
    alog = jnp.pad(a_log, (0, pad)).reshape(1, LANES)
    dskip = jnp.repeat(d_skip, D_INNER // SSM_HEADS).reshape(1, D_INNER)
    expand = np.zeros((LANES, D_INNER), np.float32)
    expand[np.arange(D_INNER) // (D_INNER // SSM_HEADS), np.arange(D_INNER)] = 1.0
    expand = jnp.asarray(expand, BF16)
    dt_tile = (GATT_END - V_END) // LANES

    def whole(shape):
        return pl.BlockSpec(shape, lambda i, j: (0,) * len(shape))

    return pl.pallas_call(
        _ssd_kernel,
        grid=(b, s // CHUNK),
        in_specs=[pl.BlockSpec((1, CHUNK, CONV_DIM), lambda i, j: (i, j, 0)),
                  pl.BlockSpec((1, CHUNK, D_INNER), lambda i, j: (i, j, 0)),
                  pl.BlockSpec((1, CHUNK, LANES), lambda i, j: (i, j, dt_tile)),
                  whole((CONV_WIDTH, CONV_DIM)), whole((1, CONV_DIM)),
                  whole((1, LANES)), whole((1, LANES)), whole((1, D_INNER)),
                  whole((1, D_INNER)), whole((LANES, D_INNER))],
        out_specs=pl.BlockSpec((1, CHUNK, D_INNER), lambda i, j: (i, j, 0)),
        out_shape=jax.ShapeDtypeStruct((b, s, D_INNER), BF16),
        scratch_shapes=[pltpu.VMEM((SSM_GROUPS, D_STATE, GROUP_LANES), F32),
                        pltpu.VMEM((CONV_DIM // LANES, HALO + CHUNK, LANES), F32),
                        pltpu.VMEM((D_INNER // LANES, CHUNK, LANES), F32)],
        compiler_params=pltpu.CompilerParams(
            dimension_semantics=("parallel", "arbitrary"), vmem_limit_bytes=VMEM_LIMIT),
        name="ssd_mixer",
    )(xbc, z, gate_dt, conv_w, conv_b.reshape(1, CONV_DIM), dtb, alog, dskip,
      norm_w.reshape(1, D_INNER), expand)


def _out_kernel(att_ref, ssm_ref, gm_ref, gp_ref, p_ref, x_ref, wb_ref, wo_ref, wp_ref,
                bg_ref, lng_ref, lnb_ref, o_ref, *, alpha):
    subs = [slice(t * OUT_SUBROWS, (t + 1) * OUT_SUBROWS) for t in range(OUT_ROWS // OUT_SUBROWS)]
    branch = []
    for rs in subs:
        o_att = jnp.concatenate([att_ref[0, j, rs, :] for j in range(HEAD_PAIRS)], axis=1)
        branch.append((_dot(o_att, wb_ref[:ATT_OUT, :]), _dot(ssm_ref[0, rs, :], wb_ref[ATT_OUT:, :]),
                       _dot(p_ref[0, rs, :].astype(BF16), wp_ref[...])))
    mixes = []
    for rs, (y_a, y_b, _) in zip(subs, branch):
        merged = (_sigmoid_t(gm_ref[0, rs, :D_MODEL] + bg_ref[0:1, :]) * y_a
                  + _sigmoid_t(gm_ref[0, rs, D_MODEL:] + bg_ref[1:2, :]) * y_b)
        mixes.append(_dot(merged.astype(BF16), wo_ref[...]))
    for rs, (_, _, pp), mix in zip(subs, branch, mixes):
        ple = _sigmoid_t(gp_ref[0, rs, :] + bg_ref[2:3, :]) * pp
        h = alpha * x_ref[0, rs, :] + mix + ple
        mu = jnp.mean(h, axis=-1, keepdims=True)
        var = jnp.mean(jnp.square(h - mu), axis=-1, keepdims=True)
        o_ref[0, rs, :] = (h - mu) * lax.rsqrt(var + LN_EPS) * lng_ref[...] + lnb_ref[...]


def _output(o_att, y_ssm, g_merge, g_ple, p_i, x, w_branch, w_out, w_ple, b_gate, ln_g, ln_b, alpha):
    b, s, d = x.shape
    rows = OUT_ROWS

    def whole(shape):
        return pl.BlockSpec(shape, lambda i, j: (0,) * len(shape), pipeline_mode=pl.Buffered(1))

    def tile(n):
        return pl.BlockSpec((1, rows, n), lambda i, j: (i, j, 0))

    return pl.pallas_call(
        functools.partial(_out_kernel, alpha=alpha),
        grid=(b, s // rows),
        in_specs=[pl.BlockSpec((1, HEAD_PAIRS, rows, LANES), lambda i, j: (i, 0, j, 0)),
                  tile(D_INNER), tile(2 * D_MODEL), tile(D_MODEL), tile(PLE_DIM), tile(D_MODEL),
                  whole((ATT_OUT + D_INNER, D_MODEL)), whole((D_MODEL, D_MODEL)),
                  whole((PLE_DIM, D_MODEL)), whole((3, D_MODEL)), whole((1, D_MODEL)),
                  whole((1, D_MODEL))],
        out_specs=tile(D_MODEL),
        out_shape=jax.ShapeDtypeStruct((b, s, d), F32),
        compiler_params=pltpu.CompilerParams(
            dimension_semantics=("parallel", "parallel"), vmem_limit_bytes=VMEM_LIMIT),
        name="merge_out_ln",
    )(o_att, y_ssm, g_merge, g_ple, p_i, x, w_branch.astype(BF16), w_out.astype(BF16),
      w_ple.astype(BF16), b_gate, ln_g.reshape(1, d), ln_b.reshape(1, d))


def _layer(x, p_i, w_in, b_gate, conv_w, conv_b, dt_bias, a_log, d_skip, ssm_norm_w,
           w_branch, w_out, w_ple, ln_g, ln_b, bias, alpha):
    w = w_in.astype(BF16)
    scale = HEAD_DIM ** -0.5
    group_w = HEADS_PER_GROUP * HEAD_DIM
    qkv = []
    for g, (_, dil) in enumerate(PATTERNS):
        cols = slice(g * group_w, (g + 1) * group_w)
        wg = jnp.concatenate([w[:, :Q_END][:, cols] * scale, w[:, Q_END:K_END][:, cols],
                              w[:, K_END:V_END][:, cols]], axis=1)
        qkv.append(_project(x, wg, dilation=dil, tiled_out=True, out_dtype=BF16, name=f"proj_qkv{g}"))
    w_gates = jnp.concatenate(
        [w[:, V_END:GATT_END], jnp.pad(w[:, XBC_END:DT_END], ((0, 0), (0, LANES - SSM_HEADS))),
         w[:, GMERGE_END:]], axis=1)
    gate_dt, g_ple = _project(x, w_gates, widths=(ATT_OUT + LANES, D_MODEL),
                              silu_widths=(ATT_OUT, 0), name="proj_gates")
    w_z_gm = jnp.concatenate([w[:, GATT_END:Z_END], w[:, DT_END:GMERGE_END]], axis=1)
    z_gate, g_merge = _project(x, w_z_gm, widths=(D_INNER, 2 * D_MODEL),
                               silu_widths=(D_INNER, 0), name="proj_z_gmerge")
    (xbc,) = _project(x, w[:, Z_END:XBC_END], widths=(CONV_DIM,), name="proj_xbc")

    o_att = _attention(qkv, bias, gate_dt)
    y_ssm = _ssd(xbc, z_gate, gate_dt, conv_w, conv_b, dt_bias, a_log, d_skip, ssm_norm_w)
    return _output(o_att, y_ssm, g_merge, g_ple, p_i, x, w_branch, w_out, w_ple, b_gate,
                   ln_g, ln_b, alpha)


def kernel(x, p, w_in, b_gate, conv_w, conv_b, dt_bias, a_log, d_skip, ssm_norm_w, w_branch, w_out,
           w_ple, ln_g, ln_b, rel_bias):
    depth = w_in.shape[0]
    alpha = (2.0 * depth) ** 0.25
    bias = _bias_tiles(rel_bias)
    for i in range(depth):
        x = _layer(x, p[i], w_in[i], b_gate[i], conv_w[i], conv_b[i], dt_bias[i], a_log[i],
                   d_skip[i], ssm_norm_w[i], w_branch[i], w_out[i], w_ple[i], ln_g[i], ln_b[i],
                   bias, alpha)
    return x
```

```python
import functools
import math

import numpy as np
import jax
import jax.numpy as jnp
from jax import lax
from jax.experimental import pallas as pl
from jax.experimental.pallas import tpu as pltpu

F32 = jnp.float32
BF16 = jnp.bfloat16

LANES = 128
SUBLANES = 8
V7X_VMEM_BYTES = 64 * 1024 * 1024
VMEM_LIMIT = (V7X_VMEM_BYTES * 3) // 4

D_MODEL = 1024
HEAD_DIM = 64
HEADS_PER_GROUP = 12
PATTERNS = ((128, 1), (512, 4), (2048, 16))
N_GROUPS = 3
ATT_HEADS = N_GROUPS * HEADS_PER_GROUP
ATT_QKV = ATT_HEADS * HEAD_DIM
ATT_OUT = HEADS_PER_GROUP * HEAD_DIM
BLK = 128
NUM_BUCKETS = 32
MAX_DISTANCE = 2048
D_INNER = 2048
SSM_HEADS = 32
SSM_GROUPS = 4
D_STATE = 128
CONV_WIDTH = 4
CONV_DIM = D_INNER + 2 * SSM_GROUPS * D_STATE
CHUNK = 128
PLE_DIM = 256
LN_EPS = 1e-5
RMS_EPS = 1e-5
LOG2E = math.log2(math.e)
Q_END = ATT_QKV
K_END = Q_END + ATT_QKV
V_END = K_END + ATT_QKV
GATT_END = V_END + ATT_OUT
Z_END = GATT_END + D_INNER
XBC_END = Z_END + CONV_DIM
DT_END = XBC_END + SSM_HEADS
GMERGE_END = DT_END + 2 * D_MODEL

HEAD_PAIRS = HEADS_PER_GROUP // 2
QKV_TILES = 3 * HEAD_PAIRS
GROUP_LANES = D_INNER // SSM_GROUPS
HEADS_PER_SSM_GROUP = SSM_HEADS // SSM_GROUPS
PROJ_ROWS = 512
SSD_ROWS = 4 * CHUNK
OUT_ROWS = 512
OUT_SUBROWS = 256


def _sigmoid_t(v):
    return 0.5 * jnp.tanh(0.5 * v) + 0.5


def _split3(v):
    hi = v.astype(BF16)
    r1 = v - hi.astype(F32)
    mid = r1.astype(BF16)
    lo = (r1 - mid.astype(F32)).astype(BF16)
    return hi, mid, lo


def _dot(a, b):
    return jnp.dot(a, b, preferred_element_type=F32)


def _dot_nt(a, b):
    return lax.dot_general(a, b, (((1,), (1,)), ((), ())), preferred_element_type=F32)


def _dot_tn(a, b):
    return lax.dot_general(a, b, (((0,), (0,)), ((), ())), preferred_element_type=F32)


def _dot_exact_rhs(a_f32, b_bf16):
    hi, mid, lo = _split3(a_f32)
    return _dot(hi, b_bf16) + _dot(mid, b_bf16) + _dot(lo, b_bf16)


def _proj_kernel(*refs, permute, tiled_out, silu_widths):
    if permute:
        x_ref, perm_ref, w_ref, *o_refs = refs
    else:
        x_ref, w_ref, *o_refs = refs
    lhs = x_ref[0].astype(BF16)
    if permute:
        lhs = _dot(perm_ref[...], lhs).astype(BF16)
    res = _dot(lhs, w_ref[...])
    if tiled_out:
        (o_ref,) = o_refs
        _, n_tiles, n_classes, _, sub, _ = o_ref.shape
        for c in range(n_tiles):
            tile = res[:, c * LANES:(c + 1) * LANES].reshape(n_classes, sub, LANES)
            o_ref[0, c, :, 0] = tile.astype(o_ref.dtype)
    else:
        off = 0
        for o_ref, n_silu in zip(o_refs, silu_widths):
            width = o_ref.shape[-1]
            if n_silu:
                o_ref[0, :, :n_silu] = _silu(res[:, off:off + n_silu]).astype(o_ref.dtype)
            if n_silu < width:
                o_ref[0, :, n_silu:] = res[:, off + n_silu:off + width].astype(o_ref.dtype)
            off += width


def _class_permutation(dilation):
    sub = PROJ_ROWS // dilation
    dst = np.arange(PROJ_ROWS)
    src = dilation * (dst % sub) + dst // sub
    perm = np.zeros((PROJ_ROWS, PROJ_ROWS), np.float32)
    perm[dst, src] = 1.0
    return jnp.asarray(perm, BF16)


def _project(x, w, *, name, dilation=1, tiled_out=False, widths=None, silu_widths=None,
             out_dtype=F32):
    b, s, d = x.shape
    n = w.shape[1]
    steps = s // PROJ_ROWS
    permute = dilation > 1
    in_specs = [pl.BlockSpec((1, PROJ_ROWS, d), lambda i, j: (i, j, 0))]
    args = [x]
    if permute:
        in_specs.append(pl.BlockSpec((PROJ_ROWS, PROJ_ROWS), lambda i, j: (0, 0)))
        args.append(_class_permutation(dilation))
    in_specs.append(pl.BlockSpec((d, n), lambda i, j: (0, 0)))
    args.append(w)
    if tiled_out:
        sub = PROJ_ROWS // dilation
        out_shape = jax.ShapeDtypeStruct((b, n // LANES, dilation, steps, sub, LANES), out_dtype)
        o_spec = pl.BlockSpec((1, n // LANES, dilation, 1, sub, LANES), lambda i, j: (i, 0, 0, j, 0, 0))
    else:
        assert not permute and sum(widths) == n
        out_shape = tuple(jax.ShapeDtypeStruct((b, s, wd), out_dtype) for wd in widths)
        o_spec = tuple(pl.BlockSpec((1, PROJ_ROWS, wd), lambda i, j: (i, j, 0)) for wd in widths)
    out = pl.pallas_call(
        functools.partial(_proj_kernel, permute=permute, tiled_out=tiled_out,
                          silu_widths=silu_widths or (0,) * len(widths or ())),
        grid=(b, steps),
        in_specs=in_specs,
        out_specs=o_spec,
        out_shape=out_shape,
        compiler_params=pltpu.CompilerParams(
            dimension_semantics=("parallel", "parallel"), vmem_limit_bytes=VMEM_LIMIT),
        name=name,
    )(*args)
    return out.reshape(b, n // LANES, s, LANES) if tiled_out else out


def _bucket_tiles():
    qi = np.arange(BLK)[:, None]
    kj = np.arange(2 * BLK)[None, :]
    delta = qi + BLK - kj
    max_exact = NUM_BUCKETS // 2
    out = []
    for window, dil in PATTERNS:
        valid = (delta >= 0) & (delta <= window // dil)
        dist = np.maximum(delta, 0) * dil
        d_f = np.maximum(dist, 1).astype(np.float32)
        large = max_exact + (np.log(d_f / max_exact) / np.float32(math.log(MAX_DISTANCE / max_exact))
                             * (NUM_BUCKETS - max_exact)).astype(np.int32)
        large = np.minimum(large, NUM_BUCKETS - 1)
        bucket = np.where(dist < max_exact, dist, large)
        out.append(np.where(valid, bucket, -1).astype(np.int32))
    return np.stack(out)


def _bias_kernel(tbl_ref, bucket_ref, o_ref):
    col = pl.program_id(0) * HEADS_PER_GROUP + pl.program_id(1)
    bucket = bucket_ref[0]
    acc = jnp.full(bucket.shape, -jnp.inf, F32)
    for bkt in range(NUM_BUCKETS):
        acc = jnp.where(bucket == bkt, LOG2E * tbl_ref[bkt, col], acc)
    o_ref[0, 0] = acc


def _bias_tiles(rel_bias):
    buckets = jnp.asarray(_bucket_tiles())
    return pl.pallas_call(
        _bias_kernel,
        grid=(N_GROUPS, HEADS_PER_GROUP),
        in_specs=[pl.BlockSpec(memory_space=pltpu.SMEM),
                  pl.BlockSpec((1, BLK, 2 * BLK), lambda g, h: (g, 0, 0))],
        out_specs=pl.BlockSpec((1, 1, BLK, 2 * BLK), lambda g, h: (g, h, 0, 0)),
        out_shape=jax.ShapeDtypeStruct((N_GROUPS, HEADS_PER_GROUP, BLK, 2 * BLK), F32),
        name="bias_tiles",
    )(rel_bias, buckets)


ATT_DEPTH = 3


def _head_stack(t):
    lane = lax.broadcasted_iota(jnp.int32, t.shape, 1)
    zero = jnp.zeros_like(t)
    return jnp.concatenate([jnp.where(lane < HEAD_DIM, t, zero),
                            jnp.where(lane >= HEAD_DIM, t, zero)], axis=0)


def _attn_kernel(q1, k1, v1, q2, k2, v2, q3, k3, v3, bias_ref, gate_ref, o_ref,
                 m_scr, l_scr, acc_scr, *, seq):
    left = lax.broadcasted_iota(jnp.int32, (BLK, LANES), 1) < HEAD_DIM
    ones_rows = jnp.ones((2 * BLK, LANES), BF16)
    r_idx = lax.broadcasted_iota(jnp.int32, (2 * BLK, LANES), 0)
    c_idx = lax.broadcasted_iota(jnp.int32, (2 * BLK, LANES), 1)
    ones_heads = jnp.where((r_idx < BLK) == (c_idx < HEAD_DIM), 1.0, 0.0).astype(BF16)

    jobs = []
    for g, (_, dil) in enumerate(PATTERNS):
        sub = seq // dil
        for r in range(dil):
            for n in range(sub // BLK):
                dst = pl.ds(n * BLK * dil + r, BLK, stride=dil) if dil > 1 else pl.ds(n * BLK, BLK)
                jobs.append((g, r * sub + n * BLK, n > 0, dst))
    group_refs = ((q1, k1, v1), (q2, k2, v2), (q3, k3, v3))

    def key_rows(ref, row0, has_prev):
        if has_prev:
            return ref[0, 0, row0 - BLK:row0 + BLK, :]
        return ref[0, 0, row0:row0 + BLK, :]

    def scores_of(job):
        g, row0, has_prev, _ = job
        q_ref, k_ref, _ = group_refs[g]
        q = q_ref[0, 0, row0:row0 + BLK, :]
        kk = key_rows(k_ref, row0, has_prev)
        if has_prev:
            return _dot_nt(_head_stack(q), kk)
        return _dot_nt(q, _head_stack(kk))

    def softmax_of(job, s):
        g, _, has_prev, _ = job
        es, ms = [], []
        for h in range(2):
            if has_prev:
                sh = s[h * BLK:(h + 1) * BLK, :] + bias_ref[g, h]
            else:
                sh = s[:, h * BLK:(h + 1) * BLK] + bias_ref[g, h, :, BLK:]
            mh = jnp.max(sh, axis=-1, keepdims=True)
            es.append(jnp.exp2(sh - mh).astype(BF16))
            ms.append(mh)
        return jnp.concatenate(es, axis=0 if has_prev else 1), jnp.where(left, ms[0], ms[1])

    def output_of(job, p, m_new):
        g, row0, has_prev, dst = job
        vv = key_rows(group_refs[g][2], row0, has_prev)
        m_scr[g, dst, :] = m_new
        if has_prev:
            pv = _dot(p, jnp.concatenate([vv, ones_rows], axis=1))
            acc_scr[g, dst, :] = jnp.where(left, pv[:BLK, :LANES], pv[BLK:, :LANES])
            l_scr[g, dst, :] = jnp.where(left, pv[:BLK, LANES:], pv[BLK:, LANES:])
        else:
            pv = _dot(p, jnp.concatenate([_head_stack(vv), ones_heads], axis=1))
            acc_scr[g, dst, :] = pv[:, :LANES]
            l_scr[g, dst, :] = pv[:, LANES:]

    pending = {u: scores_of(jobs[u]) for u in range(ATT_DEPTH)}
    for u, job in enumerate(jobs):
        p, m_new = softmax_of(job, pending.pop(u))
        if u + ATT_DEPTH < len(jobs):
            pending[u + ATT_DEPTH] = scores_of(jobs[u + ATT_DEPTH])
        output_of(job, p, m_new)

    def combine(n, carry):
        rws = pl.ds(pl.multiple_of(n * BLK, BLK), BLK)
        ms = [m_scr[g, rws, :] for g in range(N_GROUPS)]
        m_all = jnp.maximum(jnp.maximum(ms[0], ms[1]), ms[2])
        ws = [jnp.exp2(m - m_all) for m in ms]
        den = sum(w * l_scr[g, rws, :] for g, w in enumerate(ws))
        num = sum(w * acc_scr[g, rws, :] for g, w in enumerate(ws))
        o_ref[0, 0, rws, :] = (num / den * gate_ref[0, rws, :]).astype(o_ref.dtype)
        return carry

    lax.fori_loop(0, seq // BLK, combine, 0)


def _attention(qkv, bias, gate_dt):
    b, _, s, _ = qkv[0].shape
    in_specs, args = [], []
    for arr in qkv:
        for part in range(3):
            in_specs.append(pl.BlockSpec((1, 1, s, LANES),
                                         lambda i, j, part=part: (i, part * HEAD_PAIRS + j, 0, 0)))
            args.append(arr)
    in_specs.append(pl.BlockSpec((N_GROUPS, 2, BLK, 2 * BLK), lambda i, j: (0, j, 0, 0)))
    in_specs.append(pl.BlockSpec((1, s, LANES), lambda i, j: (i, 0, j)))
    return pl.pallas_call(
        functools.partial(_attn_kernel, seq=s),
        grid=(b, HEAD_PAIRS),
        in_specs=in_specs,
        out_specs=pl.BlockSpec((1, 1, s, LANES), lambda i, j: (i, j, 0, 0)),
        out_shape=jax.ShapeDtypeStruct((b, HEAD_PAIRS, s, LANES), BF16),
        scratch_shapes=[pltpu.VMEM((N_GROUPS, s, LANES), F32)] * 3,
        compiler_params=pltpu.CompilerParams(
            dimension_semantics=("parallel", "parallel"), vmem_limit_bytes=VMEM_LIMIT),
        name="dilated_attention",
    )(*args, bias, gate_dt)


TIME_STRIDE = 4
TIME_SPAN = TIME_STRIDE * SUBLANES
HALO = SUBLANES


def _strided_rows(ref, base):
    pieces = [ref[pl.ds(base + TIME_SPAN * grp + i, SUBLANES, stride=TIME_STRIDE), :]
              for grp in range(CHUNK // TIME_SPAN) for i in range(TIME_STRIDE)]
    return jnp.concatenate(pieces, axis=0)


def _strided_time_of(idx):
    q = idx // SUBLANES
    return TIME_SPAN * (q // TIME_STRIDE) + q % TIME_STRIDE + TIME_STRIDE * (idx % SUBLANES)


def _silu(v):
    h = 0.5 * v
    return h * jnp.tanh(h) + h


def _dot_split2(a_f32, b_bf16):
    hi = a_f32.astype(BF16)
    lo = (a_f32 - hi.astype(F32)).astype(BF16)
    return _dot(hi, b_bf16) + _dot(lo, b_bf16)


def _ssd_kernel(xbc_ref, z_ref, dt_ref, convw_ref, convb_ref, dtb_ref, alog_ref, dskip_ref,
                normw_ref, expand_ref, o_ref, state_scr, xpad_scr, y_scr):
    @pl.when(pl.program_id(1) == 0)
    def _():
        state_scr[...] = jnp.zeros_like(state_scr)
        xpad_scr[:, 0:HALO, :] = jnp.zeros((CONV_DIM // LANES, HALO, LANES), F32)

    t_row = _strided_time_of(lax.broadcasted_iota(jnp.int32, (CHUNK, CHUNK), 0))
    t_col = _strided_time_of(lax.broadcasted_iota(jnp.int32, (CHUNK, CHUNK), 1))
    causal = t_row >= t_col
    tril = jnp.where(causal, 1.0, 0.0).astype(BF16)
    neg_rate = -LOG2E * jnp.exp(alog_ref[...])

    def one_chunk(k, carry):
        _ssd_chunk(pl.multiple_of(k * CHUNK, CHUNK), causal, tril, neg_rate,
                   xbc_ref, z_ref, dt_ref, convw_ref, convb_ref, dtb_ref, dskip_ref, normw_ref,
                   expand_ref, o_ref, state_scr, xpad_scr, y_scr)
        return carry

    lax.fori_loop(0, xbc_ref.shape[1] // CHUNK, one_chunk, 0)


def _ssd_chunk(row0, causal, tril, neg_rate, xbc_ref, z_ref, dt_ref, convw_ref, convb_ref, dtb_ref,
               dskip_ref, normw_ref, expand_ref, o_ref, state_scr, xpad_scr, y_scr):
    x_slabs = D_INNER // LANES
    rows = pl.ds(row0, CHUNK)

    def conv_silu(j):
        lanes = slice(j * LANES, (j + 1) * LANES)
        xpad_scr[j, HALO:HALO + CHUNK, :] = xbc_ref[0, rows, lanes]
        conv = convb_ref[:, lanes]
        for k in range(CONV_WIDTH):
            tap = _strided_rows(xpad_scr.at[j], HALO - (CONV_WIDTH - 1) + k)
            conv = conv + convw_ref[k:k + 1, lanes] * tap
        xpad_scr[j, 0:HALO, :] = xbc_ref[0, pl.ds(row0 + CHUNK - HALO, HALO), lanes]
        return _silu(conv)

    raw = _strided_rows(dt_ref.at[0], row0) + dtb_ref[...]
    dt = jnp.maximum(raw, 0.0) + jnp.log1p(jnp.exp(-jnp.abs(raw)))
    a = dt * neg_rate
    hi, mid, lo = _split3(a)
    acs = _dot(tril, hi) + _dot(tril, mid) + _dot(tril, lo)
    acs_t = acs.T
    acs_end = acs[CHUNK - 1:CHUNK, :]

    def two_pieces(v):
        hi2 = v.astype(BF16)
        return hi2, (v - hi2.astype(F32)).astype(BF16)

    dt_p = two_pieces(dt)
    decay_in_p = two_pieces(jnp.exp2(acs))
    w_out_p = two_pieces(dt * jnp.exp2(acs_end - acs))

    slabs_per_group = GROUP_LANES // LANES
    for g in range(SSM_GROUPS):
        gl = slice(g * GROUP_LANES, (g + 1) * GROUP_LANES)
        expand = expand_ref[:, gl]
        dt_full, decay_in, w_out = (_dot(hi2, expand) + _dot(lo2, expand)
                                    for hi2, lo2 in (dt_p, decay_in_p, w_out_p))
        chunk_decay = decay_in[CHUNK - 1:CHUNK, :]
        xs = jnp.concatenate([conv_silu(g * slabs_per_group + j) for j in range(slabs_per_group)],
                             axis=1)
        bg32 = conv_silu(x_slabs + g)
        bg = bg32.astype(BF16)
        cg = conv_silu(x_slabs + SSM_GROUPS + g).astype(BF16)
        xdt_b = (xs * dt_full).astype(BF16)
        x_out = (xs * w_out).astype(BF16)
        cb = _dot_nt(cg, bg)
        prev = state_scr[g]
        y_off = _dot(cg, prev.astype(BF16)) * decay_in
        state_scr[g] = prev * chunk_decay + _dot(bg32.T.astype(BF16), x_out)
        for j in range(HEADS_PER_SSM_GROUP // 2):
            gs = []
            for hh in (g * HEADS_PER_SSM_GROUP + 2 * j, g * HEADS_PER_SSM_GROUP + 2 * j + 1):
                seg = acs[:, hh:hh + 1] - acs_t[hh:hh + 1, :]
                lmat = jnp.exp2(jnp.where(causal, seg, -jnp.inf))
                gs.append((cb * lmat).astype(BF16))
            slab = g * slabs_per_group + j
            pair = slice(j * LANES, (j + 1) * LANES)
            y_diag = _dot(jnp.concatenate(gs, axis=1), _head_stack(xdt_b[:, pair]))
            y = (y_diag + y_off[:, pair]
                 + dskip_ref[:, slab * LANES:(slab + 1) * LANES] * xs[:, pair])
            for q in range(CHUNK // SUBLANES):
                start = TIME_SPAN * (q // TIME_STRIDE) + q % TIME_STRIDE
                y_scr[slab, pl.ds(start, SUBLANES, stride=TIME_STRIDE), :] = (
                    y[q * SUBLANES:(q + 1) * SUBLANES, :])

        yg = jnp.concatenate([y_scr[g * slabs_per_group + j] for j in range(slabs_per_group)], axis=1)
        u = yg * z_ref[0, rows, gl]
        ms = jnp.mean(jnp.square(u), axis=-1, keepdims=True)
        o_ref[0, rows, gl] = (u * lax.rsqrt(ms + RMS_EPS) * normw_ref[:, gl]).astype(o_ref.dtype)


def _ssd(xbc, z, gate_dt, conv_w, conv_b, dt_bias, a_log, d_skip, norm_w):
    b, s, _ = xbc.shape
    pad = LANES - SSM_HEADS
    dtb = jnp.pad(dt_bias, (0, pad)).reshape(1, LANES)
    alog = jnp.pad(a_log, (0, pad)).reshape(1, LANES)
    dskip = jnp.repeat(d_skip, D_INNER // SSM_HEADS).reshape(1, D_INNER)
    expand = np.zeros((LANES, D_INNER), np.float32)
    expand[np.arange(D_INNER) // (D_INNER // SSM_HEADS), np.arange(D_INNER)] = 1.0
    expand = jnp.asarray(expand, BF16)
    dt_tile = (GATT_END - V_END) // LANES

    def whole(shape):
        return pl.BlockSpec(shape, lambda i, j: (0,) * len(shape))

    return pl.pallas_call(
        _ssd_kernel,
        grid=(b, s // SSD_ROWS),
        in_specs=[pl.BlockSpec((1, SSD_ROWS, CONV_DIM), lambda i, j: (i, j, 0)),
                  pl.BlockSpec((1, SSD_ROWS, D_INNER), lambda i, j: (i, j, 0)),
                  pl.BlockSpec((1, SSD_ROWS, LANES), lambda i, j: (i, j, dt_tile)),
                  whole((CONV_WIDTH, CONV_DIM)), whole((1, CONV_DIM)),
                  whole((1, LANES)), whole((1, LANES)), whole((1, D_INNER)),
                  whole((1, D_INNER)), whole((LANES, D_INNER))],
        out_specs=pl.BlockSpec((1, SSD_ROWS, D_INNER), lambda i, j: (i, j, 0)),
        out_shape=jax.ShapeDtypeStruct((b, s, D_INNER), BF16),
        scratch_shapes=[pltpu.VMEM((SSM_GROUPS, D_STATE, GROUP_LANES), F32),
                        pltpu.VMEM((CONV_DIM // LANES, HALO + CHUNK, LANES), F32),
                        pltpu.VMEM((D_INNER // LANES, CHUNK, LANES), F32)],
        compiler_params=pltpu.CompilerParams(
            dimension_semantics=("parallel", "arbitrary"), vmem_limit_bytes=VMEM_LIMIT),
        name="ssd_mixer",
    )(xbc, z, gate_dt, conv_w, conv_b.reshape(1, CONV_DIM), dtb, alog, dskip,
      norm_w.reshape(1, D_INNER), expand)


def _out_kernel(att_ref, ssm_ref, gm_ref, gp_ref, p_ref, x_ref, wb_ref, wo_ref, wp_ref,
                bg_ref, lng_ref, lnb_ref, o_ref, *, alpha):
    subs = [slice(t * OUT_SUBROWS, (t + 1) * OUT_SUBROWS) for t in range(OUT_ROWS // OUT_SUBROWS)]
    branch = []
    for rs in subs:
        o_att = jnp.concatenate([att_ref[0, j, rs, :] for j in range(HEAD_PAIRS)], axis=1)
        branch.append((_dot(o_att, wb_ref[:ATT_OUT, :]), _dot(ssm_ref[0, rs, :], wb_ref[ATT_OUT:, :]),
                       _dot(p_ref[0, rs, :].astype(BF16), wp_ref[...])))
    mixes = []
    for rs, (y_a, y_b, _) in zip(subs, branch):
        merged = (_sigmoid_t(gm_ref[0, rs, :D_MODEL] + bg_ref[0:1, :]) * y_a
                  + _sigmoid_t(gm_ref[0, rs, D_MODEL:] + bg_ref[1:2, :]) * y_b)
        mixes.append(_dot(merged.astype(BF16), wo_ref[...]))
    for rs, (_, _, pp), mix in zip(subs, branch, mixes):
        ple = _sigmoid_t(gp_ref[0, rs, :] + bg_ref[2:3, :]) * pp
        h = alpha * x_ref[0, rs, :] + mix + ple
        mu = jnp.mean(h, axis=-1, keepdims=True)
        var = jnp.mean(jnp.square(h - mu), axis=-1, keepdims=True)
        o_ref[0, rs, :] = (h - mu) * lax.rsqrt(var + LN_EPS) * lng_ref[...] + lnb_ref[...]


def _output(o_att, y_ssm, g_merge, g_ple, p_i, x, w_branch, w_out, w_ple, b_gate, ln_g, ln_b, alpha):
    b, s, d = x.shape
    rows = OUT_ROWS

    def whole(shape):
        return pl.BlockSpec(shape, lambda i, j: (0,) * len(shape), pipeline_mode=pl.Buffered(1))

    def tile(n):
        return pl.BlockSpec((1, rows, n), lambda i, j: (i, j, 0))

    return pl.pallas_call(
        functools.partial(_out_kernel, alpha=alpha),
        grid=(b, s // rows),
        in_specs=[pl.BlockSpec((1, HEAD_PAIRS, rows, LANES), lambda i, j: (i, 0, j, 0)),
                  tile(D_INNER), tile(2 * D_MODEL), tile(D_MODEL), tile(PLE_DIM), tile(D_MODEL),
                  whole((ATT_OUT + D_INNER, D_MODEL)), whole((D_MODEL, D_MODEL)),
                  whole((PLE_DIM, D_MODEL)), whole((3, D_MODEL)), whole((1, D_MODEL)),
                  whole((1, D_MODEL))],
        out_specs=tile(D_MODEL),
        out_shape=jax.ShapeDtypeStruct((b, s, d), F32),
        compiler_params=pltpu.CompilerParams(
            dimension_semantics=("parallel", "parallel"), vmem_limit_bytes=VMEM_LIMIT),
        name="merge_out_ln",
    )(o_att, y_ssm, g_merge, g_ple, p_i, x, w_branch.astype(BF16), w_out.astype(BF16),
      w_ple.astype(BF16), b_gate, ln_g.reshape(1, d), ln_b.reshape(1, d))


def _layer(x, p_i, w_in, b_gate, conv_w, conv_b, dt_bias, a_log, d_skip, ssm_norm_w,
           w_branch, w_out, w_ple, ln_g, ln_b, bias, alpha):
    w = w_in.astype(BF16)
    scale = LOG2E * HEAD_DIM ** -0.5
    group_w = HEADS_PER_GROUP * HEAD_DIM
    qkv = []
    for g, (_, dil) in enumerate(PATTERNS):
        cols = slice(g * group_w, (g + 1) * group_w)
        wg = jnp.concatenate([(w_in[:, :Q_END][:, cols] * scale).astype(BF16),
                              w[:, Q_END:K_END][:, cols], w[:, K_END:V_END][:, cols]], axis=1)
        qkv.append(_project(x, wg, dilation=dil, tiled_out=True, out_dtype=BF16, name=f"proj_qkv{g}"))
    w_gates = jnp.concatenate(
        [w[:, V_END:GATT_END], jnp.pad(w[:, XBC_END:DT_END], ((0, 0), (0, LANES - SSM_HEADS))),
         w[:, GMERGE_END:]], axis=1)
    gate_dt, g_ple = _project(x, w_gates, widths=(ATT_OUT + LANES, D_MODEL),
                              silu_widths=(ATT_OUT, 0), name="proj_gates")
    w_z_gm = jnp.concatenate([w[:, GATT_END:Z_END], w[:, DT_END:GMERGE_END]], axis=1)
    z_gate, g_merge = _project(x, w_z_gm, widths=(D_INNER, 2 * D_MODEL),
                               silu_widths=(D_INNER, 0), name="proj_z_gmerge")
    (xbc,) = _project(x, w[:, Z_END:XBC_END], widths=(CONV_DIM,), name="proj_xbc")

    o_att = _attention(qkv, bias, gate_dt)
    y_ssm = _ssd(xbc, z_gate, gate_dt, conv_w, conv_b, dt_bias, a_log, d_skip, ssm_norm_w)
    return _output(o_att, y_ssm, g_merge, g_ple, p_i, x, w_branch, w_out, w_ple, b_gate,
                   ln_g, ln_b, alpha)


def kernel(x, p, w_in, b_gate, conv_w, conv_b, dt_bias, a_log, d_skip, ssm_norm_w, w_branch, w_out,
           w_ple, ln_g, ln_b, rel_bias):
    depth = w_in.shape[0]
    alpha = (2.0 * depth) ** 0.25
    bias = _bias_tiles(rel_bias)
    for i in range(depth):
        x = _layer(x, p[i], w_in[i], b_gate[i], conv_w[i], conv_b[i], dt_bias[i], a_log[i],
                   d_skip[i], ssm_norm_w[i], w_branch[i], w_out[i], w_ple[i], ln_g[i], ln_b[i],
                   bias, alpha)
    return x
```

```python
import functools
import math

import numpy as np
import jax
import jax.numpy as jnp
from jax import lax
from jax.experimental import pallas as pl
from jax.experimental.pallas import tpu as pltpu

F32 = jnp.float32
BF16 = jnp.bfloat16

LANES = 128
SUBLANES = 8
V7X_VMEM_BYTES = 64 * 1024 * 1024
VMEM_LIMIT = (V7X_VMEM_BYTES * 3) // 4

D_MODEL = 1024
HEAD_DIM = 64
HEADS_PER_GROUP = 12
PATTERNS = ((128, 1), (512, 4), (2048, 16))
N_GROUPS = 3
ATT_HEADS = N_GROUPS * HEADS_PER_GROUP
ATT_QKV = ATT_HEADS * HEAD_DIM
ATT_OUT = HEADS_PER_GROUP * HEAD_DIM
BLK = 128
NUM_BUCKETS = 32
MAX_DISTANCE = 2048
D_INNER = 2048
SSM_HEADS = 32
SSM_GROUPS = 4
D_STATE = 128
CONV_WIDTH = 4
CONV_DIM = D_INNER + 2 * SSM_GROUPS * D_STATE
CHUNK = 128
PLE_DIM = 256
LN_EPS = 1e-5
RMS_EPS = 1e-5
LOG2E = math.log2(math.e)
Q_END = ATT_QKV
K_END = Q_END + ATT_QKV
V_END = K_END + ATT_QKV
GATT_END = V_END + ATT_OUT
Z_END = GATT_END + D_INNER
XBC_END = Z_END + CONV_DIM
DT_END = XBC_END + SSM_HEADS
GMERGE_END = DT_END + 2 * D_MODEL

HEAD_PAIRS = HEADS_PER_GROUP // 2
QKV_TILES = 3 * HEAD_PAIRS
GROUP_LANES = D_INNER // SSM_GROUPS
HEADS_PER_SSM_GROUP = SSM_HEADS // SSM_GROUPS
PROJ_ROWS = 512
SSD_ROWS = 4 * CHUNK
OUT_ROWS = 512
OUT_SUBROWS = 256


def _sigmoid_t(v):
    return 0.5 * jnp.tanh(0.5 * v) + 0.5


def _split3(v):
    hi = v.astype(BF16)
    r1 = v - hi.astype(F32)
    mid = r1.astype(BF16)
    lo = (r1 - mid.astype(F32)).astype(BF16)
    return hi, mid, lo


def _dot(a, b):
    return jnp.dot(a, b, preferred_element_type=F32)


def _dot_nt(a, b):
    return lax.dot_general(a, b, (((1,), (1,)), ((), ())), preferred_element_type=F32)


def _dot_tn(a, b):
    return lax.dot_general(a, b, (((0,), (0,)), ((), ())), preferred_element_type=F32)


def _dot_exact_rhs(a_f32, b_bf16):
    hi, mid, lo = _split3(a_f32)
    return _dot(hi, b_bf16) + _dot(mid, b_bf16) + _dot(lo, b_bf16)


def _proj_kernel(*refs, permute, tiled_out, silu_widths):
    if permute:
        x_ref, perm_ref, w_ref, *o_refs = refs
    else:
        x_ref, w_ref, *o_refs = refs
    lhs = x_ref[0].astype(BF16)
    if permute:
        lhs = _dot(perm_ref[...], lhs).astype(BF16)
    res = _dot_nt(lhs, w_ref[...])
    if tiled_out:
        (o_ref,) = o_refs
        _, n_tiles, n_classes, _, sub, _ = o_ref.shape
        for c in range(n_tiles):
            tile = res[:, c * LANES:(c + 1) * LANES].reshape(n_classes, sub, LANES)
            o_ref[0, c, :, 0] = tile.astype(o_ref.dtype)
    else:
        off = 0
        for o_ref, n_silu in zip(o_refs, silu_widths):
            width = o_ref.shape[-1]
            if n_silu:
                o_ref[0, :, :n_silu] = _silu(res[:, off:off + n_silu]).astype(o_ref.dtype)
            if n_silu < width:
                o_ref[0, :, n_silu:] = res[:, off + n_silu:off + width].astype(o_ref.dtype)
            off += width


def _class_permutation(dilation):
    sub = PROJ_ROWS // dilation
    dst = np.arange(PROJ_ROWS)
    src = dilation * (dst % sub) + dst // sub
    perm = np.zeros((PROJ_ROWS, PROJ_ROWS), np.float32)
    perm[dst, src] = 1.0
    return jnp.asarray(perm, BF16)


def _project(x, w, *, name, dilation=1, tiled_out=False, widths=None, silu_widths=None,
             out_dtype=F32):
    b, s, d = x.shape
    n = w.shape[0]
    steps = s // PROJ_ROWS
    permute = dilation > 1
    in_specs = [pl.BlockSpec((1, PROJ_ROWS, d), lambda i, j: (i, j, 0))]
    args = [x]
    if permute:
        in_specs.append(pl.BlockSpec((PROJ_ROWS, PROJ_ROWS), lambda i, j: (0, 0)))
        args.append(_class_permutation(dilation))
    in_specs.append(pl.BlockSpec((n, d), lambda i, j: (0, 0)))
    args.append(w)
    if tiled_out:
        sub = PROJ_ROWS // dilation
        out_shape = jax.ShapeDtypeStruct((b, n // LANES, dilation, steps, sub, LANES), out_dtype)
        o_spec = pl.BlockSpec((1, n // LANES, dilation, 1, sub, LANES), lambda i, j: (i, 0, 0, j, 0, 0))
    else:
        assert not permute and sum(widths) == n
        out_shape = tuple(jax.ShapeDtypeStruct((b, s, wd), out_dtype) for wd in widths)
        o_spec = tuple(pl.BlockSpec((1, PROJ_ROWS, wd), lambda i, j: (i, j, 0)) for wd in widths)
    out = pl.pallas_call(
        functools.partial(_proj_kernel, permute=permute, tiled_out=tiled_out,
                          silu_widths=silu_widths or (0,) * len(widths or ())),
        grid=(b, steps),
        in_specs=in_specs,
        out_specs=o_spec,
        out_shape=out_shape,
        compiler_params=pltpu.CompilerParams(
            dimension_semantics=("parallel", "parallel"), vmem_limit_bytes=VMEM_LIMIT),
        name=name,
    )(*args)
    return out.reshape(b, n // LANES, s, LANES) if tiled_out else out


def _bucket_tiles():
    qi = np.arange(BLK)[:, None]
    kj = np.arange(2 * BLK)[None, :]
    delta = qi + BLK - kj
    max_exact = NUM_BUCKETS // 2
    out = []
    for window, dil in PATTERNS:
        valid = (delta >= 0) & (delta <= window // dil)
        dist = np.maximum(delta, 0) * dil
        d_f = np.maximum(dist, 1).astype(np.float32)
        large = max_exact + (np.log(d_f / max_exact) / np.float32(math.log(MAX_DISTANCE / max_exact))
                             * (NUM_BUCKETS - max_exact)).astype(np.int32)
        large = np.minimum(large, NUM_BUCKETS - 1)
        bucket = np.where(dist < max_exact, dist, large)
        out.append(np.where(valid, bucket, -1).astype(np.int32))
    return np.stack(out)


def _bias_kernel(tbl_ref, bucket_ref, o_ref):
    col = pl.program_id(0) * HEADS_PER_GROUP + pl.program_id(1)
    bucket = bucket_ref[0]
    acc = jnp.full(bucket.shape, -jnp.inf, F32)
    for bkt in range(NUM_BUCKETS):
        acc = jnp.where(bucket == bkt, LOG2E * tbl_ref[bkt, col], acc)
    o_ref[0, 0] = acc


def _bias_tiles(rel_bias):
    buckets = jnp.asarray(_bucket_tiles())
    return pl.pallas_call(
        _bias_kernel,
        grid=(N_GROUPS, HEADS_PER_GROUP),
        in_specs=[pl.BlockSpec(memory_space=pltpu.SMEM),
                  pl.BlockSpec((1, BLK, 2 * BLK), lambda g, h: (g, 0, 0))],
        out_specs=pl.BlockSpec((1, 1, BLK, 2 * BLK), lambda g, h: (g, h, 0, 0)),
        out_shape=jax.ShapeDtypeStruct((N_GROUPS, HEADS_PER_GROUP, BLK, 2 * BLK), F32),
        name="bias_tiles",
    )(rel_bias, buckets)


ATT_DEPTH = 3


def _head_stack(t):
    lane = lax.broadcasted_iota(jnp.int32, t.shape, 1)
    zero = jnp.zeros_like(t)
    return jnp.concatenate([jnp.where(lane < HEAD_DIM, t, zero),
                            jnp.where(lane >= HEAD_DIM, t, zero)], axis=0)


def _attn_kernel(q1, k1, v1, q2, k2, v2, q3, k3, v3, bias_ref, gate_ref, o_ref,
                 m_scr, l_scr, acc_scr, *, seq):
    left = lax.broadcasted_iota(jnp.int32, (BLK, LANES), 1) < HEAD_DIM
    ones_rows = jnp.ones((2 * BLK, LANES), BF16)
    r_idx = lax.broadcasted_iota(jnp.int32, (2 * BLK, LANES), 0)
    c_idx = lax.broadcasted_iota(jnp.int32, (2 * BLK, LANES), 1)
    ones_heads = jnp.where((r_idx < BLK) == (c_idx < HEAD_DIM), 1.0, 0.0).astype(BF16)

    jobs = []
    for g, (_, dil) in enumerate(PATTERNS):
        sub = seq // dil
        for r in range(dil):
            for n in range(sub // BLK):
                dst = pl.ds(n * BLK * dil + r, BLK, stride=dil) if dil > 1 else pl.ds(n * BLK, BLK)
                jobs.append((g, r * sub + n * BLK, n > 0, dst))
    group_refs = ((q1, k1, v1), (q2, k2, v2), (q3, k3, v3))

    def key_rows(ref, row0, has_prev):
        if has_prev:
            return ref[0, 0, row0 - BLK:row0 + BLK, :]
        return ref[0, 0, row0:row0 + BLK, :]

    def scores_of(job):
        g, row0, has_prev, _ = job
        q_ref, k_ref, _ = group_refs[g]
        q = q_ref[0, 0, row0:row0 + BLK, :]
        kk = key_rows(k_ref, row0, has_prev)
        if has_prev:
            return _dot_nt(_head_stack(q), kk)
        return _dot_nt(q, _head_stack(kk))

    def softmax_of(job, s):
        g, _, has_prev, _ = job
        es, ms = [], []
        for h in range(2):
            if has_prev:
                sh = s[h * BLK:(h + 1) * BLK, :] + bias_ref[g, h]
            else:
                sh = s[:, h * BLK:(h + 1) * BLK] + bias_ref[g, h, :, BLK:]
            mh = jnp.max(sh, axis=-1, keepdims=True)
            es.append(jnp.exp2(sh - mh).astype(BF16))
            ms.append(mh)
        return jnp.concatenate(es, axis=0 if has_prev else 1), jnp.where(left, ms[0], ms[1])

    def output_of(job, p, m_new):
        g, row0, has_prev, dst = job
        vv = key_rows(group_refs[g][2], row0, has_prev)
        if has_prev:
            pv = _dot(p, jnp.concatenate([vv, ones_rows], axis=1))
            acc = jnp.where(left, pv[:BLK, :LANES], pv[BLK:, :LANES])
            l_new = jnp.where(left, pv[:BLK, LANES:], pv[BLK:, LANES:])
        else:
            pv = _dot(p, jnp.concatenate([_head_stack(vv), ones_heads], axis=1))
            acc, l_new = pv[:, :LANES], pv[:, LANES:]
        if g > 0:
            m_scr[g - 1, dst, :] = m_new
            l_scr[g - 1, dst, :] = l_new
            acc_scr[g - 1, dst, :] = acc
            return
        ms = [m_new] + [m_scr[k, dst, :] for k in range(N_GROUPS - 1)]
        ls = [l_new] + [l_scr[k, dst, :] for k in range(N_GROUPS - 1)]
        accs = [acc] + [acc_scr[k, dst, :] for k in range(N_GROUPS - 1)]
        m_all = jnp.maximum(jnp.maximum(ms[0], ms[1]), ms[2])
        ws = [jnp.exp2(m - m_all) for m in ms]
        den = sum(w * l for w, l in zip(ws, ls))
        num = sum(w * a for w, a in zip(ws, accs))
        o_ref[0, 0, dst, :] = (num / den * gate_ref[0, dst, :]).astype(o_ref.dtype)

    jobs.sort(key=lambda job: -job[0])
    pending = {u: scores_of(jobs[u]) for u in range(ATT_DEPTH)}
    for u, job in enumerate(jobs):
        p, m_new = softmax_of(job, pending.pop(u))
        if u + ATT_DEPTH < len(jobs):
            pending[u + ATT_DEPTH] = scores_of(jobs[u + ATT_DEPTH])
        output_of(job, p, m_new)


def _attention(qkv, bias, gate_dt):
    b, _, s, _ = qkv[0].shape
    in_specs, args = [], []
    for arr in qkv:
        for part in range(3):
            in_specs.append(pl.BlockSpec((1, 1, s, LANES),
                                         lambda i, j, part=part: (i, part * HEAD_PAIRS + j, 0, 0)))
            args.append(arr)
    in_specs.append(pl.BlockSpec((N_GROUPS, 2, BLK, 2 * BLK), lambda i, j: (0, j, 0, 0)))
    in_specs.append(pl.BlockSpec((1, s, LANES), lambda i, j: (i, 0, j)))
    return pl.pallas_call(
        functools.partial(_attn_kernel, seq=s),
        grid=(b, HEAD_PAIRS),
        in_specs=in_specs,
        out_specs=pl.BlockSpec((1, 1, s, LANES), lambda i, j: (i, j, 0, 0)),
        out_shape=jax.ShapeDtypeStruct((b, HEAD_PAIRS, s, LANES), BF16),
        scratch_shapes=[pltpu.VMEM((N_GROUPS - 1, s, LANES), F32)] * 3,
        compiler_params=pltpu.CompilerParams(
            dimension_semantics=("parallel", "parallel"), vmem_limit_bytes=VMEM_LIMIT),
        name="dilated_attention",
    )(*args, bias, gate_dt)


TIME_STRIDE = 4
TIME_SPAN = TIME_STRIDE * SUBLANES
HALO = SUBLANES


def _strided_rows(ref, base):
    pieces = [ref[pl.ds(base + TIME_SPAN * grp + i, SUBLANES, stride=TIME_STRIDE), :]
              for grp in range(CHUNK // TIME_SPAN) for i in range(TIME_STRIDE)]
    return jnp.concatenate(pieces, axis=0)


def _strided_time_of(idx):
    q = idx // SUBLANES
    return TIME_SPAN * (q // TIME_STRIDE) + q % TIME_STRIDE + TIME_STRIDE * (idx % SUBLANES)


def _silu(v):
    h = 0.5 * v
    return h * jnp.tanh(h) + h


def _dot_split2(a_f32, b_bf16):
    hi = a_f32.astype(BF16)
    lo = (a_f32 - hi.astype(F32)).astype(BF16)
    return _dot(hi, b_bf16) + _dot(lo, b_bf16)


def _ssd_kernel(xbc_ref, z_ref, dt_ref, convw_ref, convb_ref, dtb_ref, alog_ref, dskip_ref,
                normw_ref, expand_ref, o_ref, state_scr, xpad_scr, y_scr):
    @pl.when(pl.program_id(1) == 0)
    def _():
        state_scr[...] = jnp.zeros_like(state_scr)
        xpad_scr[:, 0:HALO, :] = jnp.zeros((CONV_DIM // LANES, HALO, LANES), F32)

    t_row = _strided_time_of(lax.broadcasted_iota(jnp.int32, (CHUNK, CHUNK), 0))
    t_col = _strided_time_of(lax.broadcasted_iota(jnp.int32, (CHUNK, CHUNK), 1))
    causal = t_row >= t_col
    tril = jnp.where(causal, 1.0, 0.0).astype(BF16)
    neg_rate = -LOG2E * jnp.exp(alog_ref[...])

    def one_chunk(k, carry):
        _ssd_chunk(pl.multiple_of(k * CHUNK, CHUNK), causal, tril, neg_rate,
                   xbc_ref, z_ref, dt_ref, convw_ref, convb_ref, dtb_ref, dskip_ref, normw_ref,
                   expand_ref, o_ref, state_scr, xpad_scr, y_scr)
        return carry

    lax.fori_loop(0, xbc_ref.shape[1] // CHUNK, one_chunk, 0)


def _ssd_chunk(row0, causal, tril, neg_rate, xbc_ref, z_ref, dt_ref, convw_ref, convb_ref, dtb_ref,
               dskip_ref, normw_ref, expand_ref, o_ref, state_scr, xpad_scr, y_scr):
    x_slabs = D_INNER // LANES
    rows = pl.ds(row0, CHUNK)

    def conv_silu(j):
        lanes = slice(j * LANES, (j + 1) * LANES)
        xpad_scr[j, HALO:HALO + CHUNK, :] = xbc_ref[0, rows, lanes]
        conv = convb_ref[:, lanes]
        for k in range(CONV_WIDTH):
            tap = _strided_rows(xpad_scr.at[j], HALO - (CONV_WIDTH - 1) + k)
            conv = conv + convw_ref[k:k + 1, lanes] * tap
        xpad_scr[j, 0:HALO, :] = xbc_ref[0, pl.ds(row0 + CHUNK - HALO, HALO), lanes]
        return _silu(conv)

    raw = _strided_rows(dt_ref.at[0], row0) + dtb_ref[...]
    dt = jnp.maximum(raw, 0.0) + jnp.log1p(jnp.exp(-jnp.abs(raw)))
    a = dt * neg_rate
    hi, mid, lo = _split3(a)
    acs = _dot(tril, hi) + _dot(tril, mid) + _dot(tril, lo)
    acs_t = acs.T
    acs_end = acs[CHUNK - 1:CHUNK, :]

    def two_pieces(v):
        hi2 = v.astype(BF16)
        return hi2, (v - hi2.astype(F32)).astype(BF16)

    dt_p = two_pieces(dt)
    decay_in_p = two_pieces(jnp.exp2(acs))
    w_out_p = two_pieces(dt * jnp.exp2(acs_end - acs))

    slabs_per_group = GROUP_LANES // LANES
    for g in range(SSM_GROUPS):
        gl = slice(g * GROUP_LANES, (g + 1) * GROUP_LANES)
        expand = expand_ref[:, gl]
        dt_full, decay_in, w_out = (_dot(hi2, expand) + _dot(lo2, expand)
                                    for hi2, lo2 in (dt_p, decay_in_p, w_out_p))
        chunk_decay = decay_in[CHUNK - 1:CHUNK, :]
        xs = jnp.concatenate([conv_silu(g * slabs_per_group + j) for j in range(slabs_per_group)],
                             axis=1)
        bg32 = conv_silu(x_slabs + g)
        bg = bg32.astype(BF16)
        cg = conv_silu(x_slabs + SSM_GROUPS + g).astype(BF16)
        xdt_b = (xs * dt_full).astype(BF16)
        x_out = (xs * w_out).astype(BF16)
        cb = _dot_nt(cg, bg)
        prev = state_scr[g]
        y_off = _dot(cg, prev.astype(BF16)) * decay_in
        state_scr[g] = prev * chunk_decay + _dot(bg32.T.astype(BF16), x_out)
        for j in range(HEADS_PER_SSM_GROUP // 2):
            gs = []
            for hh in (g * HEADS_PER_SSM_GROUP + 2 * j, g * HEADS_PER_SSM_GROUP + 2 * j + 1):
                seg = acs[:, hh:hh + 1] - acs_t[hh:hh + 1, :]
                lmat = jnp.exp2(jnp.where(causal, seg, -jnp.inf))
                gs.append((cb * lmat).astype(BF16))
            slab = g * slabs_per_group + j
            pair = slice(j * LANES, (j + 1) * LANES)
            y_diag = _dot(jnp.concatenate(gs, axis=1), _head_stack(xdt_b[:, pair]))
            y = (y_diag + y_off[:, pair]
                 + dskip_ref[:, slab * LANES:(slab + 1) * LANES] * xs[:, pair])
            for q in range(CHUNK // SUBLANES):
                start = TIME_SPAN * (q // TIME_STRIDE) + q % TIME_STRIDE
                y_scr[slab, pl.ds(start, SUBLANES, stride=TIME_STRIDE), :] = (
                    y[q * SUBLANES:(q + 1) * SUBLANES, :])

        yg = jnp.concatenate([y_scr[g * slabs_per_group + j] for j in range(slabs_per_group)], axis=1)
        u = yg * z_ref[0, rows, gl]
        ms = jnp.mean(jnp.square(u), axis=-1, keepdims=True)
        o_ref[0, rows, gl] = (u * lax.rsqrt(ms + RMS_EPS) * normw_ref[:, gl]).astype(o_ref.dtype)


def _ssd(xbc, z, gate_dt, conv_w, conv_b, dt_bias, a_log, d_skip, norm_w):
    b, s, _ = xbc.shape
    pad = LANES - SSM_HEADS
    dtb = jnp.pad(dt_bias, (0, pad)).reshape(1, LANES)
    alog = jnp.pad(a_log, (0, pad)).reshape(1, LANES)
    dskip = jnp.repeat(d_skip, D_INNER // SSM_HEADS).reshape(1, D_INNER)
    expand = np.zeros((LANES, D_INNER), np.float32)
    expand[np.arange(D_INNER) // (D_INNER // SSM_HEADS), np.arange(D_INNER)] = 1.0
    expand = jnp.asarray(expand, BF16)
    dt_tile = (GATT_END - V_END) // LANES

    def whole(shape):
        return pl.BlockSpec(shape, lambda i, j: (0,) * len(shape))

    return pl.pallas_call(
        _ssd_kernel,
        grid=(b, s // SSD_ROWS),
        in_specs=[pl.BlockSpec((1, SSD_ROWS, CONV_DIM), lambda i, j: (i, j, 0)),
                  pl.BlockSpec((1, SSD_ROWS, D_INNER), lambda i, j: (i, j, 0)),
                  pl.BlockSpec((1, SSD_ROWS, LANES), lambda i, j: (i, j, dt_tile)),
                  whole((CONV_WIDTH, CONV_DIM)), whole((1, CONV_DIM)),
                  whole((1, LANES)), whole((1, LANES)), whole((1, D_INNER)),
                  whole((1, D_INNER)), whole((LANES, D_INNER))],
        out_specs=pl.BlockSpec((1, SSD_ROWS, D_INNER), lambda i, j: (i, j, 0)),
        out_shape=jax.ShapeDtypeStruct((b, s, D_INNER), BF16),
        scratch_shapes=[pltpu.VMEM((SSM_GROUPS, D_STATE, GROUP_LANES), F32),
                        pltpu.VMEM((CONV_DIM // LANES, HALO + CHUNK, LANES), F32),
                        pltpu.VMEM((D_INNER // LANES, CHUNK, LANES), F32)],
        compiler_params=pltpu.CompilerParams(
            dimension_semantics=("parallel", "arbitrary"), vmem_limit_bytes=VMEM_LIMIT),
        name="ssd_mixer",
    )(xbc, z, gate_dt, conv_w, conv_b.reshape(1, CONV_DIM), dtb, alog, dskip,
      norm_w.reshape(1, D_INNER), expand)


def _out_kernel(att_ref, ssm_ref, gm_ref, gp_ref, p_ref, x_ref, wb_ref, wo_ref, wp_ref,
                bg_ref, lng_ref, lnb_ref, o_ref, *, alpha):
    subs = [slice(t * OUT_SUBROWS, (t + 1) * OUT_SUBROWS) for t in range(OUT_ROWS // OUT_SUBROWS)]
    branch = []
    for rs in subs:
        o_att = jnp.concatenate([att_ref[0, j, rs, :] for j in range(HEAD_PAIRS)], axis=1)
        branch.append((_dot(o_att, wb_ref[:ATT_OUT, :]), _dot(ssm_ref[0, rs, :], wb_ref[ATT_OUT:, :]),
                       _dot(p_ref[0, rs, :].astype(BF16), wp_ref[...])))
    mixes = []
    for rs, (y_a, y_b, _) in zip(subs, branch):
        merged = (_sigmoid_t(gm_ref[0, rs, :D_MODEL] + bg_ref[0:1, :]) * y_a
                  + _sigmoid_t(gm_ref[0, rs, D_MODEL:] + bg_ref[1:2, :]) * y_b)
        mixes.append(_dot(merged.astype(BF16), wo_ref[...]))
    for rs, (_, _, pp), mix in zip(subs, branch, mixes):
        ple = _sigmoid_t(gp_ref[0, rs, :] + bg_ref[2:3, :]) * pp
        h = alpha * x_ref[0, rs, :] + mix + ple
        mu = jnp.mean(h, axis=-1, keepdims=True)
        var = jnp.mean(jnp.square(h - mu), axis=-1, keepdims=True)
        o_ref[0, rs, :] = (h - mu) * lax.rsqrt(var + LN_EPS) * lng_ref[...] + lnb_ref[...]


def _output(o_att, y_ssm, g_merge, g_ple, p_i, x, w_branch, w_out, w_ple, b_gate, ln_g, ln_b, alpha):
    b, s, d = x.shape
    rows = OUT_ROWS

    def whole(shape):
        return pl.BlockSpec(shape, lambda i, j: (0,) * len(shape), pipeline_mode=pl.Buffered(1))

    def tile(n):
        return pl.BlockSpec((1, rows, n), lambda i, j: (i, j, 0))

    return pl.pallas_call(
        functools.partial(_out_kernel, alpha=alpha),
        grid=(b, s // rows),
        in_specs=[pl.BlockSpec((1, HEAD_PAIRS, rows, LANES), lambda i, j: (i, 0, j, 0)),
                  tile(D_INNER), tile(2 * D_MODEL), tile(D_MODEL), tile(PLE_DIM), tile(D_MODEL),
                  whole((ATT_OUT + D_INNER, D_MODEL)), whole((D_MODEL, D_MODEL)),
                  whole((PLE_DIM, D_MODEL)), whole((3, D_MODEL)), whole((1, D_MODEL)),
                  whole((1, D_MODEL))],
        out_specs=tile(D_MODEL),
        out_shape=jax.ShapeDtypeStruct((b, s, d), F32),
        compiler_params=pltpu.CompilerParams(
            dimension_semantics=("parallel", "parallel"), vmem_limit_bytes=VMEM_LIMIT),
        name="merge_out_ln",
    )(o_att, y_ssm, g_merge, g_ple, p_i, x, w_branch.astype(BF16), w_out.astype(BF16),
      w_ple.astype(BF16), b_gate, ln_g.reshape(1, d), ln_b.reshape(1, d))


def _layer(x, p_i, w_in, b_gate, conv_w, conv_b, dt_bias, a_log, d_skip, ssm_norm_w,
           w_branch, w_out, w_ple, ln_g, ln_b, bias, alpha):
    row_scale = jnp.ones((w_in.shape[1], 1), F32).at[:Q_END].set(LOG2E * HEAD_DIM ** -0.5)
    wt = (jnp.swapaxes(w_in, 0, 1) * row_scale).astype(BF16)
    group_w = HEADS_PER_GROUP * HEAD_DIM
    qkv = []
    for g, (_, dil) in enumerate(PATTERNS):
        wg = jnp.concatenate([wt[part + g * group_w:part + (g + 1) * group_w]
                              for part in (0, Q_END, K_END)], axis=0)
        qkv.append(_project(x, wg, dilation=dil, tiled_out=True, out_dtype=BF16, name=f"proj_qkv{g}"))
    w_gates = jnp.concatenate(
        [wt[V_END:GATT_END], jnp.pad(wt[XBC_END:DT_END], ((0, LANES - SSM_HEADS), (0, 0))),
         wt[GMERGE_END:]], axis=0)
    gate_dt, g_ple = _project(x, w_gates, widths=(ATT_OUT + LANES, D_MODEL),
                              silu_widths=(ATT_OUT, 0), name="proj_gates")
    w_z_gm = jnp.concatenate([wt[GATT_END:Z_END], wt[DT_END:GMERGE_END]], axis=0)
    z_gate, g_merge = _project(x, w_z_gm, widths=(D_INNER, 2 * D_MODEL),
                               silu_widths=(D_INNER, 0), name="proj_z_gmerge")
    (xbc,) = _project(x, wt[Z_END:XBC_END], widths=(CONV_DIM,), name="proj_xbc")

    o_att = _attention(qkv, bias, gate_dt)
    y_ssm = _ssd(xbc, z_gate, gate_dt, conv_w, conv_b, dt_bias, a_log, d_skip, ssm_norm_w)
    return _output(o_att, y_ssm, g_merge, g_ple, p_i, x, w_branch, w_out, w_ple, b_gate,
                   ln_g, ln_b, alpha)


def kernel(x, p, w_in, b_gate, conv_w, conv_b, dt_bias, a_log, d_skip, ssm_norm_w, w_branch, w_out,
           w_ple, ln_g, ln_b, rel_bias):
    depth = w_in.shape[0]
    alpha = (2.0 * depth) ** 0.25
    bias = _bias_tiles(rel_bias)
    for i in range(depth):
        x = _layer(x, p[i], w_in[i], b_gate[i], conv_w[i], conv_b[i], dt_bias[i], a_log[i],
                   d_skip[i], ssm_norm_w[i], w_branch[i], w_out[i], w_ple[i], ln_g[i], ln_b[i],
                   bias, alpha)
    return x
```

```python
import functools
import math

import numpy as np
import jax
import jax.numpy as jnp
from jax import lax
from jax.experimental import pallas as pl
from jax.experimental.pallas import tpu as pltpu

F32 = jnp.float32
BF16 = jnp.bfloat16

LANES = 128
SUBLANES = 8
V7X_VMEM_BYTES = 64 * 1024 * 1024
VMEM_LIMIT = (V7X_VMEM_BYTES * 3) // 4

D_MODEL = 1024
HEAD_DIM = 64
HEADS_PER_GROUP = 12
PATTERNS = ((128, 1), (512, 4), (2048, 16))
N_GROUPS = 3
ATT_HEADS = N_GROUPS * HEADS_PER_GROUP
ATT_QKV = ATT_HEADS * HEAD_DIM
ATT_OUT = HEADS_PER_GROUP * HEAD_DIM
BLK = 128
NUM_BUCKETS = 32
MAX_DISTANCE = 2048
D_INNER = 2048
SSM_HEADS = 32
SSM_GROUPS = 4
D_STATE = 128
CONV_WIDTH = 4
CONV_DIM = D_INNER + 2 * SSM_GROUPS * D_STATE
CHUNK = 128
PLE_DIM = 256
LN_EPS = 1e-5
RMS_EPS = 1e-5
LOG2E = math.log2(math.e)
Q_END = ATT_QKV
K_END = Q_END + ATT_QKV
V_END = K_END + ATT_QKV
GATT_END = V_END + ATT_OUT
Z_END = GATT_END + D_INNER
XBC_END = Z_END + CONV_DIM
DT_END = XBC_END + SSM_HEADS
GMERGE_END = DT_END + 2 * D_MODEL

HEAD_PAIRS = HEADS_PER_GROUP // 2
QKV_TILES = 3 * HEAD_PAIRS
GROUP_LANES = D_INNER // SSM_GROUPS
HEADS_PER_SSM_GROUP = SSM_HEADS // SSM_GROUPS
PROJ_ROWS = 512
PERM_ROWS = 256
SSD_ROWS = 4 * CHUNK
OUT_ROWS = 512
OUT_SUBROWS = 256


def _sigmoid_t(v):
    return 0.5 * jnp.tanh(0.5 * v) + 0.5


def _split3(v):
    hi = v.astype(BF16)
    r1 = v - hi.astype(F32)
    mid = r1.astype(BF16)
    lo = (r1 - mid.astype(F32)).astype(BF16)
    return hi, mid, lo


def _dot(a, b):
    return jnp.dot(a, b, preferred_element_type=F32)


def _dot_nt(a, b):
    return lax.dot_general(a, b, (((1,), (1,)), ((), ())), preferred_element_type=F32)


def _dot_tn(a, b):
    return lax.dot_general(a, b, (((0,), (0,)), ((), ())), preferred_element_type=F32)


def _dot_exact_rhs(a_f32, b_bf16):
    hi, mid, lo = _split3(a_f32)
    return _dot(hi, b_bf16) + _dot(mid, b_bf16) + _dot(lo, b_bf16)


def _proj_kernel(*refs, permute, tiled_out, silu_widths):
    if permute:
        x_ref, perm_ref, w_ref, *o_refs = refs
    else:
        x_ref, w_ref, *o_refs = refs
    lhs = x_ref[0].astype(BF16)
    if permute:
        lhs = jnp.concatenate(
            [_dot(perm_ref[...], lhs[t * PERM_ROWS:(t + 1) * PERM_ROWS, :]).astype(BF16)
             for t in range(PROJ_ROWS // PERM_ROWS)], axis=0)
    res = _dot_nt(lhs, w_ref[...])
    if tiled_out:
        (o_ref,) = o_refs
        _, n_tiles, n_classes, n_sub, sub, _ = o_ref.shape
        sub_rows = n_classes * sub
        for c in range(n_tiles):
            for t in range(n_sub):
                tile = res[t * sub_rows:(t + 1) * sub_rows, c * LANES:(c + 1) * LANES]
                o_ref[0, c, :, t] = tile.reshape(n_classes, sub, LANES).astype(o_ref.dtype)
    else:
        off = 0
        for o_ref, n_silu in zip(o_refs, silu_widths):
            width = o_ref.shape[-1]
            if n_silu:
                o_ref[0, :, :n_silu] = _silu(res[:, off:off + n_silu]).astype(o_ref.dtype)
            if n_silu < width:
                o_ref[0, :, n_silu:] = res[:, off + n_silu:off + width].astype(o_ref.dtype)
            off += width


def _class_permutation(dilation):
    sub = PERM_ROWS // dilation
    dst = np.arange(PERM_ROWS)
    src = dilation * (dst % sub) + dst // sub
    perm = np.zeros((PERM_ROWS, PERM_ROWS), np.float32)
    perm[dst, src] = 1.0
    return jnp.asarray(perm, BF16)


def _project(x, w, *, name, dilation=1, tiled_out=False, widths=None, silu_widths=None,
             out_dtype=F32):
    b, s, d = x.shape
    n = w.shape[0]
    steps = s // PROJ_ROWS
    permute = dilation > 1
    in_specs = [pl.BlockSpec((1, PROJ_ROWS, d), lambda i, j: (i, j, 0))]
    args = [x]
    if permute:
        in_specs.append(pl.BlockSpec((PERM_ROWS, PERM_ROWS), lambda i, j: (0, 0)))
        args.append(_class_permutation(dilation))
    in_specs.append(pl.BlockSpec((n, d), lambda i, j: (0, 0)))
    args.append(w)
    if tiled_out:
        n_sub = PROJ_ROWS // PERM_ROWS if permute else 1
        sub = PROJ_ROWS // (n_sub * dilation)
        out_shape = jax.ShapeDtypeStruct((b, n // LANES, dilation, steps * n_sub, sub, LANES), out_dtype)
        o_spec = pl.BlockSpec((1, n // LANES, dilation, n_sub, sub, LANES),
                              lambda i, j: (i, 0, 0, j, 0, 0))
    else:
        assert not permute and sum(widths) == n
        out_shape = tuple(jax.ShapeDtypeStruct((b, s, wd), out_dtype) for wd in widths)
        o_spec = tuple(pl.BlockSpec((1, PROJ_ROWS, wd), lambda i, j: (i, j, 0)) for wd in widths)
    out = pl.pallas_call(
        functools.partial(_proj_kernel, permute=permute, tiled_out=tiled_out,
                          silu_widths=silu_widths or (0,) * len(widths or ())),
        grid=(b, steps),
        in_specs=in_specs,
        out_specs=o_spec,
        out_shape=out_shape,
        compiler_params=pltpu.CompilerParams(
            dimension_semantics=("parallel", "parallel"), vmem_limit_bytes=VMEM_LIMIT),
        name=name,
    )(*args)
    return out.reshape(b, n // LANES, s, LANES) if tiled_out else out


def _bucket_tiles():
    qi = np.arange(BLK)[:, None]
    kj = np.arange(2 * BLK)[None, :]
    delta = qi + BLK - kj
    max_exact = NUM_BUCKETS // 2
    out = []
    for window, dil in PATTERNS:
        valid = (delta >= 0) & (delta <= window // dil)
        dist = np.maximum(delta, 0) * dil
        d_f = np.maximum(dist, 1).astype(np.float32)
        large = max_exact + (np.log(d_f / max_exact) / np.float32(math.log(MAX_DISTANCE / max_exact))
                             * (NUM_BUCKETS - max_exact)).astype(np.int32)
        large = np.minimum(large, NUM_BUCKETS - 1)
        bucket = np.where(dist < max_exact, dist, large)
        out.append(np.where(valid, bucket, -1).astype(np.int32))
    return np.stack(out)


def _bias_kernel(tbl_ref, bucket_ref, o_ref):
    col = pl.program_id(0) * HEADS_PER_GROUP + pl.program_id(1)
    bucket = bucket_ref[0]
    acc = jnp.full(bucket.shape, -jnp.inf, F32)
    for bkt in range(NUM_BUCKETS):
        acc = jnp.where(bucket == bkt, LOG2E * tbl_ref[bkt, col], acc)
    o_ref[0, 0] = acc


def _bias_tiles(rel_bias):
    buckets = jnp.asarray(_bucket_tiles())
    return pl.pallas_call(
        _bias_kernel,
        grid=(N_GROUPS, HEADS_PER_GROUP),
        in_specs=[pl.BlockSpec(memory_space=pltpu.SMEM),
                  pl.BlockSpec((1, BLK, 2 * BLK), lambda g, h: (g, 0, 0))],
        out_specs=pl.BlockSpec((1, 1, BLK, 2 * BLK), lambda g, h: (g, h, 0, 0)),
        out_shape=jax.ShapeDtypeStruct((N_GROUPS, HEADS_PER_GROUP, BLK, 2 * BLK), F32),
        name="bias_tiles",
    )(rel_bias, buckets)


ATT_DEPTH = 3


def _head_stack(t):
    lane = lax.broadcasted_iota(jnp.int32, t.shape, 1)
    zero = jnp.zeros_like(t)
    return jnp.concatenate([jnp.where(lane < HEAD_DIM, t, zero),
                            jnp.where(lane >= HEAD_DIM, t, zero)], axis=0)


def _attn_kernel(q1, k1, v1, q2, k2, v2, q3, k3, v3, bias_ref, gate_ref, o_ref,
                 m_scr, l_scr, acc_scr, *, seq):
    left = lax.broadcasted_iota(jnp.int32, (BLK, LANES), 1) < HEAD_DIM
    ones_rows = jnp.ones((2 * BLK, LANES), BF16)
    r_idx = lax.broadcasted_iota(jnp.int32, (2 * BLK, LANES), 0)
    c_idx = lax.broadcasted_iota(jnp.int32, (2 * BLK, LANES), 1)
    ones_heads = jnp.where((r_idx < BLK) == (c_idx < HEAD_DIM), 1.0, 0.0).astype(BF16)

    jobs = []
    for g, (_, dil) in enumerate(PATTERNS):
        sub = seq // dil
        for r in range(dil):
            for n in range(sub // BLK):
                dst = pl.ds(n * BLK * dil + r, BLK, stride=dil) if dil > 1 else pl.ds(n * BLK, BLK)
                jobs.append((g, r * sub + n * BLK, n > 0, dst))
    group_refs = ((q1, k1, v1), (q2, k2, v2), (q3, k3, v3))

    def key_rows(ref, row0, has_prev):
        if has_prev:
            return ref[0, 0, row0 - BLK:row0 + BLK, :]
        return ref[0, 0, row0:row0 + BLK, :]

    def scores_of(job):
        g, row0, has_prev, _ = job
        q_ref, k_ref, _ = group_refs[g]
        q = q_ref[0, 0, row0:row0 + BLK, :]
        kk = key_rows(k_ref, row0, has_prev)
        if has_prev:
            return _dot_nt(_head_stack(q), kk)
        return _dot_nt(q, _head_stack(kk))

    def softmax_of(job, s):
        g, _, has_prev, _ = job
        es, ms = [], []
        for h in range(2):
            if has_prev:
                sh = s[h * BLK:(h + 1) * BLK, :] + bias_ref[g, h]
            else:
                sh = s[:, h * BLK:(h + 1) * BLK] + bias_ref[g, h, :, BLK:]
            mh = jnp.max(sh, axis=-1, keepdims=True)
            es.append(jnp.exp2(sh - mh).astype(BF16))
            ms.append(mh)
        return jnp.concatenate(es, axis=0 if has_prev else 1), jnp.where(left, ms[0], ms[1])

    def output_of(job, p, m_new):
        g, row0, has_prev, dst = job
        vv = key_rows(group_refs[g][2], row0, has_prev)
        if has_prev:
            pv = _dot(p, jnp.concatenate([vv, ones_rows], axis=1))
            acc = jnp.where(left, pv[:BLK, :LANES], pv[BLK:, :LANES])
            l_new = jnp.where(left, pv[:BLK, LANES:], pv[BLK:, LANES:])
        else:
            pv = _dot(p, jnp.concatenate([_head_stack(vv), ones_heads], axis=1))
            acc, l_new = pv[:, :LANES], pv[:, LANES:]
        if g > 0:
            m_scr[g - 1, dst, :] = m_new
            l_scr[g - 1, dst, :] = l_new
            acc_scr[g - 1, dst, :] = acc
            return
        ms = [m_new] + [m_scr[k, dst, :] for k in range(N_GROUPS - 1)]
        ls = [l_new] + [l_scr[k, dst, :] for k in range(N_GROUPS - 1)]
        accs = [acc] + [acc_scr[k, dst, :] for k in range(N_GROUPS - 1)]
        m_all = jnp.maximum(jnp.maximum(ms[0], ms[1]), ms[2])
        ws = [jnp.exp2(m - m_all) for m in ms]
        den = sum(w * l for w, l in zip(ws, ls))
        num = sum(w * a for w, a in zip(ws, accs))
        o_ref[0, 0, dst, :] = (num / den * gate_ref[0, dst, :]).astype(o_ref.dtype)

    jobs.sort(key=lambda job: -job[0])
    pending = {u: scores_of(jobs[u]) for u in range(ATT_DEPTH)}
    for u, job in enumerate(jobs):
        p, m_new = softmax_of(job, pending.pop(u))
        if u + ATT_DEPTH < len(jobs):
            pending[u + ATT_DEPTH] = scores_of(jobs[u + ATT_DEPTH])
        output_of(job, p, m_new)


def _attention(qkv, bias, gate_dt):
    b, _, s, _ = qkv[0].shape
    in_specs, args = [], []
    for arr in qkv:
        for part in range(3):
            in_specs.append(pl.BlockSpec((1, 1, s, LANES),
                                         lambda i, j, part=part: (i, part * HEAD_PAIRS + j, 0, 0)))
            args.append(arr)
    in_specs.append(pl.BlockSpec((N_GROUPS, 2, BLK, 2 * BLK), lambda i, j: (0, j, 0, 0)))
    in_specs.append(pl.BlockSpec((1, s, LANES), lambda i, j: (i, 0, j)))
    return pl.pallas_call(
        functools.partial(_attn_kernel, seq=s),
        grid=(b, HEAD_PAIRS),
        in_specs=in_specs,
        out_specs=pl.BlockSpec((1, 1, s, LANES), lambda i, j: (i, j, 0, 0)),
        out_shape=jax.ShapeDtypeStruct((b, HEAD_PAIRS, s, LANES), BF16),
        scratch_shapes=[pltpu.VMEM((N_GROUPS - 1, s, LANES), F32)] * 3,
        compiler_params=pltpu.CompilerParams(
            dimension_semantics=("parallel", "parallel"), vmem_limit_bytes=VMEM_LIMIT),
        name="dilated_attention",
    )(*args, bias, gate_dt)


TIME_STRIDE = 4
TIME_SPAN = TIME_STRIDE * SUBLANES
HALO = SUBLANES


def _strided_rows(ref, base):
    pieces = [ref[pl.ds(base + TIME_SPAN * grp + i, SUBLANES, stride=TIME_STRIDE), :]
              for grp in range(CHUNK // TIME_SPAN) for i in range(TIME_STRIDE)]
    return jnp.concatenate(pieces, axis=0)


def _strided_time_of(idx):
    q = idx // SUBLANES
    return TIME_SPAN * (q // TIME_STRIDE) + q % TIME_STRIDE + TIME_STRIDE * (idx % SUBLANES)


def _silu(v):
    h = 0.5 * v
    return h * jnp.tanh(h) + h


def _dot_split2(a_f32, b_bf16):
    hi = a_f32.astype(BF16)
    lo = (a_f32 - hi.astype(F32)).astype(BF16)
    return _dot(hi, b_bf16) + _dot(lo, b_bf16)


def _ssd_kernel(xbc_ref, z_ref, dt_ref, convw_ref, convb_ref, dtb_ref, alog_ref, dskip_ref,
                normw_ref, expand_ref, o_ref, state_scr, xpad_scr, y_scr):
    @pl.when(pl.program_id(1) == 0)
    def _():
        state_scr[...] = jnp.zeros_like(state_scr)
        xpad_scr[:, 0:HALO, :] = jnp.zeros((CONV_DIM // LANES, HALO, LANES), F32)

    t_row = _strided_time_of(lax.broadcasted_iota(jnp.int32, (CHUNK, CHUNK), 0))
    t_col = _strided_time_of(lax.broadcasted_iota(jnp.int32, (CHUNK, CHUNK), 1))
    causal = t_row >= t_col
    tril = jnp.where(causal, 1.0, 0.0).astype(BF16)
    neg_rate = -LOG2E * jnp.exp(alog_ref[...])

    def one_chunk(k, carry):
        _ssd_chunk(pl.multiple_of(k * CHUNK, CHUNK), causal, tril, neg_rate,
                   xbc_ref, z_ref, dt_ref, convw_ref, convb_ref, dtb_ref, dskip_ref, normw_ref,
                   expand_ref, o_ref, state_scr, xpad_scr, y_scr)
        return carry

    lax.fori_loop(0, xbc_ref.shape[1] // CHUNK, one_chunk, 0)


def _ssd_chunk(row0, causal, tril, neg_rate, xbc_ref, z_ref, dt_ref, convw_ref, convb_ref, dtb_ref,
               dskip_ref, normw_ref, expand_ref, o_ref, state_scr, xpad_scr, y_scr):
    x_slabs = D_INNER // LANES
    rows = pl.ds(row0, CHUNK)

    def conv_silu(j):
        lanes = slice(j * LANES, (j + 1) * LANES)
        xpad_scr[j, HALO:HALO + CHUNK, :] = xbc_ref[0, rows, lanes]
        conv = convb_ref[:, lanes]
        for k in range(CONV_WIDTH):
            tap = _strided_rows(xpad_scr.at[j], HALO - (CONV_WIDTH - 1) + k)
            conv = conv + convw_ref[k:k + 1, lanes] * tap
        xpad_scr[j, 0:HALO, :] = xbc_ref[0, pl.ds(row0 + CHUNK - HALO, HALO), lanes]
        return _silu(conv)

    raw = _strided_rows(dt_ref.at[0], row0) + dtb_ref[...]
    dt = jnp.maximum(raw, 0.0) + jnp.log1p(jnp.exp(-jnp.abs(raw)))
    a = dt * neg_rate
    hi, mid, lo = _split3(a)
    acs = _dot(tril, hi) + _dot(tril, mid) + _dot(tril, lo)
    src_t = (acs - jnp.log2(dt)).T
    acs_end = acs[CHUNK - 1:CHUNK, :]

    def two_pieces(v):
        hi2 = v.astype(BF16)
        return hi2, (v - hi2.astype(F32)).astype(BF16)

    decay_in_p = two_pieces(jnp.exp2(acs))
    w_out_p = two_pieces(dt * jnp.exp2(acs_end - acs))

    slabs_per_group = GROUP_LANES // LANES
    for g in range(SSM_GROUPS):
        gl = slice(g * GROUP_LANES, (g + 1) * GROUP_LANES)
        expand = expand_ref[:, gl]
        decay_in, w_out = (_dot(hi2, expand) + _dot(lo2, expand)
                           for hi2, lo2 in (decay_in_p, w_out_p))
        chunk_decay = decay_in[CHUNK - 1:CHUNK, :]
        xs = jnp.concatenate([conv_silu(g * slabs_per_group + j) for j in range(slabs_per_group)],
                             axis=1)
        bg32 = conv_silu(x_slabs + g)
        bg = bg32.astype(BF16)
        cg = conv_silu(x_slabs + SSM_GROUPS + g).astype(BF16)
        x_b = xs.astype(BF16)
        x_out = (xs * w_out).astype(BF16)
        cb = _dot_nt(cg, bg)
        prev = state_scr[g]
        y_off = _dot(cg, prev.astype(BF16)) * decay_in
        state_scr[g] = prev * chunk_decay + _dot(bg32.T.astype(BF16), x_out)
        for j in range(HEADS_PER_SSM_GROUP // 2):
            gs = []
            for hh in (g * HEADS_PER_SSM_GROUP + 2 * j, g * HEADS_PER_SSM_GROUP + 2 * j + 1):
                seg = acs[:, hh:hh + 1] - src_t[hh:hh + 1, :]
                lmat = jnp.exp2(jnp.where(causal, seg, -jnp.inf))
                gs.append((cb * lmat).astype(BF16))
            slab = g * slabs_per_group + j
            pair = slice(j * LANES, (j + 1) * LANES)
            y_diag = _dot(jnp.concatenate(gs, axis=1), _head_stack(x_b[:, pair]))
            y = (y_diag + y_off[:, pair]
                 + dskip_ref[:, slab * LANES:(slab + 1) * LANES] * xs[:, pair])
            for q in range(CHUNK // SUBLANES):
                start = TIME_SPAN * (q // TIME_STRIDE) + q % TIME_STRIDE
                y_scr[slab, pl.ds(start, SUBLANES, stride=TIME_STRIDE), :] = (
                    y[q * SUBLANES:(q + 1) * SUBLANES, :])

        yg = jnp.concatenate([y_scr[g * slabs_per_group + j] for j in range(slabs_per_group)], axis=1)
        u = yg * z_ref[0, rows, gl]
        ms = jnp.mean(jnp.square(u), axis=-1, keepdims=True)
        o_ref[0, rows, gl] = (u * lax.rsqrt(ms + RMS_EPS) * normw_ref[:, gl]).astype(o_ref.dtype)


def _ssd(xbc, z, gate_dt, conv_w, conv_b, dt_bias, a_log, d_skip, norm_w):
    b, s, _ = xbc.shape
    pad = LANES - SSM_HEADS
    dtb = jnp.pad(dt_bias, (0, pad)).reshape(1, LANES)
    alog = jnp.pad(a_log, (0, pad)).reshape(1, LANES)
    dskip = jnp.repeat(d_skip, D_INNER // SSM_HEADS).reshape(1, D_INNER)
    expand = np.zeros((LANES, D_INNER), np.float32)
    expand[np.arange(D_INNER) // (D_INNER // SSM_HEADS), np.arange(D_INNER)] = 1.0
    expand = jnp.asarray(expand, BF16)
    dt_tile = (GATT_END - V_END) // LANES

    def whole(shape):
        return pl.BlockSpec(shape, lambda i, j: (0,) * len(shape))

    return pl.pallas_call(
        _ssd_kernel,
        grid=(b, s // SSD_ROWS),
        in_specs=[pl.BlockSpec((1, SSD_ROWS, CONV_DIM), lambda i, j: (i, j, 0)),
                  pl.BlockSpec((1, SSD_ROWS, D_INNER), lambda i, j: (i, j, 0)),
                  pl.BlockSpec((1, SSD_ROWS, LANES), lambda i, j: (i, j, dt_tile)),
                  whole((CONV_WIDTH, CONV_DIM)), whole((1, CONV_DIM)),
                  whole((1, LANES)), whole((1, LANES)), whole((1, D_INNER)),
                  whole((1, D_INNER)), whole((LANES, D_INNER))],
        out_specs=pl.BlockSpec((1, SSD_ROWS, D_INNER), lambda i, j: (i, j, 0)),
        out_shape=jax.ShapeDtypeStruct((b, s, D_INNER), BF16),
        scratch_shapes=[pltpu.VMEM((SSM_GROUPS, D_STATE, GROUP_LANES), F32),
                        pltpu.VMEM((CONV_DIM // LANES, HALO + CHUNK, LANES), F32),
                        pltpu.VMEM((D_INNER // LANES, CHUNK, LANES), F32)],
        compiler_params=pltpu.CompilerParams(
            dimension_semantics=("parallel", "arbitrary"), vmem_limit_bytes=VMEM_LIMIT),
        name="ssd_mixer",
    )(xbc, z, gate_dt, conv_w, conv_b.reshape(1, CONV_DIM), dtb, alog, dskip,
      norm_w.reshape(1, D_INNER), expand)


def _out_kernel(att_ref, ssm_ref, gm_ref, gp_ref, p_ref, x_ref, wb_ref, wo_ref, wp_ref,
                bg_ref, lng_ref, lnb_ref, o_ref, *, alpha):
    subs = [slice(t * OUT_SUBROWS, (t + 1) * OUT_SUBROWS) for t in range(OUT_ROWS // OUT_SUBROWS)]
    branch = []
    for rs in subs:
        o_att = jnp.concatenate([att_ref[0, j, rs, :] for j in range(HEAD_PAIRS)], axis=1)
        branch.append((_dot(o_att, wb_ref[:ATT_OUT, :]), _dot(ssm_ref[0, rs, :], wb_ref[ATT_OUT:, :]),
                       _dot(p_ref[0, rs, :].astype(BF16), wp_ref[...])))
    mixes = []
    for rs, (y_a, y_b, _) in zip(subs, branch):
        merged = (_sigmoid_t(gm_ref[0, rs, :D_MODEL] + bg_ref[0:1, :]) * y_a
                  + _sigmoid_t(gm_ref[0, rs, D_MODEL:] + bg_ref[1:2, :]) * y_b)
        mixes.append(_dot(merged.astype(BF16), wo_ref[...]))
    for rs, (_, _, pp), mix in zip(subs, branch, mixes):
        ple = _sigmoid_t(gp_ref[0, rs, :] + bg_ref[2:3, :]) * pp
        h = alpha * x_ref[0, rs, :] + mix + ple
        mu = jnp.mean(h, axis=-1, keepdims=True)
        var = jnp.mean(jnp.square(h - mu), axis=-1, keepdims=True)
        o_ref[0, rs, :] = (h - mu) * lax.rsqrt(var + LN_EPS) * lng_ref[...] + lnb_ref[...]


def _output(o_att, y_ssm, g_merge, g_ple, p_i, x, w_branch, w_out, w_ple, b_gate, ln_g, ln_b, alpha):
    b, s, d = x.shape
    rows = OUT_ROWS

    def whole(shape):
        return pl.BlockSpec(shape, lambda i, j: (0,) * len(shape), pipeline_mode=pl.Buffered(1))

    def tile(n):
        return pl.BlockSpec((1, rows, n), lambda i, j: (i, j, 0))

    return pl.pallas_call(
        functools.partial(_out_kernel, alpha=alpha),
        grid=(b, s // rows),
        in_specs=[pl.BlockSpec((1, HEAD_PAIRS, rows, LANES), lambda i, j: (i, 0, j, 0)),
                  tile(D_INNER), tile(2 * D_MODEL), tile(D_MODEL), tile(PLE_DIM), tile(D_MODEL),
                  whole((ATT_OUT + D_INNER, D_MODEL)), whole((D_MODEL, D_MODEL)),
                  whole((PLE_DIM, D_MODEL)), whole((3, D_MODEL)), whole((1, D_MODEL)),
                  whole((1, D_MODEL))],
        out_specs=tile(D_MODEL),
        out_shape=jax.ShapeDtypeStruct((b, s, d), F32),
        compiler_params=pltpu.CompilerParams(
            dimension_semantics=("parallel", "parallel"), vmem_limit_bytes=VMEM_LIMIT),
        name="merge_out_ln",
    )(o_att, y_ssm, g_merge, g_ple, p_i, x, w_branch.astype(BF16), w_out.astype(BF16),
      w_ple.astype(BF16), b_gate, ln_g.reshape(1, d), ln_b.reshape(1, d))


def _layer(x, p_i, w_in, b_gate, conv_w, conv_b, dt_bias, a_log, d_skip, ssm_norm_w,
           w_branch, w_out, w_ple, ln_g, ln_b, bias, alpha):
    row_scale = jnp.ones((w_in.shape[1], 1), F32).at[:Q_END].set(LOG2E * HEAD_DIM ** -0.5)
    wt = (jnp.swapaxes(w_in, 0, 1) * row_scale).astype(BF16)
    group_w = HEADS_PER_GROUP * HEAD_DIM
    qkv = []
    for g, (_, dil) in enumerate(PATTERNS):
        wg = jnp.concatenate([wt[part + g * group_w:part + (g + 1) * group_w]
                              for part in (0, Q_END, K_END)], axis=0)
        qkv.append(_project(x, wg, dilation=dil, tiled_out=True, out_dtype=BF16, name=f"proj_qkv{g}"))
    w_gates = jnp.concatenate(
        [wt[V_END:GATT_END], jnp.pad(wt[XBC_END:DT_END], ((0, LANES - SSM_HEADS), (0, 0))),
         wt[GMERGE_END:]], axis=0)
    gate_dt, g_ple = _project(x, w_gates, widths=(ATT_OUT + LANES, D_MODEL),
                              silu_widths=(ATT_OUT, 0), name="proj_gates")
    w_z_gm = jnp.concatenate([wt[GATT_END:Z_END], wt[DT_END:GMERGE_END]], axis=0)
    z_gate, g_merge = _project(x, w_z_gm, widths=(D_INNER, 2 * D_MODEL),
                               silu_widths=(D_INNER, 0), name="proj_z_gmerge")
    (xbc,) = _project(x, wt[Z_END:XBC_END], widths=(CONV_DIM,), name="proj_xbc")

    o_att = _attention(qkv, bias, gate_dt)
    y_ssm = _ssd(xbc, z_gate, gate_dt, conv_w, conv_b, dt_bias, a_log, d_skip, ssm_norm_w)
    return _output(o_att, y_ssm, g_merge, g_ple, p_i, x, w_branch, w_out, w_ple, b_gate,
                   ln_g, ln_b, alpha)


def kernel(x, p, w_in, b_gate, conv_w, conv_b, dt_bias, a_log, d_skip, ssm_norm_w, w_branch, w_out,
           w_ple, ln_g, ln_b, rel_bias):
    depth = w_in.shape[0]
    alpha = (2.0 * depth) ** 0.25
    bias = _bias_tiles(rel_bias)
    for i in range(depth):
        x = _layer(x, p[i], w_in[i], b_gate[i], conv_w[i], conv_b[i], dt_bias[i], a_log[i],
                   d_skip[i], ssm_norm_w[i], w_branch[i], w_out[i], w_ple[i], ln_g[i], ln_b[i],
                   bias, alpha)
    return x
```

```python
import functools
import math

import numpy as np
import jax
import jax.numpy as jnp
from jax import lax
from jax.experimental import pallas as pl
from jax.experimental.pallas import tpu as pltpu

F32 = jnp.float32
BF16 = jnp.bfloat16

LANES = 128
SUBLANES = 8
V7X_VMEM_BYTES = 64 * 1024 * 1024
VMEM_LIMIT = (V7X_VMEM_BYTES * 3) // 4

D_MODEL = 1024
HEAD_DIM = 64
HEADS_PER_GROUP = 12
PATTERNS = ((128, 1), (512, 4), (2048, 16))
N_GROUPS = 3
ATT_HEADS = N_GROUPS * HEADS_PER_GROUP
ATT_QKV = ATT_HEADS * HEAD_DIM
ATT_OUT = HEADS_PER_GROUP * HEAD_DIM
BLK = 128
NUM_BUCKETS = 32
MAX_DISTANCE = 2048
D_INNER = 2048
SSM_HEADS = 32
SSM_GROUPS = 4
D_STATE = 128
CONV_WIDTH = 4
CONV_DIM = D_INNER + 2 * SSM_GROUPS * D_STATE
CHUNK = 128
PLE_DIM = 256
LN_EPS = 1e-5
RMS_EPS = 1e-5
LOG2E = math.log2(math.e)
Q_END = ATT_QKV
K_END = Q_END + ATT_QKV
V_END = K_END + ATT_QKV
GATT_END = V_END + ATT_OUT
Z_END = GATT_END + D_INNER
XBC_END = Z_END + CONV_DIM
DT_END = XBC_END + SSM_HEADS
GMERGE_END = DT_END + 2 * D_MODEL

HEAD_PAIRS = HEADS_PER_GROUP // 2
QKV_TILES = 3 * HEAD_PAIRS
GROUP_LANES = D_INNER // SSM_GROUPS
HEADS_PER_SSM_GROUP = SSM_HEADS // SSM_GROUPS
PROJ_ROWS = 512
QKV_ROWS = 1024
PERM_ROWS = 256
SSD_ROWS = 4 * CHUNK
OUT_ROWS = 512
OUT_SUBROWS = 256


def _sigmoid_t(v):
    return 0.5 * jnp.tanh(0.5 * v) + 0.5


def _split3(v):
    hi = v.astype(BF16)
    r1 = v - hi.astype(F32)
    mid = r1.astype(BF16)
    lo = (r1 - mid.astype(F32)).astype(BF16)
    return hi, mid, lo


def _dot(a, b):
    return jnp.dot(a, b, preferred_element_type=F32)


def _dot_nt(a, b):
    return lax.dot_general(a, b, (((1,), (1,)), ((), ())), preferred_element_type=F32)


def _dot_tn(a, b):
    return lax.dot_general(a, b, (((0,), (0,)), ((), ())), preferred_element_type=F32)


def _dot_exact_rhs(a_f32, b_bf16):
    hi, mid, lo = _split3(a_f32)
    return _dot(hi, b_bf16) + _dot(mid, b_bf16) + _dot(lo, b_bf16)


def _proj_kernel(*refs, permute, tiled_out, silu_widths):
    if permute:
        x_ref, perm_ref, w_ref, *o_refs = refs
    else:
        x_ref, w_ref, *o_refs = refs
    lhs = x_ref[0].astype(BF16)
    if permute:
        lhs = jnp.concatenate(
            [_dot(perm_ref[...], lhs[t * PERM_ROWS:(t + 1) * PERM_ROWS, :]).astype(BF16)
             for t in range(lhs.shape[0] // PERM_ROWS)], axis=0)
    res = _dot_nt(lhs, w_ref[...])
    if tiled_out:
        (o_ref,) = o_refs
        _, n_tiles, n_classes, n_sub, sub, _ = o_ref.shape
        sub_rows = n_classes * sub
        for c in range(n_tiles):
            for t in range(n_sub):
                tile = res[t * sub_rows:(t + 1) * sub_rows, c * LANES:(c + 1) * LANES]
                o_ref[0, c, :, t] = tile.reshape(n_classes, sub, LANES).astype(o_ref.dtype)
    else:
        off = 0
        for o_ref, n_silu in zip(o_refs, silu_widths):
            width = o_ref.shape[-1]
            if n_silu:
                o_ref[0, :, :n_silu] = _silu(res[:, off:off + n_silu]).astype(o_ref.dtype)
            if n_silu < width:
                o_ref[0, :, n_silu:] = res[:, off + n_silu:off + width].astype(o_ref.dtype)
            off += width


def _class_permutation(dilation):
    sub = PERM_ROWS // dilation
    dst = np.arange(PERM_ROWS)
    src = dilation * (dst % sub) + dst // sub
    perm = np.zeros((PERM_ROWS, PERM_ROWS), np.float32)
    perm[dst, src] = 1.0
    return jnp.asarray(perm, BF16)


def _project(x, w, *, name, dilation=1, tiled_out=False, widths=None, silu_widths=None,
             out_dtype=F32, rows=PROJ_ROWS):
    b, s, d = x.shape
    n = w.shape[0]
    steps = s // rows
    permute = dilation > 1
    in_specs = [pl.BlockSpec((1, rows, d), lambda i, j: (i, j, 0))]
    args = [x]
    if permute:
        in_specs.append(pl.BlockSpec((PERM_ROWS, PERM_ROWS), lambda i, j: (0, 0)))
        args.append(_class_permutation(dilation))
    in_specs.append(pl.BlockSpec((n, d), lambda i, j: (0, 0)))
    args.append(w)
    if tiled_out:
        n_sub = rows // PERM_ROWS if permute else 1
        sub = rows // (n_sub * dilation)
        out_shape = jax.ShapeDtypeStruct((b, n // LANES, dilation, steps * n_sub, sub, LANES), out_dtype)
        o_spec = pl.BlockSpec((1, n // LANES, dilation, n_sub, sub, LANES),
                              lambda i, j: (i, 0, 0, j, 0, 0))
    else:
        assert not permute and sum(widths) == n
        out_shape = tuple(jax.ShapeDtypeStruct((b, s, wd), out_dtype) for wd in widths)
        o_spec = tuple(pl.BlockSpec((1, rows, wd), lambda i, j: (i, j, 0)) for wd in widths)
    out = pl.pallas_call(
        functools.partial(_proj_kernel, permute=permute, tiled_out=tiled_out,
                          silu_widths=silu_widths or (0,) * len(widths or ())),
        grid=(b, steps),
        in_specs=in_specs,
        out_specs=o_spec,
        out_shape=out_shape,
        compiler_params=pltpu.CompilerParams(
            dimension_semantics=("parallel", "parallel"), vmem_limit_bytes=VMEM_LIMIT),
        name=name,
    )(*args)
    return out.reshape(b, n // LANES, s, LANES) if tiled_out else out


def _bucket_tiles():
    qi = np.arange(BLK)[:, None]
    kj = np.arange(2 * BLK)[None, :]
    delta = qi + BLK - kj
    max_exact = NUM_BUCKETS // 2
    out = []
    for window, dil in PATTERNS:
        valid = (delta >= 0) & (delta <= window // dil)
        dist = np.maximum(delta, 0) * dil
        d_f = np.maximum(dist, 1).astype(np.float32)
        large = max_exact + (np.log(d_f / max_exact) / np.float32(math.log(MAX_DISTANCE / max_exact))
                             * (NUM_BUCKETS - max_exact)).astype(np.int32)
        large = np.minimum(large, NUM_BUCKETS - 1)
        bucket = np.where(dist < max_exact, dist, large)
        out.append(np.where(valid, bucket, -1).astype(np.int32))
    return np.stack(out)


def _bias_kernel(tbl_ref, bucket_ref, o_ref):
    col = pl.program_id(0) * HEADS_PER_GROUP + pl.program_id(1)
    bucket = bucket_ref[0]
    acc = jnp.full(bucket.shape, -jnp.inf, F32)
    for bkt in range(NUM_BUCKETS):
        acc = jnp.where(bucket == bkt, LOG2E * tbl_ref[bkt, col], acc)
    o_ref[0, 0] = acc


def _bias_tiles(rel_bias):
    buckets = jnp.asarray(_bucket_tiles())
    return pl.pallas_call(
        _bias_kernel,
        grid=(N_GROUPS, HEADS_PER_GROUP),
        in_specs=[pl.BlockSpec(memory_space=pltpu.SMEM),
                  pl.BlockSpec((1, BLK, 2 * BLK), lambda g, h: (g, 0, 0))],
        out_specs=pl.BlockSpec((1, 1, BLK, 2 * BLK), lambda g, h: (g, h, 0, 0)),
        out_shape=jax.ShapeDtypeStruct((N_GROUPS, HEADS_PER_GROUP, BLK, 2 * BLK), F32),
        name="bias_tiles",
    )(rel_bias, buckets)


ATT_PAIRS = 2
ATT_DEPTH = 3


def _head_stack(t):
    lane = lax.broadcasted_iota(jnp.int32, t.shape, 1)
    zero = jnp.zeros_like(t)
    return jnp.concatenate([jnp.where(lane < HEAD_DIM, t, zero),
                            jnp.where(lane >= HEAD_DIM, t, zero)], axis=0)


def _attn_kernel(q1, k1, v1, q2, k2, v2, q3, k3, v3, bias_ref, gate_ref, o_ref,
                 m_scr, l_scr, acc_scr, *, seq):
    left = lax.broadcasted_iota(jnp.int32, (BLK, LANES), 1) < HEAD_DIM
    ones_rows = jnp.ones((2 * BLK, LANES), BF16)
    r_idx = lax.broadcasted_iota(jnp.int32, (2 * BLK, LANES), 0)
    c_idx = lax.broadcasted_iota(jnp.int32, (2 * BLK, LANES), 1)
    ones_heads = jnp.where((r_idx < BLK) == (c_idx < HEAD_DIM), 1.0, 0.0).astype(BF16)

    jobs = []
    for pp in range(ATT_PAIRS):
        for g, (_, dil) in reversed(list(enumerate(PATTERNS))):
            sub = seq // dil
            for r in range(dil):
                for n in range(sub // BLK):
                    dst = pl.ds(n * BLK * dil + r, BLK, stride=dil) if dil > 1 else pl.ds(n * BLK, BLK)
                    jobs.append((pp, g, r * sub + n * BLK, n > 0, dst))
    group_refs = ((q1, k1, v1), (q2, k2, v2), (q3, k3, v3))

    def key_rows(ref, pp, row0, has_prev):
        if has_prev:
            return ref[0, pp, row0 - BLK:row0 + BLK, :]
        return ref[0, pp, row0:row0 + BLK, :]

    def scores_of(job):
        pp, g, row0, has_prev, _ = job
        q_ref, k_ref, _ = group_refs[g]
        q = q_ref[0, pp, row0:row0 + BLK, :]
        kk = key_rows(k_ref, pp, row0, has_prev)
        if has_prev:
            return _dot_nt(_head_stack(q), kk)
        return _dot_nt(q, _head_stack(kk))

    def softmax_of(job, s):
        pp, g, _, has_prev, _ = job
        es, ms = [], []
        for h in range(2):
            if has_prev:
                sh = s[h * BLK:(h + 1) * BLK, :] + bias_ref[g, 2 * pp + h]
            else:
                sh = s[:, h * BLK:(h + 1) * BLK] + bias_ref[g, 2 * pp + h, :, BLK:]
            mh = jnp.max(sh, axis=-1, keepdims=True)
            es.append(jnp.exp2(sh - mh).astype(BF16))
            ms.append(mh)
        return jnp.concatenate(es, axis=0 if has_prev else 1), jnp.where(left, ms[0], ms[1])

    def output_of(job, p, m_new):
        pp, g, row0, has_prev, dst = job
        vv = key_rows(group_refs[g][2], pp, row0, has_prev)
        if has_prev:
            pv = _dot(p, jnp.concatenate([vv, ones_rows], axis=1))
            acc = jnp.where(left, pv[:BLK, :LANES], pv[BLK:, :LANES])
            l_new = jnp.where(left, pv[:BLK, LANES:], pv[BLK:, LANES:])
        else:
            pv = _dot(p, jnp.concatenate([_head_stack(vv), ones_heads], axis=1))
            acc, l_new = pv[:, :LANES], pv[:, LANES:]
        if g > 0:
            m_scr[g - 1, pp, dst, :] = m_new
            l_scr[g - 1, pp, dst, :] = l_new
            acc_scr[g - 1, pp, dst, :] = acc
            return
        ms = [m_new] + [m_scr[k, pp, dst, :] for k in range(N_GROUPS - 1)]
        ls = [l_new] + [l_scr[k, pp, dst, :] for k in range(N_GROUPS - 1)]
        accs = [acc] + [acc_scr[k, pp, dst, :] for k in range(N_GROUPS - 1)]
        m_all = jnp.maximum(jnp.maximum(ms[0], ms[1]), ms[2])
        ws = [jnp.exp2(m - m_all) for m in ms]
        den = sum(w * l for w, l in zip(ws, ls))
        num = sum(w * a for w, a in zip(ws, accs))
        gate = gate_ref[0, dst, pp * LANES:(pp + 1) * LANES]
        o_ref[0, pp, dst, :] = (num / den * gate).astype(o_ref.dtype)

    pending = {u: scores_of(jobs[u]) for u in range(ATT_DEPTH)}
    for u, job in enumerate(jobs):
        p, m_new = softmax_of(job, pending.pop(u))
        if u + ATT_DEPTH < len(jobs):
            pending[u + ATT_DEPTH] = scores_of(jobs[u + ATT_DEPTH])
        output_of(job, p, m_new)


def _attention(qkv, bias, gate_dt):
    b, _, s, _ = qkv[0].shape
    steps = HEAD_PAIRS // ATT_PAIRS
    in_specs, args = [], []
    for arr in qkv:
        for part in range(3):
            in_specs.append(pl.BlockSpec((1, ATT_PAIRS, s, LANES),
                                         lambda i, j, part=part: (i, part * steps + j, 0, 0)))
            args.append(arr)
    in_specs.append(pl.BlockSpec((N_GROUPS, 2 * ATT_PAIRS, BLK, 2 * BLK), lambda i, j: (0, j, 0, 0)))
    in_specs.append(pl.BlockSpec((1, s, ATT_PAIRS * LANES), lambda i, j: (i, 0, j)))
    return pl.pallas_call(
        functools.partial(_attn_kernel, seq=s),
        grid=(b, steps),
        in_specs=in_specs,
        out_specs=pl.BlockSpec((1, ATT_PAIRS, s, LANES), lambda i, j: (i, j, 0, 0)),
        out_shape=jax.ShapeDtypeStruct((b, HEAD_PAIRS, s, LANES), BF16),
        scratch_shapes=[pltpu.VMEM((N_GROUPS - 1, ATT_PAIRS, s, LANES), F32)] * 3,
        compiler_params=pltpu.CompilerParams(
            dimension_semantics=("parallel", "parallel"), vmem_limit_bytes=VMEM_LIMIT),
        name="dilated_attention",
    )(*args, bias, gate_dt)


TIME_STRIDE = 4
TIME_SPAN = TIME_STRIDE * SUBLANES
HALO = SUBLANES


def _strided_rows(ref, base):
    pieces = [ref[pl.ds(base + TIME_SPAN * grp + i, SUBLANES, stride=TIME_STRIDE), :]
              for grp in range(CHUNK // TIME_SPAN) for i in range(TIME_STRIDE)]
    return jnp.concatenate(pieces, axis=0)


def _strided_time_of(idx):
    q = idx // SUBLANES
    return TIME_SPAN * (q // TIME_STRIDE) + q % TIME_STRIDE + TIME_STRIDE * (idx % SUBLANES)


def _silu(v):
    h = 0.5 * v
    return h * jnp.tanh(h) + h


def _dot_split2(a_f32, b_bf16):
    hi = a_f32.astype(BF16)
    lo = (a_f32 - hi.astype(F32)).astype(BF16)
    return _dot(hi, b_bf16) + _dot(lo, b_bf16)


def _ssd_kernel(xbc_ref, z_ref, dt_ref, convw_ref, convb_ref, dtb_ref, alog_ref, dskip_ref,
                normw_ref, expand_ref, o_ref, state_scr, xpad_scr, y_scr):
    @pl.when(pl.program_id(1) == 0)
    def _():
        state_scr[...] = jnp.zeros_like(state_scr)
        xpad_scr[:, 0:HALO, :] = jnp.zeros((CONV_DIM // LANES, HALO, LANES), F32)

    t_row = _strided_time_of(lax.broadcasted_iota(jnp.int32, (CHUNK, CHUNK), 0))
    t_col = _strided_time_of(lax.broadcasted_iota(jnp.int32, (CHUNK, CHUNK), 1))
    causal = t_row >= t_col
    tril = jnp.where(causal, 1.0, 0.0).astype(BF16)
    neg_rate = -LOG2E * jnp.exp(alog_ref[...])

    def one_chunk(k, carry):
        _ssd_chunk(pl.multiple_of(k * CHUNK, CHUNK), causal, tril, neg_rate,
                   xbc_ref, z_ref, dt_ref, convw_ref, convb_ref, dtb_ref, dskip_ref, normw_ref,
                   expand_ref, o_ref, state_scr, xpad_scr, y_scr)
        return carry

    lax.fori_loop(0, xbc_ref.shape[1] // CHUNK, one_chunk, 0)


def _ssd_chunk(row0, causal, tril, neg_rate, xbc_ref, z_ref, dt_ref, convw_ref, convb_ref, dtb_ref,
               dskip_ref, normw_ref, expand_ref, o_ref, state_scr, xpad_scr, y_scr):
    x_slabs = D_INNER // LANES
    rows = pl.ds(row0, CHUNK)

    def conv_silu(j):
        lanes = slice(j * LANES, (j + 1) * LANES)
        xpad_scr[j, HALO:HALO + CHUNK, :] = xbc_ref[0, rows, lanes]
        conv = convb_ref[:, lanes]
        for k in range(CONV_WIDTH):
            tap = _strided_rows(xpad_scr.at[j], HALO - (CONV_WIDTH - 1) + k)
            conv = conv + convw_ref[k:k + 1, lanes] * tap
        xpad_scr[j, 0:HALO, :] = xbc_ref[0, pl.ds(row0 + CHUNK - HALO, HALO), lanes]
        return _silu(conv)

    raw = _strided_rows(dt_ref.at[0], row0) + dtb_ref[...]
    dt = jnp.maximum(raw, 0.0) + jnp.log1p(jnp.exp(-jnp.abs(raw)))
    a = dt * neg_rate
    hi, mid, lo = _split3(a)
    acs = _dot(tril, hi) + _dot(tril, mid) + _dot(tril, lo)
    src_t = (acs - jnp.log2(dt)).T
    acs_end = acs[CHUNK - 1:CHUNK, :]

    def two_pieces(v):
        hi2 = v.astype(BF16)
        return hi2, (v - hi2.astype(F32)).astype(BF16)

    decay_in_p = two_pieces(jnp.exp2(acs))
    w_out_p = two_pieces(dt * jnp.exp2(acs_end - acs))

    slabs_per_group = GROUP_LANES // LANES
    for g in range(SSM_GROUPS):
        gl = slice(g * GROUP_LANES, (g + 1) * GROUP_LANES)
        expand = expand_ref[:, gl]
        decay_in, w_out = (_dot(hi2, expand) + _dot(lo2, expand)
                           for hi2, lo2 in (decay_in_p, w_out_p))
        chunk_decay = decay_in[CHUNK - 1:CHUNK, :]
        xs = jnp.concatenate([conv_silu(g * slabs_per_group + j) for j in range(slabs_per_group)],
                             axis=1)
        bg32 = conv_silu(x_slabs + g)
        bg = bg32.astype(BF16)
        cg = conv_silu(x_slabs + SSM_GROUPS + g).astype(BF16)
        x_b = xs.astype(BF16)
        x_out = (xs * w_out).astype(BF16)
        cb = _dot_nt(cg, bg)
        prev = state_scr[g]
        y_off = _dot(cg, prev.astype(BF16)) * decay_in
        state_scr[g] = prev * chunk_decay + _dot(bg32.T.astype(BF16), x_out)
        for j in range(HEADS_PER_SSM_GROUP // 2):
            gs = []
            for hh in (g * HEADS_PER_SSM_GROUP + 2 * j, g * HEADS_PER_SSM_GROUP + 2 * j + 1):
                seg = acs[:, hh:hh + 1] - src_t[hh:hh + 1, :]
                lmat = jnp.exp2(jnp.where(causal, seg, -jnp.inf))
                gs.append((cb * lmat).astype(BF16))
            slab = g * slabs_per_group + j
            pair = slice(j * LANES, (j + 1) * LANES)
            y_diag = _dot(jnp.concatenate(gs, axis=1), _head_stack(x_b[:, pair]))
            y = (y_diag + y_off[:, pair]
                 + dskip_ref[:, slab * LANES:(slab + 1) * LANES] * xs[:, pair])
            for q in range(CHUNK // SUBLANES):
                start = TIME_SPAN * (q // TIME_STRIDE) + q % TIME_STRIDE
                y_scr[slab, pl.ds(start, SUBLANES, stride=TIME_STRIDE), :] = (
                    y[q * SUBLANES:(q + 1) * SUBLANES, :])

        yg = jnp.concatenate([y_scr[g * slabs_per_group + j] for j in range(slabs_per_group)], axis=1)
        u = yg * z_ref[0, rows, gl]
        ms = jnp.mean(jnp.square(u), axis=-1, keepdims=True)
        o_ref[0, rows, gl] = (u * lax.rsqrt(ms + RMS_EPS) * normw_ref[:, gl]).astype(o_ref.dtype)


def _ssd(xbc, z, gate_dt, conv_w, conv_b, dt_bias, a_log, d_skip, norm_w):
    b, s, _ = xbc.shape
    pad = LANES - SSM_HEADS
    dtb = jnp.pad(dt_bias, (0, pad)).reshape(1, LANES)
    alog = jnp.pad(a_log, (0, pad)).reshape(1, LANES)
    dskip = jnp.repeat(d_skip, D_INNER // SSM_HEADS).reshape(1, D_INNER)
    expand = np.zeros((LANES, D_INNER), np.float32)
    expand[np.arange(D_INNER) // (D_INNER // SSM_HEADS), np.arange(D_INNER)] = 1.0
    expand = jnp.asarray(expand, BF16)
    dt_tile = (GATT_END - V_END) // LANES

    def whole(shape):
        return pl.BlockSpec(shape, lambda i, j: (0,) * len(shape))

    return pl.pallas_call(
        _ssd_kernel,
        grid=(b, s // SSD_ROWS),
        in_specs=[pl.BlockSpec((1, SSD_ROWS, CONV_DIM), lambda i, j: (i, j, 0)),
                  pl.BlockSpec((1, SSD_ROWS, D_INNER), lambda i, j: (i, j, 0)),
                  pl.BlockSpec((1, SSD_ROWS, LANES), lambda i, j: (i, j, dt_tile)),
                  whole((CONV_WIDTH, CONV_DIM)), whole((1, CONV_DIM)),
                  whole((1, LANES)), whole((1, LANES)), whole((1, D_INNER)),
                  whole((1, D_INNER)), whole((LANES, D_INNER))],
        out_specs=pl.BlockSpec((1, SSD_ROWS, D_INNER), lambda i, j: (i, j, 0)),
        out_shape=jax.ShapeDtypeStruct((b, s, D_INNER), BF16),
        scratch_shapes=[pltpu.VMEM((SSM_GROUPS, D_STATE, GROUP_LANES), F32),
                        pltpu.VMEM((CONV_DIM // LANES, HALO + CHUNK, LANES), F32),
                        pltpu.VMEM((D_INNER // LANES, CHUNK, LANES), F32)],
        compiler_params=pltpu.CompilerParams(
            dimension_semantics=("parallel", "arbitrary"), vmem_limit_bytes=VMEM_LIMIT),
        name="ssd_mixer",
    )(xbc, z, gate_dt, conv_w, conv_b.reshape(1, CONV_DIM), dtb, alog, dskip,
      norm_w.reshape(1, D_INNER), expand)


def _out_kernel(att_ref, ssm_ref, gm_ref, gp_ref, p_ref, x_ref, wb_ref, wo_ref, wp_ref,
                bg_ref, lng_ref, lnb_ref, o_ref, *, alpha):
    subs = [slice(t * OUT_SUBROWS, (t + 1) * OUT_SUBROWS) for t in range(OUT_ROWS // OUT_SUBROWS)]
    branch = []
    for rs in subs:
        o_att = jnp.concatenate([att_ref[0, j, rs, :] for j in range(HEAD_PAIRS)], axis=1)
        branch.append((_dot(o_att, wb_ref[:ATT_OUT, :]), _dot(ssm_ref[0, rs, :], wb_ref[ATT_OUT:, :]),
                       _dot(p_ref[0, rs, :].astype(BF16), wp_ref[...])))
    mixes = []
    for rs, (y_a, y_b, _) in zip(subs, branch):
        merged = (_sigmoid_t(gm_ref[0, rs, :D_MODEL] + bg_ref[0:1, :]) * y_a
                  + _sigmoid_t(gm_ref[0, rs, D_MODEL:] + bg_ref[1:2, :]) * y_b)
        mixes.append(_dot(merged.astype(BF16), wo_ref[...]))
    for rs, (_, _, pp), mix in zip(subs, branch, mixes):
        ple = _sigmoid_t(gp_ref[0, rs, :] + bg_ref[2:3, :]) * pp
        h = alpha * x_ref[0, rs, :] + mix + ple
        mu = jnp.mean(h, axis=-1, keepdims=True)
        var = jnp.mean(jnp.square(h - mu), axis=-1, keepdims=True)
        o_ref[0, rs, :] = (h - mu) * lax.rsqrt(var + LN_EPS) * lng_ref[...] + lnb_ref[...]


def _output(o_att, y_ssm, g_merge, g_ple, p_i, x, w_branch, w_out, w_ple, b_gate, ln_g, ln_b, alpha):
    b, s, d = x.shape
    rows = OUT_ROWS

    def whole(shape):
        return pl.BlockSpec(shape, lambda i, j: (0,) * len(shape), pipeline_mode=pl.Buffered(1))

    def tile(n):
        return pl.BlockSpec((1, rows, n), lambda i, j: (i, j, 0))

    return pl.pallas_call(
        functools.partial(_out_kernel, alpha=alpha),
        grid=(b, s // rows),
        in_specs=[pl.BlockSpec((1, HEAD_PAIRS, rows, LANES), lambda i, j: (i, 0, j, 0)),
                  tile(D_INNER), tile(2 * D_MODEL), tile(D_MODEL), tile(PLE_DIM), tile(D_MODEL),
                  whole((ATT_OUT + D_INNER, D_MODEL)), whole((D_MODEL, D_MODEL)),
                  whole((PLE_DIM, D_MODEL)), whole((3, D_MODEL)), whole((1, D_MODEL)),
                  whole((1, D_MODEL))],
        out_specs=tile(D_MODEL),
        out_shape=jax.ShapeDtypeStruct((b, s, d), F32),
        compiler_params=pltpu.CompilerParams(
            dimension_semantics=("parallel", "parallel"), vmem_limit_bytes=VMEM_LIMIT),
        name="merge_out_ln",
    )(o_att, y_ssm, g_merge, g_ple, p_i, x, w_branch.astype(BF16), w_out.astype(BF16),
      w_ple.astype(BF16), b_gate, ln_g.reshape(1, d), ln_b.reshape(1, d))


def _layer(x, p_i, w_in, b_gate, conv_w, conv_b, dt_bias, a_log, d_skip, ssm_norm_w,
           w_branch, w_out, w_ple, ln_g, ln_b, bias, alpha):
    row_scale = jnp.ones((w_in.shape[1], 1), F32).at[:Q_END].set(LOG2E * HEAD_DIM ** -0.5)
    wt = (jnp.swapaxes(w_in, 0, 1) * row_scale).astype(BF16)
    group_w = HEADS_PER_GROUP * HEAD_DIM
    qkv = []
    for g, (_, dil) in enumerate(PATTERNS):
        wg = jnp.concatenate([wt[part + g * group_w:part + (g + 1) * group_w]
                              for part in (0, Q_END, K_END)], axis=0)
        qkv.append(_project(x, wg, dilation=dil, tiled_out=True, out_dtype=BF16, rows=QKV_ROWS,
                            name=f"proj_qkv{g}"))
    w_gates = jnp.concatenate(
        [wt[V_END:GATT_END], jnp.pad(wt[XBC_END:DT_END], ((0, LANES - SSM_HEADS), (0, 0))),
         wt[GMERGE_END:]], axis=0)
    gate_dt, g_ple = _project(x, w_gates, widths=(ATT_OUT + LANES, D_MODEL),
                              silu_widths=(ATT_OUT, 0), name="proj_gates")
    w_z_gm = jnp.concatenate([wt[GATT_END:Z_END], wt[DT_END:GMERGE_END]], axis=0)
    z_gate, g_merge = _project(x, w_z_gm, widths=(D_INNER, 2 * D_MODEL),
                               silu_widths=(D_INNER, 0), name="proj_z_gmerge")
    (xbc,) = _project(x, wt[Z_END:XBC_END], widths=(CONV_DIM,), name="proj_xbc")

    o_att = _attention(qkv, bias, gate_dt)
    y_ssm = _ssd(xbc, z_gate, gate_dt, conv_w, conv_b, dt_bias, a_log, d_skip, ssm_norm_w)
    return _output(o_att, y_ssm, g_merge, g_ple, p_i, x, w_branch, w_out, w_ple, b_gate,
                   ln_g, ln_b, alpha)


def kernel(x, p, w_in, b_gate, conv_w, conv_b, dt_bias, a_log, d_skip, ssm_norm_w, w_branch, w_out,
           w_ple, ln_g, ln_b, rel_bias):
    depth = w_in.shape[0]
    alpha = (2.0 * depth) ** 0.25
    bias = _bias_tiles(rel_bias)
    for i in range(depth):
        x = _layer(x, p[i], w_in[i], b_gate[i], conv_w[i], conv_b[i], dt_bias[i], a_log[i],
                   d_skip[i], ssm_norm_w[i], w_branch[i], w_out[i], w_ple[i], ln_g[i], ln_b[i],
                   bias, alpha)
    return x
```

```python
import functools
import math

import numpy as np
import jax
import jax.numpy as jnp
from jax import lax
from jax.experimental import pallas as pl
from jax.experimental.pallas import tpu as pltpu

F32 = jnp.float32
BF16 = jnp.bfloat16

LANES = 128
SUBLANES = 8
V7X_VMEM_BYTES = 64 * 1024 * 1024
VMEM_LIMIT = (V7X_VMEM_BYTES * 3) // 4

D_MODEL = 1024
HEAD_DIM = 64
HEADS_PER_GROUP = 12
PATTERNS = ((128, 1), (512, 4), (2048, 16))
N_GROUPS = 3
ATT_HEADS = N_GROUPS * HEADS_PER_GROUP
ATT_QKV = ATT_HEADS * HEAD_DIM
ATT_OUT = HEADS_PER_GROUP * HEAD_DIM
BLK = 128
NUM_BUCKETS = 32
MAX_DISTANCE = 2048
D_INNER = 2048
SSM_HEADS = 32
SSM_GROUPS = 4
D_STATE = 128
CONV_WIDTH = 4
CONV_DIM = D_INNER + 2 * SSM_GROUPS * D_STATE
CHUNK = 128
PLE_DIM = 256
LN_EPS = 1e-5
RMS_EPS = 1e-5
LOG2E = math.log2(math.e)
Q_END = ATT_QKV
K_END = Q_END + ATT_QKV
V_END = K_END + ATT_QKV
GATT_END = V_END + ATT_OUT
Z_END = GATT_END + D_INNER
XBC_END = Z_END + CONV_DIM
DT_END = XBC_END + SSM_HEADS
GMERGE_END = DT_END + 2 * D_MODEL

HEAD_PAIRS = HEADS_PER_GROUP // 2
QKV_TILES = 3 * HEAD_PAIRS
GROUP_LANES = D_INNER // SSM_GROUPS
HEADS_PER_SSM_GROUP = SSM_HEADS // SSM_GROUPS
PROJ_ROWS = 512
QKV_ROWS = 1024
PERM_ROWS = 256
SSD_ROWS = 2 * CHUNK
SSD_BATCH = 2
OUT_ROWS = 512
OUT_SUBROWS = 256


def _sigmoid_t(v):
    return 0.5 * jnp.tanh(0.5 * v) + 0.5


def _split3(v):
    hi = v.astype(BF16)
    r1 = v - hi.astype(F32)
    mid = r1.astype(BF16)
    lo = (r1 - mid.astype(F32)).astype(BF16)
    return hi, mid, lo


def _dot(a, b):
    return jnp.dot(a, b, preferred_element_type=F32)


def _dot_nt(a, b):
    return lax.dot_general(a, b, (((1,), (1,)), ((), ())), preferred_element_type=F32)


def _dot_tn(a, b):
    return lax.dot_general(a, b, (((0,), (0,)), ((), ())), preferred_element_type=F32)


def _dot_exact_rhs(a_f32, b_bf16):
    hi, mid, lo = _split3(a_f32)
    return _dot(hi, b_bf16) + _dot(mid, b_bf16) + _dot(lo, b_bf16)


def _proj_kernel(*refs, permute, tiled_out, silu_widths):
    if permute:
        x_ref, perm_ref, w_ref, *o_refs = refs
    else:
        x_ref, w_ref, *o_refs = refs
    lhs = x_ref[0].astype(BF16)
    if permute:
        lhs = jnp.concatenate(
            [_dot(perm_ref[...], lhs[t * PERM_ROWS:(t + 1) * PERM_ROWS, :]).astype(BF16)
             for t in range(lhs.shape[0] // PERM_ROWS)], axis=0)
    res = _dot_nt(lhs, w_ref[...])
    if tiled_out:
        (o_ref,) = o_refs
        _, n_tiles, n_classes, n_sub, sub, _ = o_ref.shape
        sub_rows = n_classes * sub
        for c in range(n_tiles):
            for t in range(n_sub):
                tile = res[t * sub_rows:(t + 1) * sub_rows, c * LANES:(c + 1) * LANES]
                o_ref[0, c, :, t] = tile.reshape(n_classes, sub, LANES).astype(o_ref.dtype)
    else:
        off = 0
        for o_ref, n_silu in zip(o_refs, silu_widths):
            width = o_ref.shape[-1]
            if n_silu:
                o_ref[0, :, :n_silu] = _silu(res[:, off:off + n_silu]).astype(o_ref.dtype)
            if n_silu < width:
                o_ref[0, :, n_silu:] = res[:, off + n_silu:off + width].astype(o_ref.dtype)
            off += width


def _class_permutation(dilation):
    sub = PERM_ROWS // dilation
    dst = np.arange(PERM_ROWS)
    src = dilation * (dst % sub) + dst // sub
    perm = np.zeros((PERM_ROWS, PERM_ROWS), np.float32)
    perm[dst, src] = 1.0
    return jnp.asarray(perm, BF16)


def _project(x, w, *, name, dilation=1, tiled_out=False, widths=None, silu_widths=None,
             out_dtype=F32, rows=PROJ_ROWS):
    b, s, d = x.shape
    n = w.shape[0]
    steps = s // rows
    permute = dilation > 1
    in_specs = [pl.BlockSpec((1, rows, d), lambda i, j: (i, j, 0))]
    args = [x]
    if permute:
        in_specs.append(pl.BlockSpec((PERM_ROWS, PERM_ROWS), lambda i, j: (0, 0)))
        args.append(_class_permutation(dilation))
    in_specs.append(pl.BlockSpec((n, d), lambda i, j: (0, 0)))
    args.append(w)
    if tiled_out:
        n_sub = rows // PERM_ROWS if permute else 1
        sub = rows // (n_sub * dilation)
        out_shape = jax.ShapeDtypeStruct((b, n // LANES, dilation, steps * n_sub, sub, LANES), out_dtype)
        o_spec = pl.BlockSpec((1, n // LANES, dilation, n_sub, sub, LANES),
                              lambda i, j: (i, 0, 0, j, 0, 0))
    else:
        assert not permute and sum(widths) == n
        out_shape = tuple(jax.ShapeDtypeStruct((b, s, wd), out_dtype) for wd in widths)
        o_spec = tuple(pl.BlockSpec((1, rows, wd), lambda i, j: (i, j, 0)) for wd in widths)
    out = pl.pallas_call(
        functools.partial(_proj_kernel, permute=permute, tiled_out=tiled_out,
                          silu_widths=silu_widths or (0,) * len(widths or ())),
        grid=(b, steps),
        in_specs=in_specs,
        out_specs=o_spec,
        out_shape=out_shape,
        compiler_params=pltpu.CompilerParams(
            dimension_semantics=("parallel", "parallel"), vmem_limit_bytes=VMEM_LIMIT),
        name=name,
    )(*args)
    return out.reshape(b, n // LANES, s, LANES) if tiled_out else out


def _bucket_tiles():
    qi = np.arange(BLK)[:, None]
    kj = np.arange(2 * BLK)[None, :]
    delta = qi + BLK - kj
    max_exact = NUM_BUCKETS // 2
    out = []
    for window, dil in PATTERNS:
        valid = (delta >= 0) & (delta <= window // dil)
        dist = np.maximum(delta, 0) * dil
        d_f = np.maximum(dist, 1).astype(np.float32)
        large = max_exact + (np.log(d_f / max_exact) / np.float32(math.log(MAX_DISTANCE / max_exact))
                             * (NUM_BUCKETS - max_exact)).astype(np.int32)
        large = np.minimum(large, NUM_BUCKETS - 1)
        bucket = np.where(dist < max_exact, dist, large)
        out.append(np.where(valid, bucket, -1).astype(np.int32))
    return np.stack(out)


def _bias_kernel(tbl_ref, bucket_ref, o_ref):
    col = pl.program_id(0) * HEADS_PER_GROUP + pl.program_id(1)
    bucket = bucket_ref[0]
    acc = jnp.full(bucket.shape, -jnp.inf, F32)
    for bkt in range(NUM_BUCKETS):
        acc = jnp.where(bucket == bkt, LOG2E * tbl_ref[bkt, col], acc)
    o_ref[0, 0] = acc


def _bias_tiles(rel_bias):
    buckets = jnp.asarray(_bucket_tiles())
    return pl.pallas_call(
        _bias_kernel,
        grid=(N_GROUPS, HEADS_PER_GROUP),
        in_specs=[pl.BlockSpec(memory_space=pltpu.SMEM),
                  pl.BlockSpec((1, BLK, 2 * BLK), lambda g, h: (g, 0, 0))],
        out_specs=pl.BlockSpec((1, 1, BLK, 2 * BLK), lambda g, h: (g, h, 0, 0)),
        out_shape=jax.ShapeDtypeStruct((N_GROUPS, HEADS_PER_GROUP, BLK, 2 * BLK), F32),
        name="bias_tiles",
    )(rel_bias, buckets)


ATT_PAIRS = 2
ATT_DEPTH = 3


def _head_stack(t):
    lane = lax.broadcasted_iota(jnp.int32, t.shape, 1)
    zero = jnp.zeros_like(t)
    return jnp.concatenate([jnp.where(lane < HEAD_DIM, t, zero),
                            jnp.where(lane >= HEAD_DIM, t, zero)], axis=0)


def _attn_kernel(q1, k1, v1, q2, k2, v2, q3, k3, v3, bias_ref, gate_ref, o_ref,
                 m_scr, l_scr, acc_scr, *, seq):
    left = lax.broadcasted_iota(jnp.int32, (BLK, LANES), 1) < HEAD_DIM
    ones_rows = jnp.ones((2 * BLK, LANES), BF16)
    r_idx = lax.broadcasted_iota(jnp.int32, (2 * BLK, LANES), 0)
    c_idx = lax.broadcasted_iota(jnp.int32, (2 * BLK, LANES), 1)
    ones_heads = jnp.where((r_idx < BLK) == (c_idx < HEAD_DIM), 1.0, 0.0).astype(BF16)

    jobs = []
    for pp in range(ATT_PAIRS):
        for g, (_, dil) in reversed(list(enumerate(PATTERNS))):
            sub = seq // dil
            for r in range(dil):
                for n in range(sub // BLK):
                    dst = pl.ds(n * BLK * dil + r, BLK, stride=dil) if dil > 1 else pl.ds(n * BLK, BLK)
                    jobs.append((pp, g, r * sub + n * BLK, n > 0, dst))
    group_refs = ((q1, k1, v1), (q2, k2, v2), (q3, k3, v3))

    def key_rows(ref, pp, row0, has_prev):
        if has_prev:
            return ref[0, pp, row0 - BLK:row0 + BLK, :]
        return ref[0, pp, row0:row0 + BLK, :]

    def scores_of(job):
        pp, g, row0, has_prev, _ = job
        q_ref, k_ref, _ = group_refs[g]
        q = q_ref[0, pp, row0:row0 + BLK, :]
        kk = key_rows(k_ref, pp, row0, has_prev)
        if has_prev:
            return _dot_nt(_head_stack(q), kk)
        return _dot_nt(q, _head_stack(kk))

    def softmax_of(job, s):
        pp, g, _, has_prev, _ = job
        es, ms = [], []
        for h in range(2):
            if has_prev:
                sh = s[h * BLK:(h + 1) * BLK, :] + bias_ref[g, 2 * pp + h]
            else:
                sh = s[:, h * BLK:(h + 1) * BLK] + bias_ref[g, 2 * pp + h, :, BLK:]
            mh = jnp.max(sh, axis=-1, keepdims=True)
            es.append(jnp.exp2(sh - mh).astype(BF16))
            ms.append(mh)
        return jnp.concatenate(es, axis=0 if has_prev else 1), jnp.where(left, ms[0], ms[1])

    def output_of(job, p, m_new):
        pp, g, row0, has_prev, dst = job
        vv = key_rows(group_refs[g][2], pp, row0, has_prev)
        if has_prev:
            pv = _dot(p, jnp.concatenate([vv, ones_rows], axis=1))
            acc = jnp.where(left, pv[:BLK, :LANES], pv[BLK:, :LANES])
            l_new = jnp.where(left, pv[:BLK, LANES:], pv[BLK:, LANES:])
        else:
            pv = _dot(p, jnp.concatenate([_head_stack(vv), ones_heads], axis=1))
            acc, l_new = pv[:, :LANES], pv[:, LANES:]
        if g > 0:
            m_scr[g - 1, pp, dst, :] = m_new
            l_scr[g - 1, pp, dst, :] = l_new
            acc_scr[g - 1, pp, dst, :] = acc
            return
        ms = [m_new] + [m_scr[k, pp, dst, :] for k in range(N_GROUPS - 1)]
        ls = [l_new] + [l_scr[k, pp, dst, :] for k in range(N_GROUPS - 1)]
        accs = [acc] + [acc_scr[k, pp, dst, :] for k in range(N_GROUPS - 1)]
        m_all = jnp.maximum(jnp.maximum(ms[0], ms[1]), ms[2])
        ws = [jnp.exp2(m - m_all) for m in ms]
        den = sum(w * l for w, l in zip(ws, ls))
        num = sum(w * a for w, a in zip(ws, accs))
        gate = gate_ref[0, dst, pp * LANES:(pp + 1) * LANES]
        o_ref[0, pp, dst, :] = (num / den * gate).astype(o_ref.dtype)

    pending = {u: scores_of(jobs[u]) for u in range(ATT_DEPTH)}
    for u, job in enumerate(jobs):
        p, m_new = softmax_of(job, pending.pop(u))
        if u + ATT_DEPTH < len(jobs):
            pending[u + ATT_DEPTH] = scores_of(jobs[u + ATT_DEPTH])
        output_of(job, p, m_new)


def _attention(qkv, bias, gate_dt):
    b, _, s, _ = qkv[0].shape
    steps = HEAD_PAIRS // ATT_PAIRS
    in_specs, args = [], []
    for arr in qkv:
        for part in range(3):
            in_specs.append(pl.BlockSpec((1, ATT_PAIRS, s, LANES),
                                         lambda i, j, part=part: (i, part * steps + j, 0, 0)))
            args.append(arr)
    in_specs.append(pl.BlockSpec((N_GROUPS, 2 * ATT_PAIRS, BLK, 2 * BLK), lambda i, j: (0, j, 0, 0)))
    in_specs.append(pl.BlockSpec((1, s, ATT_PAIRS * LANES), lambda i, j: (i, 0, j)))
    return pl.pallas_call(
        functools.partial(_attn_kernel, seq=s),
        grid=(b, steps),
        in_specs=in_specs,
        out_specs=pl.BlockSpec((1, ATT_PAIRS, s, LANES), lambda i, j: (i, j, 0, 0)),
        out_shape=jax.ShapeDtypeStruct((b, HEAD_PAIRS, s, LANES), BF16),
        scratch_shapes=[pltpu.VMEM((N_GROUPS - 1, ATT_PAIRS, s, LANES), F32)] * 3,
        compiler_params=pltpu.CompilerParams(
            dimension_semantics=("parallel", "parallel"), vmem_limit_bytes=VMEM_LIMIT),
        name="dilated_attention",
    )(*args, bias, gate_dt)


TIME_STRIDE = 4
TIME_SPAN = TIME_STRIDE * SUBLANES
HALO = SUBLANES


def _strided_rows(ref, base):
    pieces = [ref[pl.ds(base + TIME_SPAN * grp + i, SUBLANES, stride=TIME_STRIDE), :]
              for grp in range(CHUNK // TIME_SPAN) for i in range(TIME_STRIDE)]
    return jnp.concatenate(pieces, axis=0)


def _strided_time_of(idx):
    q = idx // SUBLANES
    return TIME_SPAN * (q // TIME_STRIDE) + q % TIME_STRIDE + TIME_STRIDE * (idx % SUBLANES)


def _silu(v):
    h = 0.5 * v
    return h * jnp.tanh(h) + h


def _dot_split2(a_f32, b_bf16):
    hi = a_f32.astype(BF16)
    lo = (a_f32 - hi.astype(F32)).astype(BF16)
    return _dot(hi, b_bf16) + _dot(lo, b_bf16)


def _ssd_kernel(xbc_ref, z_ref, dt_ref, convw_ref, convb_ref, dtb_ref, alog_ref, dskip_ref,
                normw_ref, expand_ref, o_ref, state_scr, xpad_scr, y_scr):
    @pl.when(pl.program_id(1) == 0)
    def _():
        state_scr[...] = jnp.zeros_like(state_scr)
        xpad_scr[:, :, 0:HALO, :] = jnp.zeros((SSD_BATCH, CONV_DIM // LANES, HALO, LANES), F32)

    t_row = _strided_time_of(lax.broadcasted_iota(jnp.int32, (CHUNK, CHUNK), 0))
    t_col = _strided_time_of(lax.broadcasted_iota(jnp.int32, (CHUNK, CHUNK), 1))
    causal = t_row >= t_col
    tril = jnp.where(causal, 1.0, 0.0).astype(BF16)
    neg_rate = -LOG2E * jnp.exp(alog_ref[...])

    def one_chunk(k, carry):
        for bb in range(SSD_BATCH):
            _ssd_chunk(bb, pl.multiple_of(k * CHUNK, CHUNK), causal, tril, neg_rate,
                       xbc_ref, z_ref, dt_ref, convw_ref, convb_ref, dtb_ref, dskip_ref, normw_ref,
                       expand_ref, o_ref, state_scr, xpad_scr, y_scr)
        return carry

    lax.fori_loop(0, xbc_ref.shape[1] // CHUNK, one_chunk, 0)


def _ssd_chunk(bb, row0, causal, tril, neg_rate, xbc_ref, z_ref, dt_ref, convw_ref, convb_ref,
               dtb_ref, dskip_ref, normw_ref, expand_ref, o_ref, state_scr, xpad_scr, y_scr):
    x_slabs = D_INNER // LANES
    rows = pl.ds(row0, CHUNK)

    def conv_silu(j):
        lanes = slice(j * LANES, (j + 1) * LANES)
        xpad_scr[bb, j, HALO:HALO + CHUNK, :] = xbc_ref[bb, rows, lanes]
        conv = convb_ref[:, lanes]
        for k in range(CONV_WIDTH):
            tap = _strided_rows(xpad_scr.at[bb, j], HALO - (CONV_WIDTH - 1) + k)
            conv = conv + convw_ref[k:k + 1, lanes] * tap
        xpad_scr[bb, j, 0:HALO, :] = xbc_ref[bb, pl.ds(row0 + CHUNK - HALO, HALO), lanes]
        return _silu(conv)

    raw = _strided_rows(dt_ref.at[bb], row0) + dtb_ref[...]
    dt = jnp.maximum(raw, 0.0) + jnp.log1p(jnp.exp(-jnp.abs(raw)))
    a = dt * neg_rate
    hi, mid, lo = _split3(a)
    acs = _dot(tril, hi) + _dot(tril, mid) + _dot(tril, lo)
    src_t = (acs - jnp.log2(dt)).T
    acs_end = acs[CHUNK - 1:CHUNK, :]

    def two_pieces(v):
        hi2 = v.astype(BF16)
        return hi2, (v - hi2.astype(F32)).astype(BF16)

    decay_in_p = two_pieces(jnp.exp2(acs))
    w_out_p = two_pieces(dt * jnp.exp2(acs_end - acs))

    slabs_per_group = GROUP_LANES // LANES
    for g in range(SSM_GROUPS):
        gl = slice(g * GROUP_LANES, (g + 1) * GROUP_LANES)
        expand = expand_ref[:, gl]
        decay_in, w_out = (_dot(hi2, expand) + _dot(lo2, expand)
                           for hi2, lo2 in (decay_in_p, w_out_p))
        chunk_decay = decay_in[CHUNK - 1:CHUNK, :]
        xs = jnp.concatenate([conv_silu(g * slabs_per_group + j) for j in range(slabs_per_group)],
                             axis=1)
        bg32 = conv_silu(x_slabs + g)
        bg = bg32.astype(BF16)
        cg = conv_silu(x_slabs + SSM_GROUPS + g).astype(BF16)
        x_b = xs.astype(BF16)
        x_out = (xs * w_out).astype(BF16)
        cb = _dot_nt(cg, bg)
        prev = state_scr[bb, g]
        y_off = _dot(cg, prev.astype(BF16)) * decay_in
        state_scr[bb, g] = prev * chunk_decay + _dot(bg32.T.astype(BF16), x_out)
        for j in range(HEADS_PER_SSM_GROUP // 2):
            gs = []
            for hh in (g * HEADS_PER_SSM_GROUP + 2 * j, g * HEADS_PER_SSM_GROUP + 2 * j + 1):
                seg = acs[:, hh:hh + 1] - src_t[hh:hh + 1, :]
                lmat = jnp.exp2(jnp.where(causal, seg, -jnp.inf))
                gs.append((cb * lmat).astype(BF16))
            slab = g * slabs_per_group + j
            pair = slice(j * LANES, (j + 1) * LANES)
            y_diag = _dot(jnp.concatenate(gs, axis=1), _head_stack(x_b[:, pair]))
            y = (y_diag + y_off[:, pair]
                 + dskip_ref[:, slab * LANES:(slab + 1) * LANES] * xs[:, pair])
            for q in range(CHUNK // SUBLANES):
                start = TIME_SPAN * (q // TIME_STRIDE) + q % TIME_STRIDE
                y_scr[bb, slab, pl.ds(start, SUBLANES, stride=TIME_STRIDE), :] = (
                    y[q * SUBLANES:(q + 1) * SUBLANES, :])

        yg = jnp.concatenate([y_scr[bb, g * slabs_per_group + j] for j in range(slabs_per_group)],
                             axis=1)
        u = yg * z_ref[bb, rows, gl]
        ms = jnp.mean(jnp.square(u), axis=-1, keepdims=True)
        o_ref[bb, rows, gl] =(u * lax.rsqrt(ms + RMS_EPS) * normw_ref[:, gl]).astype(o_ref.dtype)


def _ssd(xbc, z, gate_dt, conv_w, conv_b, dt_bias, a_log, d_skip, norm_w):
    b, s, _ = xbc.shape
    pad = LANES - SSM_HEADS
    dtb = jnp.pad(dt_bias, (0, pad)).reshape(1, LANES)
    alog = jnp.pad(a_log, (0, pad)).reshape(1, LANES)
    dskip = jnp.repeat(d_skip, D_INNER // SSM_HEADS).reshape(1, D_INNER)
    expand = np.zeros((LANES, D_INNER), np.float32)
    expand[np.arange(D_INNER) // (D_INNER // SSM_HEADS), np.arange(D_INNER)] = 1.0
    expand = jnp.asarray(expand, BF16)
    dt_tile = (GATT_END - V_END) // LANES

    def whole(shape):
        return pl.BlockSpec(shape, lambda i, j: (0,) * len(shape))

    return pl.pallas_call(
        _ssd_kernel,
        grid=(b // SSD_BATCH, s // SSD_ROWS),
        in_specs=[pl.BlockSpec((SSD_BATCH, SSD_ROWS, CONV_DIM), lambda i, j: (i, j, 0)),
                  pl.BlockSpec((SSD_BATCH, SSD_ROWS, D_INNER), lambda i, j: (i, j, 0)),
                  pl.BlockSpec((SSD_BATCH, SSD_ROWS, LANES), lambda i, j: (i, j, dt_tile)),
                  whole((CONV_WIDTH, CONV_DIM)), whole((1, CONV_DIM)),
                  whole((1, LANES)), whole((1, LANES)), whole((1, D_INNER)),
                  whole((1, D_INNER)), whole((LANES, D_INNER))],
        out_specs=pl.BlockSpec((SSD_BATCH, SSD_ROWS, D_INNER), lambda i, j: (i, j, 0)),
        out_shape=jax.ShapeDtypeStruct((b, s, D_INNER), BF16),
        scratch_shapes=[pltpu.VMEM((SSD_BATCH, SSM_GROUPS, D_STATE, GROUP_LANES), F32),
                        pltpu.VMEM((SSD_BATCH, CONV_DIM // LANES, HALO + CHUNK, LANES), F32),
                        pltpu.VMEM((SSD_BATCH, D_INNER // LANES, CHUNK, LANES), F32)],
        compiler_params=pltpu.CompilerParams(
            dimension_semantics=("parallel", "arbitrary"), vmem_limit_bytes=VMEM_LIMIT),
        name="ssd_mixer",
    )(xbc, z, gate_dt, conv_w, conv_b.reshape(1, CONV_DIM), dtb, alog, dskip,
      norm_w.reshape(1, D_INNER), expand)


def _out_kernel(att_ref, ssm_ref, gm_ref, gp_ref, p_ref, x_ref, wb_ref, wo_ref, wp_ref,
                bg_ref, lng_ref, lnb_ref, o_ref, *, alpha):
    subs = [slice(t * OUT_SUBROWS, (t + 1) * OUT_SUBROWS) for t in range(OUT_ROWS // OUT_SUBROWS)]
    branch = []
    for rs in subs:
        o_att = jnp.concatenate([att_ref[0, j, rs, :] for j in range(HEAD_PAIRS)], axis=1)
        branch.append((_dot(o_att, wb_ref[:ATT_OUT, :]), _dot(ssm_ref[0, rs, :], wb_ref[ATT_OUT:, :]),
                       _dot(p_ref[0, rs, :].astype(BF16), wp_ref[...])))
    mixes = []
    for rs, (y_a, y_b, _) in zip(subs, branch):
        merged = (_sigmoid_t(gm_ref[0, rs, :D_MODEL] + bg_ref[0:1, :]) * y_a
                  + _sigmoid_t(gm_ref[0, rs, D_MODEL:] + bg_ref[1:2, :]) * y_b)
        mixes.append(_dot(merged.astype(BF16), wo_ref[...]))
    for rs, (_, _, pp), mix in zip(subs, branch, mixes):
        ple = _sigmoid_t(gp_ref[0, rs, :] + bg_ref[2:3, :]) * pp
        h = alpha * x_ref[0, rs, :] + mix + ple
        mu = jnp.mean(h, axis=-1, keepdims=True)
        var = jnp.mean(jnp.square(h - mu), axis=-1, keepdims=True)
        o_ref[0, rs, :] = (h - mu) * lax.rsqrt(var + LN_EPS) * lng_ref[...] + lnb_ref[...]


def _output(o_att, y_ssm, g_merge, g_ple, p_i, x, w_branch, w_out, w_ple, b_gate, ln_g, ln_b, alpha):
    b, s, d = x.shape
    rows = OUT_ROWS

    def whole(shape):
        return pl.BlockSpec(shape, lambda i, j: (0,) * len(shape), pipeline_mode=pl.Buffered(1))

    def tile(n):
        return pl.BlockSpec((1, rows, n), lambda i, j: (i, j, 0))

    return pl.pallas_call(
        functools.partial(_out_kernel, alpha=alpha),
        grid=(b, s // rows),
        in_specs=[pl.BlockSpec((1, HEAD_PAIRS, rows, LANES), lambda i, j: (i, 0, j, 0)),
                  tile(D_INNER), tile(2 * D_MODEL), tile(D_MODEL), tile(PLE_DIM), tile(D_MODEL),
                  whole((ATT_OUT + D_INNER, D_MODEL)), whole((D_MODEL, D_MODEL)),
                  whole((PLE_DIM, D_MODEL)), whole((3, D_MODEL)), whole((1, D_MODEL)),
                  whole((1, D_MODEL))],
        out_specs=tile(D_MODEL),
        out_shape=jax.ShapeDtypeStruct((b, s, d), F32),
        compiler_params=pltpu.CompilerParams(
            dimension_semantics=("parallel", "parallel"), vmem_limit_bytes=VMEM_LIMIT),
        name="merge_out_ln",
    )(o_att, y_ssm, g_merge, g_ple, p_i, x, w_branch.astype(BF16), w_out.astype(BF16),
      w_ple.astype(BF16), b_gate, ln_g.reshape(1, d), ln_b.reshape(1, d))


def _layer(x, p_i, w_in, b_gate, conv_w, conv_b, dt_bias, a_log, d_skip, ssm_norm_w,
           w_branch, w_out, w_ple, ln_g, ln_b, bias, alpha):
    row_scale = jnp.ones((w_in.shape[1], 1), F32).at[:Q_END].set(LOG2E * HEAD_DIM ** -0.5)
    wt = (jnp.swapaxes(w_in, 0, 1) * row_scale).astype(BF16)
    group_w = HEADS_PER_GROUP * HEAD_DIM
    qkv = []
    for g, (_, dil) in enumerate(PATTERNS):
        wg = jnp.concatenate([wt[part + g * group_w:part + (g + 1) * group_w]
                              for part in (0, Q_END, K_END)], axis=0)
        qkv.append(_project(x, wg, dilation=dil, tiled_out=True, out_dtype=BF16, rows=QKV_ROWS,
                            name=f"proj_qkv{g}"))
    w_gates = jnp.concatenate(
        [wt[V_END:GATT_END], jnp.pad(wt[XBC_END:DT_END], ((0, LANES - SSM_HEADS), (0, 0))),
         wt[GMERGE_END:]], axis=0)
    gate_dt, g_ple = _project(x, w_gates, widths=(ATT_OUT + LANES, D_MODEL),
                              silu_widths=(ATT_OUT, 0), name="proj_gates")
    w_z_gm = jnp.concatenate([wt[GATT_END:Z_END], wt[DT_END:GMERGE_END]], axis=0)
    z_gate, g_merge = _project(x, w_z_gm, widths=(D_INNER, 2 * D_MODEL),
                               silu_widths=(D_INNER, 0), name="proj_z_gmerge")
    (xbc,) = _project(x, wt[Z_END:XBC_END], widths=(CONV_DIM,), name="proj_xbc")

    o_att = _attention(qkv, bias, gate_dt)
    y_ssm = _ssd(xbc, z_gate, gate_dt, conv_w, conv_b, dt_bias, a_log, d_skip, ssm_norm_w)
    return _output(o_att, y_ssm, g_merge, g_ple, p_i, x, w_branch, w_out, w_ple, b_gate,
                   ln_g, ln_b, alpha)


def kernel(x, p, w_in, b_gate, conv_w, conv_b, dt_bias, a_log, d_skip, ssm_norm_w, w_branch, w_out,
           w_ple, ln_g, ln_b, rel_bias):
    depth = w_in.shape[0]
    alpha = (2.0 * depth) ** 0.25
    bias = _bias_tiles(rel_bias)
    for i in range(depth):
        x = _layer(x, p[i], w_in[i], b_gate[i], conv_w[i], conv_b[i], dt_bias[i], a_log[i],
                   d_skip[i], ssm_norm_w[i], w_branch[i], w_out[i], w_ple[i], ln_g[i], ln_b[i],
                   bias, alpha)
    return x
```

```python
import functools
import math

import numpy as np
import jax
import jax.numpy as jnp
from jax import lax
from jax.experimental import pallas as pl
from jax.experimental.pallas import tpu as pltpu

F32 = jnp.float32
BF16 = jnp.bfloat16

LANES = 128
SUBLANES = 8
V7X_VMEM_BYTES = 64 * 1024 * 1024
VMEM_LIMIT = (V7X_VMEM_BYTES * 3) // 4

D_MODEL = 1024
HEAD_DIM = 64
HEADS_PER_GROUP = 12
PATTERNS = ((128, 1), (512, 4), (2048, 16))
N_GROUPS = 3
ATT_HEADS = N_GROUPS * HEADS_PER_GROUP
ATT_QKV = ATT_HEADS * HEAD_DIM
ATT_OUT = HEADS_PER_GROUP * HEAD_DIM
BLK = 128
NUM_BUCKETS = 32
MAX_DISTANCE = 2048
D_INNER = 2048
SSM_HEADS = 32
SSM_GROUPS = 4
D_STATE = 128
CONV_WIDTH = 4
CONV_DIM = D_INNER + 2 * SSM_GROUPS * D_STATE
CHUNK = 128
PLE_DIM = 256
LN_EPS = 1e-5
RMS_EPS = 1e-5
LOG2E = math.log2(math.e)
Q_END = ATT_QKV
K_END = Q_END + ATT_QKV
V_END = K_END + ATT_QKV
GATT_END = V_END + ATT_OUT
Z_END = GATT_END + D_INNER
XBC_END = Z_END + CONV_DIM
DT_END = XBC_END + SSM_HEADS
GMERGE_END = DT_END + 2 * D_MODEL

HEAD_PAIRS = HEADS_PER_GROUP // 2
GROUP_LANES = D_INNER // SSM_GROUPS
HEADS_PER_SSM_GROUP = SSM_HEADS // SSM_GROUPS
PROJ_ROWS = 512
QKV_ROWS = 1024
PERM_ROWS = 256
SSD_ROWS = CHUNK
SSD_BATCH = 4
OUT_ROWS = 512
OUT_SUBROWS = 256


def _sigmoid_t(v):
    return 0.5 * jnp.tanh(0.5 * v) + 0.5


def _split3(v):
    hi = v.astype(BF16)
    r1 = v - hi.astype(F32)
    mid = r1.astype(BF16)
    lo = (r1 - mid.astype(F32)).astype(BF16)
    return hi, mid, lo


def _dot(a, b):
    return jnp.dot(a, b, preferred_element_type=F32)


def _dot_nt(a, b):
    return lax.dot_general(a, b, (((1,), (1,)), ((), ())), preferred_element_type=F32)


def _proj_kernel(*refs, permute, tiled_out, silu_widths):
    if permute:
        x_ref, perm_ref, w_ref, *o_refs = refs
    else:
        x_ref, w_ref, *o_refs = refs
    lhs = x_ref[0].astype(BF16)
    if permute:
        lhs = jnp.concatenate(
            [_dot(perm_ref[...], lhs[t * PERM_ROWS:(t + 1) * PERM_ROWS, :]).astype(BF16)
             for t in range(lhs.shape[0] // PERM_ROWS)], axis=0)
    res = _dot_nt(lhs, w_ref[...])
    if tiled_out:
        (o_ref,) = o_refs
        _, n_tiles, n_classes, n_sub, sub, _ = o_ref.shape
        sub_rows = n_classes * sub
        for c in range(n_tiles):
            for t in range(n_sub):
                tile = res[t * sub_rows:(t + 1) * sub_rows, c * LANES:(c + 1) * LANES]
                o_ref[0, c, :, t] = tile.reshape(n_classes, sub, LANES).astype(o_ref.dtype)
    else:
        off = 0
        for o_ref, n_silu in zip(o_refs, silu_widths):
            width = o_ref.shape[-1]
            if n_silu:
                o_ref[0, :, :n_silu] = _silu(res[:, off:off + n_silu]).astype(o_ref.dtype)
            if n_silu < width:
                o_ref[0, :, n_silu:] = res[:, off + n_silu:off + width].astype(o_ref.dtype)
            off += width


def _class_permutation(dilation):
    sub = PERM_ROWS // dilation
    dst = np.arange(PERM_ROWS)
    src = dilation * (dst % sub) + dst // sub
    perm = np.zeros((PERM_ROWS, PERM_ROWS), np.float32)
    perm[dst, src] = 1.0
    return jnp.asarray(perm, BF16)


def _project(x, w, *, name, dilation=1, tiled_out=False, widths=None, silu_widths=None,
             out_dtype=F32, rows=PROJ_ROWS):
    b, s, d = x.shape
    n = w.shape[0]
    steps = s // rows
    permute = dilation > 1
    in_specs = [pl.BlockSpec((1, rows, d), lambda i, j: (i, j, 0))]
    args = [x]
    if permute:
        in_specs.append(pl.BlockSpec((PERM_ROWS, PERM_ROWS), lambda i, j: (0, 0)))
        args.append(_class_permutation(dilation))
    in_specs.append(pl.BlockSpec((n, d), lambda i, j: (0, 0)))
    args.append(w)
    if tiled_out:
        n_sub = rows // PERM_ROWS if permute else 1
        sub = rows // (n_sub * dilation)
        out_shape = jax.ShapeDtypeStruct((b, n // LANES, dilation, steps * n_sub, sub, LANES), out_dtype)
        o_spec = pl.BlockSpec((1, n // LANES, dilation, n_sub, sub, LANES),
                              lambda i, j: (i, 0, 0, j, 0, 0))
    else:
        assert not permute and sum(widths) == n
        out_shape = tuple(jax.ShapeDtypeStruct((b, s, wd), out_dtype) for wd in widths)
        o_spec = tuple(pl.BlockSpec((1, rows, wd), lambda i, j: (i, j, 0)) for wd in widths)
    out = pl.pallas_call(
        functools.partial(_proj_kernel, permute=permute, tiled_out=tiled_out,
                          silu_widths=silu_widths or (0,) * len(widths or ())),
        grid=(b, steps),
        in_specs=in_specs,
        out_specs=o_spec,
        out_shape=out_shape,
        compiler_params=pltpu.CompilerParams(
            dimension_semantics=("parallel", "parallel"), vmem_limit_bytes=VMEM_LIMIT),
        name=name,
    )(*args)
    return out.reshape(b, n // LANES, s, LANES) if tiled_out else out


def _bucket_tiles():
    qi = np.arange(BLK)[:, None]
    kj = np.arange(2 * BLK)[None, :]
    delta = qi + BLK - kj
    max_exact = NUM_BUCKETS // 2
    out = []
    for window, dil in PATTERNS:
        valid = (delta >= 0) & (delta <= window // dil)
        dist = np.maximum(delta, 0) * dil
        d_f = np.maximum(dist, 1).astype(np.float32)
        large = max_exact + (np.log(d_f / max_exact) / np.float32(math.log(MAX_DISTANCE / max_exact))
                             * (NUM_BUCKETS - max_exact)).astype(np.int32)
        large = np.minimum(large, NUM_BUCKETS - 1)
        bucket = np.where(dist < max_exact, dist, large)
        out.append(np.where(valid, bucket, -1).astype(np.int32))
    return np.stack(out)


def _bias_kernel(tbl_ref, bucket_ref, o_ref):
    g = pl.program_id(0)
    bucket = bucket_ref[0]
    o_ref[...] = jnp.full(o_ref.shape, -jnp.inf, F32)
    for bkt in range(NUM_BUCKETS):
        hit = bucket == bkt
        for h in range(HEADS_PER_GROUP):
            o_ref[0, h] = jnp.where(hit, LOG2E * tbl_ref[bkt, g * HEADS_PER_GROUP + h], o_ref[0, h])


def _bias_tiles(rel_bias):
    buckets = jnp.asarray(_bucket_tiles())
    return pl.pallas_call(
        _bias_kernel,
        grid=(N_GROUPS,),
        in_specs=[pl.BlockSpec(memory_space=pltpu.SMEM),
                  pl.BlockSpec((1, BLK, 2 * BLK), lambda g: (g, 0, 0))],
        out_specs=pl.BlockSpec((1, HEADS_PER_GROUP, BLK, 2 * BLK), lambda g: (g, 0, 0, 0)),
        out_shape=jax.ShapeDtypeStruct((N_GROUPS, HEADS_PER_GROUP, BLK, 2 * BLK), F32),
        name="bias_tiles",
    )(rel_bias, buckets)


ATT_PAIRS = 2
ATT_DEPTH = 3


def _head_stack(t):
    lane = lax.broadcasted_iota(jnp.int32, t.shape, 1)
    zero = jnp.zeros_like(t)
    return jnp.concatenate([jnp.where(lane < HEAD_DIM, t, zero),
                            jnp.where(lane >= HEAD_DIM, t, zero)], axis=0)


def _attn_kernel(q1, k1, v1, q2, k2, v2, q3, k3, v3, bias_ref, gate_ref, o_ref,
                 m_scr, l_scr, acc_scr, *, seq):
    left = lax.broadcasted_iota(jnp.int32, (BLK, LANES), 1) < HEAD_DIM
    ones_rows = jnp.ones((2 * BLK, LANES), BF16)
    r_idx = lax.broadcasted_iota(jnp.int32, (2 * BLK, LANES), 0)
    c_idx = lax.broadcasted_iota(jnp.int32, (2 * BLK, LANES), 1)
    ones_heads = jnp.where((r_idx < BLK) == (c_idx < HEAD_DIM), 1.0, 0.0).astype(BF16)

    jobs = []
    for pp in range(ATT_PAIRS):
        for g, (_, dil) in reversed(list(enumerate(PATTERNS))):
            sub = seq // dil
            for r in range(dil):
                for n in range(sub // BLK):
                    dst = pl.ds(n * BLK * dil + r, BLK, stride=dil) if dil > 1 else pl.ds(n * BLK, BLK)
                    jobs.append((pp, g, r * sub + n * BLK, n > 0, dst))
    group_refs = ((q1, k1, v1), (q2, k2, v2), (q3, k3, v3))

    def key_rows(ref, pp, row0, has_prev):
        if has_prev:
            return ref[0, pp, row0 - BLK:row0 + BLK, :]
        return ref[0, pp, row0:row0 + BLK, :]

    def scores_of(job):
        pp, g, row0, has_prev, _ = job
        q_ref, k_ref, _ = group_refs[g]
        q = q_ref[0, pp, row0:row0 + BLK, :]
        kk = key_rows(k_ref, pp, row0, has_prev)
        if has_prev:
            return _dot_nt(_head_stack(q), kk)
        return _dot_nt(q, _head_stack(kk))

    def softmax_of(job, s):
        pp, g, _, has_prev, _ = job
        es, ms = [], []
        for h in range(2):
            if has_prev:
                sh = s[h * BLK:(h + 1) * BLK, :] + bias_ref[g, 2 * pp + h]
            else:
                sh = s[:, h * BLK:(h + 1) * BLK] + bias_ref[g, 2 * pp + h, :, BLK:]
            mh = jnp.max(sh, axis=-1, keepdims=True)
            es.append(jnp.exp2(sh - mh).astype(BF16))
            ms.append(mh)
        return jnp.concatenate(es, axis=0 if has_prev else 1), jnp.where(left, ms[0], ms[1])

    def output_of(job, p, m_new):
        pp, g, row0, has_prev, dst = job
        vv = key_rows(group_refs[g][2], pp, row0, has_prev)
        if has_prev:
            pv = _dot(p, jnp.concatenate([vv, ones_rows], axis=1))
            acc = jnp.where(left, pv[:BLK, :LANES], pv[BLK:, :LANES])
            l_new = jnp.where(left, pv[:BLK, LANES:], pv[BLK:, LANES:])
        else:
            pv = _dot(p, jnp.concatenate([_head_stack(vv), ones_heads], axis=1))
            acc, l_new = pv[:, :LANES], pv[:, LANES:]
        if g > 0:
            m_scr[g - 1, pp, dst, :] = m_new
            l_scr[g - 1, pp, dst, :] = l_new
            acc_scr[g - 1, pp, dst, :] = acc
            return
        ms = [m_new] + [m_scr[k, pp, dst, :] for k in range(N_GROUPS - 1)]
        ls = [l_new] + [l_scr[k, pp, dst, :] for k in range(N_GROUPS - 1)]
        accs = [acc] + [acc_scr[k, pp, dst, :] for k in range(N_GROUPS - 1)]
        m_all = jnp.maximum(jnp.maximum(ms[0], ms[1]), ms[2])
        ws = [jnp.exp2(m - m_all) for m in ms]
        den = sum(w * l for w, l in zip(ws, ls))
        num = sum(w * a for w, a in zip(ws, accs))
        gate = gate_ref[0, dst, pp * LANES:(pp + 1) * LANES]
        o_ref[0, pp, dst, :] = (num / den * gate).astype(o_ref.dtype)

    pending = {u: scores_of(jobs[u]) for u in range(ATT_DEPTH)}
    for u, job in enumerate(jobs):
        p, m_new = softmax_of(job, pending.pop(u))
        if u + ATT_DEPTH < len(jobs):
            pending[u + ATT_DEPTH] = scores_of(jobs[u + ATT_DEPTH])
        output_of(job, p, m_new)


def _attention(qkv, bias, gate_dt):
    b, _, s, _ = qkv[0].shape
    steps = HEAD_PAIRS // ATT_PAIRS
    in_specs, args = [], []
    for arr in qkv:
        for part in range(3):
            in_specs.append(pl.BlockSpec((1, ATT_PAIRS, s, LANES),
                                         lambda i, j, part=part: (i, part * steps + j, 0, 0)))
            args.append(arr)
    in_specs.append(pl.BlockSpec((N_GROUPS, 2 * ATT_PAIRS, BLK, 2 * BLK), lambda i, j: (0, j, 0, 0)))
    in_specs.append(pl.BlockSpec((1, s, ATT_PAIRS * LANES), lambda i, j: (i, 0, j)))
    return pl.pallas_call(
        functools.partial(_attn_kernel, seq=s),
        grid=(b, steps),
        in_specs=in_specs,
        out_specs=pl.BlockSpec((1, ATT_PAIRS, s, LANES), lambda i, j: (i, j, 0, 0)),
        out_shape=jax.ShapeDtypeStruct((b, HEAD_PAIRS, s, LANES), BF16),
        scratch_shapes=[pltpu.VMEM((N_GROUPS - 1, ATT_PAIRS, s, LANES), F32)] * 3,
        compiler_params=pltpu.CompilerParams(
            dimension_semantics=("parallel", "parallel"), vmem_limit_bytes=VMEM_LIMIT),
        name="dilated_attention",
    )(*args, bias, gate_dt)


TIME_STRIDE = 4
TIME_SPAN = TIME_STRIDE * SUBLANES
HALO = SUBLANES


def _strided_rows(ref, base):
    pieces = [ref[pl.ds(base + TIME_SPAN * grp + i, SUBLANES, stride=TIME_STRIDE), :]
              for grp in range(CHUNK // TIME_SPAN) for i in range(TIME_STRIDE)]
    return jnp.concatenate(pieces, axis=0)


def _strided_time_of(idx):
    q = idx // SUBLANES
    return TIME_SPAN * (q // TIME_STRIDE) + q % TIME_STRIDE + TIME_STRIDE * (idx % SUBLANES)


def _silu(v):
    h = 0.5 * v
    return h * jnp.tanh(h) + h


def _ssd_kernel(xbc_ref, z_ref, dt_ref, convw_ref, convb_ref, dtb_ref, alog_ref, dskip_ref,
                normw_ref, expand_ref, o_ref, state_scr, xpad_scr, y_scr):
    @pl.when(pl.program_id(1) == 0)
    def _():
        state_scr[...] = jnp.zeros_like(state_scr)
        xpad_scr[:, :, 0:HALO, :] = jnp.zeros((SSD_BATCH, CONV_DIM // LANES, HALO, LANES), F32)

    t_row = _strided_time_of(lax.broadcasted_iota(jnp.int32, (CHUNK, CHUNK), 0))
    t_col = _strided_time_of(lax.broadcasted_iota(jnp.int32, (CHUNK, CHUNK), 1))
    causal = t_row >= t_col
    tril = jnp.where(causal, 1.0, 0.0).astype(BF16)
    neg_rate = -LOG2E * jnp.exp(alog_ref[...])

    def one_chunk(k, carry):
        for bb in range(SSD_BATCH):
            _ssd_chunk(bb, pl.multiple_of(k * CHUNK, CHUNK), causal, tril, neg_rate,
                       xbc_ref, z_ref, dt_ref, convw_ref, convb_ref, dtb_ref, dskip_ref, normw_ref,
                       expand_ref, o_ref, state_scr, xpad_scr, y_scr)
        return carry

    lax.fori_loop(0, xbc_ref.shape[1] // CHUNK, one_chunk, 0)


def _ssd_chunk(bb, row0, causal, tril, neg_rate, xbc_ref, z_ref, dt_ref, convw_ref, convb_ref,
               dtb_ref, dskip_ref, normw_ref, expand_ref, o_ref, state_scr, xpad_scr, y_scr):
    x_slabs = D_INNER // LANES
    rows = pl.ds(row0, CHUNK)

    def conv_silu(j):
        lanes = slice(j * LANES, (j + 1) * LANES)
        xpad_scr[bb, j, HALO:HALO + CHUNK, :] = xbc_ref[bb, rows, lanes]
        conv = convb_ref[:, lanes]
        for k in range(CONV_WIDTH):
            tap = _strided_rows(xpad_scr.at[bb, j], HALO - (CONV_WIDTH - 1) + k)
            conv = conv + convw_ref[k:k + 1, lanes] * tap
        xpad_scr[bb, j, 0:HALO, :] = xbc_ref[bb, pl.ds(row0 + CHUNK - HALO, HALO), lanes]
        return _silu(conv)

    raw = _strided_rows(dt_ref.at[bb], row0) + dtb_ref[...]
    dt = jnp.maximum(raw, 0.0) + jnp.log1p(jnp.exp(-jnp.abs(raw)))
    a = dt * neg_rate
    hi, mid, lo = _split3(a)
    acs = _dot(tril, hi) + _dot(tril, mid) + _dot(tril, lo)
    src_t = (acs - jnp.log2(dt)).T
    acs_end = acs[CHUNK - 1:CHUNK, :]

    def two_pieces(v):
        hi2 = v.astype(BF16)
        return hi2, (v - hi2.astype(F32)).astype(BF16)

    decay_in_p = two_pieces(jnp.exp2(acs))
    w_out_p = two_pieces(dt * jnp.exp2(acs_end - acs))

    slabs_per_group = GROUP_LANES // LANES
    for g in range(SSM_GROUPS):
        gl = slice(g * GROUP_LANES, (g + 1) * GROUP_LANES)
        expand = expand_ref[:, gl]
        decay_in, w_out = (_dot(hi2, expand) + _dot(lo2, expand)
                           for hi2, lo2 in (decay_in_p, w_out_p))
        chunk_decay = decay_in[CHUNK - 1:CHUNK, :]
        xs = jnp.concatenate([conv_silu(g * slabs_per_group + j) for j in range(slabs_per_group)],
                             axis=1)
        bg32 = conv_silu(x_slabs + g)
        bg = bg32.astype(BF16)
        cg = conv_silu(x_slabs + SSM_GROUPS + g).astype(BF16)
        x_b = xs.astype(BF16)
        x_out = (xs * w_out).astype(BF16)
        cb = _dot_nt(cg, bg)
        prev = state_scr[bb, g]
        y_off = _dot(cg, prev.astype(BF16)) * decay_in
        state_scr[bb, g] = prev * chunk_decay + _dot(bg32.T.astype(BF16), x_out)
        for j in range(HEADS_PER_SSM_GROUP // 2):
            gs = []
            for hh in (g * HEADS_PER_SSM_GROUP + 2 * j, g * HEADS_PER_SSM_GROUP + 2 * j + 1):
                seg = acs[:, hh:hh + 1] - src_t[hh:hh + 1, :]
                lmat = jnp.exp2(jnp.where(causal, seg, -jnp.inf))
                gs.append((cb * lmat).astype(BF16))
            slab = g * slabs_per_group + j
            pair = slice(j * LANES, (j + 1) * LANES)
            y_diag = _dot(jnp.concatenate(gs, axis=1), _head_stack(x_b[:, pair]))
            y = (y_diag + y_off[:, pair]
                 + dskip_ref[:, slab * LANES:(slab + 1) * LANES] * xs[:, pair])
            for q in range(CHUNK // SUBLANES):
                start = TIME_SPAN * (q // TIME_STRIDE) + q % TIME_STRIDE
                y_scr[bb, slab, pl.ds(start, SUBLANES, stride=TIME_STRIDE), :] = (
                    y[q * SUBLANES:(q + 1) * SUBLANES, :])

        yg = jnp.concatenate([y_scr[bb, g * slabs_per_group + j] for j in range(slabs_per_group)],
                             axis=1)
        u = yg * z_ref[bb, rows, gl]
        ms = jnp.mean(jnp.square(u), axis=-1, keepdims=True)
        o_ref[bb, rows, gl] =(u * lax.rsqrt(ms + RMS_EPS) * normw_ref[:, gl]).astype(o_ref.dtype)


def _ssd(xbc, z, gate_dt, conv_w, conv_b, dt_bias, a_log, d_skip, norm_w):
    b, s, _ = xbc.shape
    assert b % SSD_BATCH == 0 and s % SSD_ROWS == 0
    pad = LANES - SSM_HEADS
    dtb = jnp.pad(dt_bias, (0, pad)).reshape(1, LANES)
    alog = jnp.pad(a_log, (0, pad)).reshape(1, LANES)
    dskip = jnp.repeat(d_skip, D_INNER // SSM_HEADS).reshape(1, D_INNER)
    expand = np.zeros((LANES, D_INNER), np.float32)
    expand[np.arange(D_INNER) // (D_INNER // SSM_HEADS), np.arange(D_INNER)] = 1.0
    expand = jnp.asarray(expand, BF16)
    dt_tile = (GATT_END - V_END) // LANES

    def whole(shape):
        return pl.BlockSpec(shape, lambda i, j: (0,) * len(shape))

    return pl.pallas_call(
        _ssd_kernel,
        grid=(b // SSD_BATCH, s // SSD_ROWS),
        in_specs=[pl.BlockSpec((SSD_BATCH, SSD_ROWS, CONV_DIM), lambda i, j: (i, j, 0)),
                  pl.BlockSpec((SSD_BATCH, SSD_ROWS, D_INNER), lambda i, j: (i, j, 0)),
                  pl.BlockSpec((SSD_BATCH, SSD_ROWS, LANES), lambda i, j: (i, j, dt_tile)),
                  whole((CONV_WIDTH, CONV_DIM)), whole((1, CONV_DIM)),
                  whole((1, LANES)), whole((1, LANES)), whole((1, D_INNER)),
                  whole((1, D_INNER)), whole((LANES, D_INNER))],
        out_specs=pl.BlockSpec((SSD_BATCH, SSD_ROWS, D_INNER), lambda i, j: (i, j, 0)),
        out_shape=jax.ShapeDtypeStruct((b, s, D_INNER), BF16),
        scratch_shapes=[pltpu.VMEM((SSD_BATCH, SSM_GROUPS, D_STATE, GROUP_LANES), F32),
                        pltpu.VMEM((SSD_BATCH, CONV_DIM // LANES, HALO + CHUNK, LANES), F32),
                        pltpu.VMEM((SSD_BATCH, D_INNER // LANES, CHUNK, LANES), F32)],
        compiler_params=pltpu.CompilerParams(
            dimension_semantics=("parallel", "arbitrary"), vmem_limit_bytes=VMEM_LIMIT),
        name="ssd_mixer",
    )(xbc, z, gate_dt, conv_w, conv_b.reshape(1, CONV_DIM), dtb, alog, dskip,
      norm_w.reshape(1, D_INNER), expand)


def _out_kernel(att_ref, ssm_ref, gm_ref, gp_ref, p_ref, x_ref, wb_ref, wo_ref, wp_ref,
                bg_ref, lng_ref, lnb_ref, o_ref, *, alpha):
    subs = [slice(t * OUT_SUBROWS, (t + 1) * OUT_SUBROWS) for t in range(OUT_ROWS // OUT_SUBROWS)]
    branch = []
    for rs in subs:
        o_att = jnp.concatenate([att_ref[0, j, rs, :] for j in range(HEAD_PAIRS)], axis=1)
        branch.append((_dot(o_att, wb_ref[:ATT_OUT, :]), _dot(ssm_ref[0, rs, :], wb_ref[ATT_OUT:, :]),
                       _dot(p_ref[0, rs, :].astype(BF16), wp_ref[...])))
    mixes = []
    for rs, (y_a, y_b, _) in zip(subs, branch):
        merged = (_sigmoid_t(gm_ref[0, rs, :D_MODEL] + bg_ref[0:1, :]) * y_a
                  + _sigmoid_t(gm_ref[0, rs, D_MODEL:] + bg_ref[1:2, :]) * y_b)
        mixes.append(_dot(merged.astype(BF16), wo_ref[...]))
    for rs, (_, _, pp), mix in zip(subs, branch, mixes):
        ple = _sigmoid_t(gp_ref[0, rs, :] + bg_ref[2:3, :]) * pp
        h = alpha * x_ref[0, rs, :] + mix + ple
        mu = jnp.mean(h, axis=-1, keepdims=True)
        var = jnp.mean(jnp.square(h - mu), axis=-1, keepdims=True)
        o_ref[0, rs, :] = (h - mu) * lax.rsqrt(var + LN_EPS) * lng_ref[...] + lnb_ref[...]


def _output(o_att, y_ssm, g_merge, g_ple, p_i, x, w_branch, w_out, w_ple, b_gate, ln_g, ln_b, alpha):
    b, s, d = x.shape
    rows = OUT_ROWS

    def whole(shape):
        return pl.BlockSpec(shape, lambda i, j: (0,) * len(shape), pipeline_mode=pl.Buffered(1))

    def tile(n):
        return pl.BlockSpec((1, rows, n), lambda i, j: (i, j, 0))

    return pl.pallas_call(
        functools.partial(_out_kernel, alpha=alpha),
        grid=(b, s // rows),
        in_specs=[pl.BlockSpec((1, HEAD_PAIRS, rows, LANES), lambda i, j: (i, 0, j, 0)),
                  tile(D_INNER), tile(2 * D_MODEL), tile(D_MODEL), tile(PLE_DIM), tile(D_MODEL),
                  whole((ATT_OUT + D_INNER, D_MODEL)), whole((D_MODEL, D_MODEL)),
                  whole((PLE_DIM, D_MODEL)), whole((3, D_MODEL)), whole((1, D_MODEL)),
                  whole((1, D_MODEL))],
        out_specs=tile(D_MODEL),
        out_shape=jax.ShapeDtypeStruct((b, s, d), F32),
        compiler_params=pltpu.CompilerParams(
            dimension_semantics=("parallel", "parallel"), vmem_limit_bytes=VMEM_LIMIT),
        name="merge_out_ln",
    )(o_att, y_ssm, g_merge, g_ple, p_i, x, w_branch.astype(BF16), w_out.astype(BF16),
      w_ple.astype(BF16), b_gate, ln_g.reshape(1, d), ln_b.reshape(1, d))


def _layer(x, p_i, w_in, b_gate, conv_w, conv_b, dt_bias, a_log, d_skip, ssm_norm_w,
           w_branch, w_out, w_ple, ln_g, ln_b, bias, alpha):
    row_scale = jnp.ones((w_in.shape[1], 1), F32).at[:Q_END].set(LOG2E * HEAD_DIM ** -0.5)
    wt = (jnp.swapaxes(w_in, 0, 1) * row_scale).astype(BF16)
    group_w = HEADS_PER_GROUP * HEAD_DIM
    qkv = []
    for g, (_, dil) in enumerate(PATTERNS):
        wg = jnp.concatenate([wt[part + g * group_w:part + (g + 1) * group_w]
                              for part in (0, Q_END, K_END)], axis=0)
        qkv.append(_project(x, wg, dilation=dil, tiled_out=True, out_dtype=BF16, rows=QKV_ROWS,
                            name=f"proj_qkv{g}"))
    w_gates = jnp.concatenate(
        [wt[V_END:GATT_END], jnp.pad(wt[XBC_END:DT_END], ((0, LANES - SSM_HEADS), (0, 0))),
         wt[GMERGE_END:]], axis=0)
    gate_dt, g_ple = _project(x, w_gates, widths=(ATT_OUT + LANES, D_MODEL),
                              silu_widths=(ATT_OUT, 0), rows=QKV_ROWS, name="proj_gates")
    w_z_gm = jnp.concatenate([wt[GATT_END:Z_END], wt[DT_END:GMERGE_END]], axis=0)
    z_gate, g_merge = _project(x, w_z_gm, widths=(D_INNER, 2 * D_MODEL),
                               silu_widths=(D_INNER, 0), name="proj_z_gmerge")
    (xbc,) = _project(x, wt[Z_END:XBC_END], widths=(CONV_DIM,), name="proj_xbc")

    o_att = _attention(qkv, bias, gate_dt)
    y_ssm = _ssd(xbc, z_gate, gate_dt, conv_w, conv_b, dt_bias, a_log, d_skip, ssm_norm_w)
    return _output(o_att, y_ssm, g_merge, g_ple, p_i, x, w_branch, w_out, w_ple, b_gate,
                   ln_g, ln_b, alpha)


def kernel(x, p, w_in, b_gate, conv_w, conv_b, dt_bias, a_log, d_skip, ssm_norm_w, w_branch, w_out,
           w_ple, ln_g, ln_b, rel_bias):
    depth = w_in.shape[0]
    alpha = (2.0 * depth) ** 0.25
    bias = _bias_tiles(rel_bias)
    for i in range(depth):
        x = _layer(x, p[i], w_in[i], b_gate[i], conv_w[i], conv_b[i], dt_bias[i], a_log[i],
                   d_skip[i], ssm_norm_w[i], w_branch[i], w_out[i], w_ple[i], ln_g[i], ln_b[i],
                   bias, alpha)
    return x
```

```python
import functools
import math

import numpy as np
import jax
import jax.numpy as jnp
from jax import lax
from jax.experimental import pallas as pl
from jax.experimental.pallas import tpu as pltpu

F32 = jnp.float32
BF16 = jnp.bfloat16

LANES = 128
SUBLANES = 8
V7X_VMEM_BYTES = 64 * 1024 * 1024
VMEM_LIMIT = (V7X_VMEM_BYTES * 3) // 4

D_MODEL = 1024
HEAD_DIM = 64
HEADS_PER_GROUP = 12
PATTERNS = ((128, 1), (512, 4), (2048, 16))
N_GROUPS = 3
ATT_HEADS = N_GROUPS * HEADS_PER_GROUP
ATT_QKV = ATT_HEADS * HEAD_DIM
ATT_OUT = HEADS_PER_GROUP * HEAD_DIM
BLK = 128
NUM_BUCKETS = 32
MAX_DISTANCE = 2048
D_INNER = 2048
SSM_HEADS = 32
SSM_GROUPS = 4
D_STATE = 128
CONV_WIDTH = 4
CONV_DIM = D_INNER + 2 * SSM_GROUPS * D_STATE
CHUNK = 128
PLE_DIM = 256
LN_EPS = 1e-5
RMS_EPS = 1e-5
LOG2E = math.log2(math.e)
Q_END = ATT_QKV
K_END = Q_END + ATT_QKV
V_END = K_END + ATT_QKV
GATT_END = V_END + ATT_OUT
Z_END = GATT_END + D_INNER
XBC_END = Z_END + CONV_DIM
DT_END = XBC_END + SSM_HEADS
GMERGE_END = DT_END + 2 * D_MODEL

HEAD_PAIRS = HEADS_PER_GROUP // 2
GROUP_LANES = D_INNER // SSM_GROUPS
HEADS_PER_SSM_GROUP = SSM_HEADS // SSM_GROUPS
PROJ_ROWS = 512
QKV_ROWS = 1024
PERM_ROWS = 256
SSD_ROWS = CHUNK
SSD_BATCH = 4
OUT_ROWS = 512
OUT_SUBROWS = 256


def _sigmoid_t(v):
    return 0.5 * jnp.tanh(0.5 * v) + 0.5


def _split3(v):
    hi = v.astype(BF16)
    r1 = v - hi.astype(F32)
    mid = r1.astype(BF16)
    lo = (r1 - mid.astype(F32)).astype(BF16)
    return hi, mid, lo


def _dot(a, b):
    return jnp.dot(a, b, preferred_element_type=F32)


def _dot_nt(a, b):
    return lax.dot_general(a, b, (((1,), (1,)), ((), ())), preferred_element_type=F32)


def _proj_kernel(*refs, permute, tiled_out, silu_widths):
    if permute:
        x_ref, perm_ref, w_ref, *o_refs = refs
    else:
        x_ref, w_ref, *o_refs = refs
    lhs = x_ref[0].astype(BF16)
    if permute:
        lhs = jnp.concatenate(
            [_dot(perm_ref[...], lhs[t * PERM_ROWS:(t + 1) * PERM_ROWS, :]).astype(BF16)
             for t in range(lhs.shape[0] // PERM_ROWS)], axis=0)
    res = _dot_nt(lhs, w_ref[...])
    if tiled_out:
        (o_ref,) = o_refs
        _, n_tiles, n_classes, n_sub, sub, _ = o_ref.shape
        sub_rows = n_classes * sub
        for c in range(n_tiles):
            for t in range(n_sub):
                tile = res[t * sub_rows:(t + 1) * sub_rows, c * LANES:(c + 1) * LANES]
                o_ref[0, c, :, t] = tile.reshape(n_classes, sub, LANES).astype(o_ref.dtype)
    else:
        off = 0
        for o_ref, n_silu in zip(o_refs, silu_widths):
            width = o_ref.shape[-1]
            if n_silu:
                o_ref[0, :, :n_silu] = _silu(res[:, off:off + n_silu]).astype(o_ref.dtype)
            if n_silu < width:
                o_ref[0, :, n_silu:] = res[:, off + n_silu:off + width].astype(o_ref.dtype)
            off += width


def _class_permutation(dilation):
    sub = PERM_ROWS // dilation
    dst = np.arange(PERM_ROWS)
    src = dilation * (dst % sub) + dst // sub
    perm = np.zeros((PERM_ROWS, PERM_ROWS), np.float32)
    perm[dst, src] = 1.0
    return jnp.asarray(perm, BF16)


def _project(x, w, *, name, dilation=1, tiled_out=False, widths=None, silu_widths=None,
             out_dtype=F32, rows=PROJ_ROWS):
    b, s, d = x.shape
    n = w.shape[0]
    steps = s // rows
    permute = dilation > 1
    in_specs = [pl.BlockSpec((1, rows, d), lambda i, j: (i, j, 0))]
    args = [x]
    if permute:
        in_specs.append(pl.BlockSpec((PERM_ROWS, PERM_ROWS), lambda i, j: (0, 0)))
        args.append(_class_permutation(dilation))
    in_specs.append(pl.BlockSpec((n, d), lambda i, j: (0, 0)))
    args.append(w)
    if tiled_out:
        n_sub = rows // PERM_ROWS if permute else 1
        sub = rows // (n_sub * dilation)
        out_shape = jax.ShapeDtypeStruct((b, n // LANES, dilation, steps * n_sub, sub, LANES), out_dtype)
        o_spec = pl.BlockSpec((1, n // LANES, dilation, n_sub, sub, LANES),
                              lambda i, j: (i, 0, 0, j, 0, 0))
    else:
        assert not permute and sum(widths) == n
        out_shape = tuple(jax.ShapeDtypeStruct((b, s, wd), out_dtype) for wd in widths)
        o_spec = tuple(pl.BlockSpec((1, rows, wd), lambda i, j: (i, j, 0)) for wd in widths)
    out = pl.pallas_call(
        functools.partial(_proj_kernel, permute=permute, tiled_out=tiled_out,
                          silu_widths=silu_widths or (0,) * len(widths or ())),
        grid=(b, steps),
        in_specs=in_specs,
        out_specs=o_spec,
        out_shape=out_shape,
        compiler_params=pltpu.CompilerParams(
            dimension_semantics=("parallel", "parallel"), vmem_limit_bytes=VMEM_LIMIT),
        name=name,
    )(*args)
    return out.reshape(b, n // LANES, s, LANES) if tiled_out else out


def _bucket_tiles():
    qi = np.arange(BLK)[:, None]
    kj = np.arange(2 * BLK)[None, :]
    delta = qi + BLK - kj
    max_exact = NUM_BUCKETS // 2
    out = []
    for window, dil in PATTERNS:
        valid = (delta >= 0) & (delta <= window // dil)
        dist = np.maximum(delta, 0) * dil
        d_f = np.maximum(dist, 1).astype(np.float32)
        large = max_exact + (np.log(d_f / max_exact) / np.float32(math.log(MAX_DISTANCE / max_exact))
                             * (NUM_BUCKETS - max_exact)).astype(np.int32)
        large = np.minimum(large, NUM_BUCKETS - 1)
        bucket = np.where(dist < max_exact, dist, large)
        out.append(np.where(valid, bucket, -1).astype(np.int32))
    return np.stack(out)


def _bias_kernel(tbl_ref, bucket_ref, o_ref):
    g = pl.program_id(0)
    bucket = bucket_ref[0]
    o_ref[...] = jnp.full(o_ref.shape, -jnp.inf, F32)
    for bkt in range(NUM_BUCKETS):
        hit = bucket == bkt
        for h in range(HEADS_PER_GROUP):
            o_ref[0, h] = jnp.where(hit, LOG2E * tbl_ref[bkt, g * HEADS_PER_GROUP + h], o_ref[0, h])


def _bias_tiles(rel_bias):
    buckets = jnp.asarray(_bucket_tiles())
    return pl.pallas_call(
        _bias_kernel,
        grid=(N_GROUPS,),
        in_specs=[pl.BlockSpec(memory_space=pltpu.SMEM),
                  pl.BlockSpec((1, BLK, 2 * BLK), lambda g: (g, 0, 0))],
        out_specs=pl.BlockSpec((1, HEADS_PER_GROUP, BLK, 2 * BLK), lambda g: (g, 0, 0, 0)),
        out_shape=jax.ShapeDtypeStruct((N_GROUPS, HEADS_PER_GROUP, BLK, 2 * BLK), F32),
        name="bias_tiles",
    )(rel_bias, buckets)


ATT_PAIRS = 2
MERGE_GROUP = 1
ATT_DEPTH = 3


def _head_stack(t):
    lane = lax.broadcasted_iota(jnp.int32, t.shape, 1)
    zero = jnp.zeros_like(t)
    return jnp.concatenate([jnp.where(lane < HEAD_DIM, t, zero),
                            jnp.where(lane >= HEAD_DIM, t, zero)], axis=0)


def _attn_kernel(q1, k1, v1, q2, k2, v2, q3, k3, v3, bias_ref, gate_ref, o_ref,
                 m_scr, l_scr, acc_scr, out_scr, *, seq):
    left = lax.broadcasted_iota(jnp.int32, (BLK, LANES), 1) < HEAD_DIM
    ones_rows = jnp.ones((2 * BLK, LANES), BF16)
    r_idx = lax.broadcasted_iota(jnp.int32, (2 * BLK, LANES), 0)
    c_idx = lax.broadcasted_iota(jnp.int32, (2 * BLK, LANES), 1)
    ones_heads = jnp.where((r_idx < BLK) == (c_idx < HEAD_DIM), 1.0, 0.0).astype(BF16)

    merge_dil = PATTERNS[MERGE_GROUP][1]
    merge_sub = seq // merge_dil
    jobs = []
    for pp in range(ATT_PAIRS):
        for g in (2, 0, MERGE_GROUP):
            dil = PATTERNS[g][1]
            sub = seq // dil
            for r in range(dil):
                for n in range(sub // BLK):
                    if g == 0:
                        rows = (pl.ds(n * BLK, BLK),)
                    elif g == MERGE_GROUP:
                        rows = (pl.ds(r * sub + n * BLK, BLK),
                                pl.ds(n * BLK * dil + r, BLK, stride=dil))
                    else:
                        step = dil // merge_dil
                        rows = (pl.ds((r % merge_dil) * merge_sub + r // merge_dil, BLK, stride=step),)
                    jobs.append((pp, g, r * sub + n * BLK, n > 0, rows))
    group_refs = ((q1, k1, v1), (q2, k2, v2), (q3, k3, v3))

    def key_rows(ref, pp, row0, has_prev):
        if has_prev:
            return ref[0, pp, row0 - BLK:row0 + BLK, :]
        return ref[0, pp, row0:row0 + BLK, :]

    def scores_of(job):
        pp, g, row0, has_prev, _ = job
        q_ref, k_ref, _ = group_refs[g]
        q = q_ref[0, pp, row0:row0 + BLK, :]
        kk = key_rows(k_ref, pp, row0, has_prev)
        if has_prev:
            return _dot_nt(_head_stack(q), kk)
        return _dot_nt(q, _head_stack(kk))

    def softmax_of(job, s):
        pp, g, _, has_prev, _ = job
        es, ms = [], []
        for h in range(2):
            if has_prev:
                sh = s[h * BLK:(h + 1) * BLK, :] + bias_ref[g, 2 * pp + h]
            else:
                sh = s[:, h * BLK:(h + 1) * BLK] + bias_ref[g, 2 * pp + h, :, BLK:]
            mh = jnp.max(sh, axis=-1, keepdims=True)
            es.append(jnp.exp2(sh - mh).astype(BF16))
            ms.append(mh)
        return jnp.concatenate(es, axis=0 if has_prev else 1), jnp.where(left, ms[0], ms[1])

    def output_of(job, p, m_new):
        pp, g, row0, has_prev, rows = job
        vv = key_rows(group_refs[g][2], pp, row0, has_prev)
        if has_prev:
            pv = _dot(p, jnp.concatenate([vv, ones_rows], axis=1))
            acc = jnp.where(left, pv[:BLK, :LANES], pv[BLK:, :LANES])
            l_new = jnp.where(left, pv[:BLK, LANES:], pv[BLK:, LANES:])
        else:
            pv = _dot(p, jnp.concatenate([_head_stack(vv), ones_heads], axis=1))
            acc, l_new = pv[:, :LANES], pv[:, LANES:]
        if g != MERGE_GROUP:
            slot = 0 if g == 0 else 1
            m_scr[slot, pp, rows[0], :] = m_new
            l_scr[slot, pp, rows[0], :] = l_new
            acc_scr[slot, pp, rows[0], :] = acc
            return
        merged, natural = rows
        ms = [m_new, m_scr[0, pp, natural, :], m_scr[1, pp, merged, :]]
        ls = [l_new, l_scr[0, pp, natural, :], l_scr[1, pp, merged, :]]
        accs = [acc, acc_scr[0, pp, natural, :], acc_scr[1, pp, merged, :]]
        m_all = jnp.maximum(jnp.maximum(ms[0], ms[1]), ms[2])
        ws = [jnp.exp2(m - m_all) for m in ms]
        den = sum(w * l for w, l in zip(ws, ls))
        num = sum(w * a for w, a in zip(ws, accs))
        out_scr[pp, natural, :] = num / den

    pending = {u: scores_of(jobs[u]) for u in range(ATT_DEPTH)}
    for u, job in enumerate(jobs):
        p, m_new = softmax_of(job, pending.pop(u))
        if u + ATT_DEPTH < len(jobs):
            pending[u + ATT_DEPTH] = scores_of(jobs[u + ATT_DEPTH])
        output_of(job, p, m_new)

    for pp in range(ATT_PAIRS):
        gate = gate_ref[0, :, pp * LANES:(pp + 1) * LANES]
        o_ref[0, pp] = (out_scr[pp] * gate).astype(o_ref.dtype)


def _attention(qkv, bias, gate_dt):
    b, _, s, _ = qkv[0].shape
    steps = HEAD_PAIRS // ATT_PAIRS
    in_specs, args = [], []
    for arr in qkv:
        for part in range(3):
            in_specs.append(pl.BlockSpec((1, ATT_PAIRS, s, LANES),
                                         lambda i, j, part=part: (i, part * steps + j, 0, 0)))
            args.append(arr)
    in_specs.append(pl.BlockSpec((N_GROUPS, 2 * ATT_PAIRS, BLK, 2 * BLK), lambda i, j: (0, j, 0, 0)))
    in_specs.append(pl.BlockSpec((1, s, ATT_PAIRS * LANES), lambda i, j: (i, 0, j)))
    return pl.pallas_call(
        functools.partial(_attn_kernel, seq=s),
        grid=(b, steps),
        in_specs=in_specs,
        out_specs=pl.BlockSpec((1, ATT_PAIRS, s, LANES), lambda i, j: (i, j, 0, 0)),
        out_shape=jax.ShapeDtypeStruct((b, HEAD_PAIRS, s, LANES), BF16),
        scratch_shapes=[pltpu.VMEM((N_GROUPS - 1, ATT_PAIRS, s, LANES), F32)] * 3
        + [pltpu.VMEM((ATT_PAIRS, s, LANES), F32)],
        compiler_params=pltpu.CompilerParams(
            dimension_semantics=("parallel", "parallel"), vmem_limit_bytes=VMEM_LIMIT),
        name="dilated_attention",
    )(*args, bias, gate_dt)


TIME_STRIDE = 4
TIME_SPAN = TIME_STRIDE * SUBLANES
HALO = SUBLANES


def _strided_rows(ref, base):
    pieces = [ref[pl.ds(base + TIME_SPAN * grp + i, SUBLANES, stride=TIME_STRIDE), :]
              for grp in range(CHUNK // TIME_SPAN) for i in range(TIME_STRIDE)]
    return jnp.concatenate(pieces, axis=0)


def _strided_time_of(idx):
    q = idx // SUBLANES
    return TIME_SPAN * (q // TIME_STRIDE) + q % TIME_STRIDE + TIME_STRIDE * (idx % SUBLANES)


def _silu(v):
    h = 0.5 * v
    return h * jnp.tanh(h) + h


def _ssd_kernel(xbc_ref, z_ref, dt_ref, convw_ref, convb_ref, dtb_ref, alog_ref, dskip_ref,
                normw_ref, expand_ref, o_ref, state_scr, xpad_scr, y_scr):
    @pl.when(pl.program_id(1) == 0)
    def _():
        state_scr[...] = jnp.zeros_like(state_scr)
        xpad_scr[:, :, 0:HALO, :] = jnp.zeros((SSD_BATCH, CONV_DIM // LANES, HALO, LANES), F32)

    t_row = _strided_time_of(lax.broadcasted_iota(jnp.int32, (CHUNK, CHUNK), 0))
    t_col = _strided_time_of(lax.broadcasted_iota(jnp.int32, (CHUNK, CHUNK), 1))
    causal = t_row >= t_col
    tril = jnp.where(causal, 1.0, 0.0).astype(BF16)
    neg_rate = -LOG2E * jnp.exp(alog_ref[...])

    def one_chunk(k, carry):
        for bb in range(SSD_BATCH):
            _ssd_chunk(bb, pl.multiple_of(k * CHUNK, CHUNK), causal, tril, neg_rate,
                       xbc_ref, z_ref, dt_ref, convw_ref, convb_ref, dtb_ref, dskip_ref, normw_ref,
                       expand_ref, o_ref, state_scr, xpad_scr, y_scr)
        return carry

    lax.fori_loop(0, xbc_ref.shape[1] // CHUNK, one_chunk, 0)


def _ssd_chunk(bb, row0, causal, tril, neg_rate, xbc_ref, z_ref, dt_ref, convw_ref, convb_ref,
               dtb_ref, dskip_ref, normw_ref, expand_ref, o_ref, state_scr, xpad_scr, y_scr):
    x_slabs = D_INNER // LANES
    rows = pl.ds(row0, CHUNK)

    def conv_silu(j):
        lanes = slice(j * LANES, (j + 1) * LANES)
        xpad_scr[bb, j, HALO:HALO + CHUNK, :] = xbc_ref[bb, rows, lanes]
        conv = convb_ref[:, lanes]
        for k in range(CONV_WIDTH):
            tap = _strided_rows(xpad_scr.at[bb, j], HALO - (CONV_WIDTH - 1) + k)
            conv = conv + convw_ref[k:k + 1, lanes] * tap
        xpad_scr[bb, j, 0:HALO, :] = xbc_ref[bb, pl.ds(row0 + CHUNK - HALO, HALO), lanes]
        return _silu(conv)

    raw = _strided_rows(dt_ref.at[bb], row0) + dtb_ref[...]
    dt = jnp.maximum(raw, 0.0) + jnp.log1p(jnp.exp(-jnp.abs(raw)))
    a = dt * neg_rate
    hi, mid, lo = _split3(a)
    acs = _dot(tril, hi) + _dot(tril, mid) + _dot(tril, lo)
    src_t = (acs - jnp.log2(dt)).T
    acs_end = acs[CHUNK - 1:CHUNK, :]

    def two_pieces(v):
        hi2 = v.astype(BF16)
        return hi2, (v - hi2.astype(F32)).astype(BF16)

    decay_in_p = two_pieces(jnp.exp2(acs))
    w_out_p = two_pieces(dt * jnp.exp2(acs_end - acs))

    slabs_per_group = GROUP_LANES // LANES
    for g in range(SSM_GROUPS):
        gl = slice(g * GROUP_LANES, (g + 1) * GROUP_LANES)
        expand = expand_ref[:, gl]
        decay_in, w_out = (_dot(hi2, expand) + _dot(lo2, expand)
                           for hi2, lo2 in (decay_in_p, w_out_p))
        chunk_decay = decay_in[CHUNK - 1:CHUNK, :]
        xs = jnp.concatenate([conv_silu(g * slabs_per_group + j) for j in range(slabs_per_group)],
                             axis=1)
        bg32 = conv_silu(x_slabs + g)
        bg = bg32.astype(BF16)
        cg = conv_silu(x_slabs + SSM_GROUPS + g).astype(BF16)
        x_b = xs.astype(BF16)
        x_out = (xs * w_out).astype(BF16)
        cb = _dot_nt(cg, bg)
        prev = state_scr[bb, g]
        y_off = _dot(cg, prev.astype(BF16)) * decay_in
        state_scr[bb, g] = prev * chunk_decay + _dot(bg32.T.astype(BF16), x_out)
        for j in range(HEADS_PER_SSM_GROUP // 2):
            gs = []
            for hh in (g * HEADS_PER_SSM_GROUP + 2 * j, g * HEADS_PER_SSM_GROUP + 2 * j + 1):
                seg = acs[:, hh:hh + 1] - src_t[hh:hh + 1, :]
                lmat = jnp.exp2(jnp.where(causal, seg, -jnp.inf))
                gs.append((cb * lmat).astype(BF16))
            slab = g * slabs_per_group + j
            pair = slice(j * LANES, (j + 1) * LANES)
            y_diag = _dot(jnp.concatenate(gs, axis=1), _head_stack(x_b[:, pair]))
            y = (y_diag + y_off[:, pair]
                 + dskip_ref[:, slab * LANES:(slab + 1) * LANES] * xs[:, pair])
            for q in range(CHUNK // SUBLANES):
                start = TIME_SPAN * (q // TIME_STRIDE) + q % TIME_STRIDE
                y_scr[bb, slab, pl.ds(start, SUBLANES, stride=TIME_STRIDE), :] = (
                    y[q * SUBLANES:(q + 1) * SUBLANES, :])

        yg = jnp.concatenate([y_scr[bb, g * slabs_per_group + j] for j in range(slabs_per_group)],
                             axis=1)
        u = yg * z_ref[bb, rows, gl]
        ms = jnp.mean(jnp.square(u), axis=-1, keepdims=True)
        o_ref[bb, rows, gl] =(u * lax.rsqrt(ms + RMS_EPS) * normw_ref[:, gl]).astype(o_ref.dtype)


def _ssd(xbc, z, gate_dt, conv_w, conv_b, dt_bias, a_log, d_skip, norm_w):
    b, s, _ = xbc.shape
    assert b % SSD_BATCH == 0 and s % SSD_ROWS == 0
    pad = LANES - SSM_HEADS
    dtb = jnp.pad(dt_bias, (0, pad)).reshape(1, LANES)
    alog = jnp.pad(a_log, (0, pad)).reshape(1, LANES)
    dskip = jnp.repeat(d_skip, D_INNER // SSM_HEADS).reshape(1, D_INNER)
    expand = np.zeros((LANES, D_INNER), np.float32)
    expand[np.arange(D_INNER) // (D_INNER // SSM_HEADS), np.arange(D_INNER)] = 1.0
    expand = jnp.asarray(expand, BF16)
    dt_tile = (GATT_END - V_END) // LANES

    def whole(shape):
        return pl.BlockSpec(shape, lambda i, j: (0,) * len(shape))

    return pl.pallas_call(
        _ssd_kernel,
        grid=(b // SSD_BATCH, s // SSD_ROWS),
        in_specs=[pl.BlockSpec((SSD_BATCH, SSD_ROWS, CONV_DIM), lambda i, j: (i, j, 0)),
                  pl.BlockSpec((SSD_BATCH, SSD_ROWS, D_INNER), lambda i, j: (i, j, 0)),
                  pl.BlockSpec((SSD_BATCH, SSD_ROWS, LANES), lambda i, j: (i, j, dt_tile)),
                  whole((CONV_WIDTH, CONV_DIM)), whole((1, CONV_DIM)),
                  whole((1, LANES)), whole((1, LANES)), whole((1, D_INNER)),
                  whole((1, D_INNER)), whole((LANES, D_INNER))],
        out_specs=pl.BlockSpec((SSD_BATCH, SSD_ROWS, D_INNER), lambda i, j: (i, j, 0)),
        out_shape=jax.ShapeDtypeStruct((b, s, D_INNER), BF16),
        scratch_shapes=[pltpu.VMEM((SSD_BATCH, SSM_GROUPS, D_STATE, GROUP_LANES), F32),
                        pltpu.VMEM((SSD_BATCH, CONV_DIM // LANES, HALO + CHUNK, LANES), F32),
                        pltpu.VMEM((SSD_BATCH, D_INNER // LANES, CHUNK, LANES), F32)],
        compiler_params=pltpu.CompilerParams(
            dimension_semantics=("parallel", "arbitrary"), vmem_limit_bytes=VMEM_LIMIT),
        name="ssd_mixer",
    )(xbc, z, gate_dt, conv_w, conv_b.reshape(1, CONV_DIM), dtb, alog, dskip,
      norm_w.reshape(1, D_INNER), expand)


def _out_kernel(att_ref, ssm_ref, gm_ref, gp_ref, p_ref, x_ref, wb_ref, wo_ref, wp_ref,
                bg_ref, lng_ref, lnb_ref, o_ref, *, alpha):
    subs = [slice(t * OUT_SUBROWS, (t + 1) * OUT_SUBROWS) for t in range(OUT_ROWS // OUT_SUBROWS)]
    branch = []
    for rs in subs:
        o_att = jnp.concatenate([att_ref[0, j, rs, :] for j in range(HEAD_PAIRS)], axis=1)
        branch.append((_dot(o_att, wb_ref[:ATT_OUT, :]), _dot(ssm_ref[0, rs, :], wb_ref[ATT_OUT:, :]),
                       _dot(p_ref[0, rs, :].astype(BF16), wp_ref[...])))
    mixes = []
    for rs, (y_a, y_b, _) in zip(subs, branch):
        merged = (_sigmoid_t(gm_ref[0, rs, :D_MODEL] + bg_ref[0:1, :]) * y_a
                  + _sigmoid_t(gm_ref[0, rs, D_MODEL:] + bg_ref[1:2, :]) * y_b)
        mixes.append(_dot(merged.astype(BF16), wo_ref[...]))
    for rs, (_, _, pp), mix in zip(subs, branch, mixes):
        ple = _sigmoid_t(gp_ref[0, rs, :] + bg_ref[2:3, :]) * pp
        h = alpha * x_ref[0, rs, :] + mix + ple
        mu = jnp.mean(h, axis=-1, keepdims=True)
        var = jnp.mean(jnp.square(h - mu), axis=-1, keepdims=True)
        o_ref[0, rs, :] = (h - mu) * lax.rsqrt(var + LN_EPS) * lng_ref[...] + lnb_ref[...]


def _output(o_att, y_ssm, g_merge, g_ple, p_i, x, w_branch, w_out, w_ple, b_gate, ln_g, ln_b, alpha):
    b, s, d = x.shape
    rows = OUT_ROWS

    def whole(shape):
        return pl.BlockSpec(shape, lambda i, j: (0,) * len(shape), pipeline_mode=pl.Buffered(1))

    def tile(n):
        return pl.BlockSpec((1, rows, n), lambda i, j: (i, j, 0))

    return pl.pallas_call(
        functools.partial(_out_kernel, alpha=alpha),
        grid=(b, s // rows),
        in_specs=[pl.BlockSpec((1, HEAD_PAIRS, rows, LANES), lambda i, j: (i, 0, j, 0)),
                  tile(D_INNER), tile(2 * D_MODEL), tile(D_MODEL), tile(PLE_DIM), tile(D_MODEL),
                  whole((ATT_OUT + D_INNER, D_MODEL)), whole((D_MODEL, D_MODEL)),
                  whole((PLE_DIM, D_MODEL)), whole((3, D_MODEL)), whole((1, D_MODEL)),
                  whole((1, D_MODEL))],
        out_specs=tile(D_MODEL),
        out_shape=jax.ShapeDtypeStruct((b, s, d), F32),
        compiler_params=pltpu.CompilerParams(
            dimension_semantics=("parallel", "parallel"), vmem_limit_bytes=VMEM_LIMIT),
        name="merge_out_ln",
    )(o_att, y_ssm, g_merge, g_ple, p_i, x, w_branch.astype(BF16), w_out.astype(BF16),
      w_ple.astype(BF16), b_gate, ln_g.reshape(1, d), ln_b.reshape(1, d))


def _layer(x, p_i, w_in, b_gate, conv_w, conv_b, dt_bias, a_log, d_skip, ssm_norm_w,
           w_branch, w_out, w_ple, ln_g, ln_b, bias, alpha):
    row_scale = jnp.ones((w_in.shape[1], 1), F32).at[:Q_END].set(LOG2E * HEAD_DIM ** -0.5)
    wt = (jnp.swapaxes(w_in, 0, 1) * row_scale).astype(BF16)
    group_w = HEADS_PER_GROUP * HEAD_DIM
    qkv = []
    for g, (_, dil) in enumerate(PATTERNS):
        wg = jnp.concatenate([wt[part + g * group_w:part + (g + 1) * group_w]
                              for part in (0, Q_END, K_END)], axis=0)
        qkv.append(_project(x, wg, dilation=dil, tiled_out=True, out_dtype=BF16, rows=QKV_ROWS,
                            name=f"proj_qkv{g}"))
    w_gates = jnp.concatenate(
        [wt[V_END:GATT_END], jnp.pad(wt[XBC_END:DT_END], ((0, LANES - SSM_HEADS), (0, 0))),
         wt[GMERGE_END:]], axis=0)
    gate_dt, g_ple = _project(x, w_gates, widths=(ATT_OUT + LANES, D_MODEL),
                              silu_widths=(ATT_OUT, 0), rows=QKV_ROWS, name="proj_gates")
    w_z_gm = jnp.concatenate([wt[GATT_END:Z_END], wt[DT_END:GMERGE_END]], axis=0)
    z_gate, g_merge = _project(x, w_z_gm, widths=(D_INNER, 2 * D_MODEL),
                               silu_widths=(D_INNER, 0), name="proj_z_gmerge")
    (xbc,) = _project(x, wt[Z_END:XBC_END], widths=(CONV_DIM,), name="proj_xbc")

    o_att = _attention(qkv, bias, gate_dt)
    y_ssm = _ssd(xbc, z_gate, gate_dt, conv_w, conv_b, dt_bias, a_log, d_skip, ssm_norm_w)
    return _output(o_att, y_ssm, g_merge, g_ple, p_i, x, w_branch, w_out, w_ple, b_gate,
                   ln_g, ln_b, alpha)


def kernel(x, p, w_in, b_gate, conv_w, conv_b, dt_bias, a_log, d_skip, ssm_norm_w, w_branch, w_out,
           w_ple, ln_g, ln_b, rel_bias):
    depth = w_in.shape[0]
    alpha = (2.0 * depth) ** 0.25
    bias = _bias_tiles(rel_bias)
    for i in range(depth):
        x = _layer(x, p[i], w_in[i], b_gate[i], conv_w[i], conv_b[i], dt_bias[i], a_log[i],
                   d_skip[i], ssm_norm_w[i], w_branch[i], w_out[i], w_ple[i], ln_g[i], ln_b[i],
                   bias, alpha)
    return x
```

```python
import functools
import math

import numpy as np
import jax
import jax.numpy as jnp
from jax import lax
from jax.experimental import pallas as pl
from jax.experimental.pallas import tpu as pltpu

F32 = jnp.float32
BF16 = jnp.bfloat16

LANES = 128
SUBLANES = 8
V7X_VMEM_BYTES = 64 * 1024 * 1024
VMEM_LIMIT = (V7X_VMEM_BYTES * 3) // 4

D_MODEL = 1024
HEAD_DIM = 64
HEADS_PER_GROUP = 12
PATTERNS = ((128, 1), (512, 4), (2048, 16))
N_GROUPS = 3
ATT_HEADS = N_GROUPS * HEADS_PER_GROUP
ATT_QKV = ATT_HEADS * HEAD_DIM
ATT_OUT = HEADS_PER_GROUP * HEAD_DIM
BLK = 128
NUM_BUCKETS = 32
MAX_DISTANCE = 2048
D_INNER = 2048
SSM_HEADS = 32
SSM_GROUPS = 4
D_STATE = 128
CONV_WIDTH = 4
CONV_DIM = D_INNER + 2 * SSM_GROUPS * D_STATE
CHUNK = 128
PLE_DIM = 256
LN_EPS = 1e-5
RMS_EPS = 1e-5
LOG2E = math.log2(math.e)
Q_END = ATT_QKV
K_END = Q_END + ATT_QKV
V_END = K_END + ATT_QKV
GATT_END = V_END + ATT_OUT
Z_END = GATT_END + D_INNER
XBC_END = Z_END + CONV_DIM
DT_END = XBC_END + SSM_HEADS
GMERGE_END = DT_END + 2 * D_MODEL

HEAD_PAIRS = HEADS_PER_GROUP // 2
GROUP_LANES = D_INNER // SSM_GROUPS
HEADS_PER_SSM_GROUP = SSM_HEADS // SSM_GROUPS
PROJ_ROWS = 512
QKV_ROWS = 1024
PERM_ROWS = 256
SSD_ROWS = CHUNK
SSD_BATCH = 4
OUT_ROWS = 512
OUT_SUBROWS = 256


def _sigmoid_t(v):
    return 0.5 * jnp.tanh(0.5 * v) + 0.5


def _split3(v):
    hi = v.astype(BF16)
    r1 = v - hi.astype(F32)
    mid = r1.astype(BF16)
    lo = (r1 - mid.astype(F32)).astype(BF16)
    return hi, mid, lo


def _dot(a, b):
    return jnp.dot(a, b, preferred_element_type=F32)


def _dot_nt(a, b):
    return lax.dot_general(a, b, (((1,), (1,)), ((), ())), preferred_element_type=F32)


def _softplus(v):
    return jnp.maximum(v, 0.0) + jnp.log1p(jnp.exp(-jnp.abs(v)))


def _proj_kernel(*refs, permute, tiled_out, silu_widths, softplus_tail):
    tail_ref = None
    if permute:
        x_ref, perm_ref, w_ref, *o_refs = refs
    elif softplus_tail:
        x_ref, w_ref, tail_ref, *o_refs = refs
    else:
        x_ref, w_ref, *o_refs = refs
    lhs = x_ref[0].astype(BF16)
    if permute:
        lhs = jnp.concatenate(
            [_dot(perm_ref[...], lhs[t * PERM_ROWS:(t + 1) * PERM_ROWS, :]).astype(BF16)
             for t in range(lhs.shape[0] // PERM_ROWS)], axis=0)
    res = _dot_nt(lhs, w_ref[...])
    if tiled_out:
        (o_ref,) = o_refs
        _, n_tiles, n_classes, n_sub, sub, _ = o_ref.shape
        sub_rows = n_classes * sub
        for c in range(n_tiles):
            for t in range(n_sub):
                tile = res[t * sub_rows:(t + 1) * sub_rows, c * LANES:(c + 1) * LANES]
                o_ref[0, c, :, t] = tile.reshape(n_classes, sub, LANES).astype(o_ref.dtype)
    else:
        off = 0
        for o_ref, n_silu in zip(o_refs, silu_widths):
            width = o_ref.shape[-1]
            if n_silu:
                o_ref[0, :, :n_silu] = _silu(res[:, off:off + n_silu]).astype(o_ref.dtype)
            if n_silu < width:
                tail = res[:, off + n_silu:off + width]
                if tail_ref is not None and o_ref is o_refs[0]:
                    tail = _softplus(tail + tail_ref[...])
                o_ref[0, :, n_silu:] = tail.astype(o_ref.dtype)
            off += width


def _class_permutation(dilation):
    sub = PERM_ROWS // dilation
    dst = np.arange(PERM_ROWS)
    src = dilation * (dst % sub) + dst // sub
    perm = np.zeros((PERM_ROWS, PERM_ROWS), np.float32)
    perm[dst, src] = 1.0
    return jnp.asarray(perm, BF16)


def _project(x, w, *, name, dilation=1, tiled_out=False, widths=None, silu_widths=None,
             softplus_tail=None, out_dtype=F32, rows=PROJ_ROWS):
    b, s, d = x.shape
    n = w.shape[0]
    steps = s // rows
    permute = dilation > 1
    in_specs = [pl.BlockSpec((1, rows, d), lambda i, j: (i, j, 0))]
    args = [x]
    if permute:
        in_specs.append(pl.BlockSpec((PERM_ROWS, PERM_ROWS), lambda i, j: (0, 0)))
        args.append(_class_permutation(dilation))
    in_specs.append(pl.BlockSpec((n, d), lambda i, j: (0, 0)))
    args.append(w)
    if softplus_tail is not None:
        assert not permute and not tiled_out
        in_specs.append(pl.BlockSpec(softplus_tail.shape, lambda i, j: (0, 0)))
        args.append(softplus_tail)
    if tiled_out:
        n_sub = rows // PERM_ROWS if permute else 1
        sub = rows // (n_sub * dilation)
        out_shape = jax.ShapeDtypeStruct((b, n // LANES, dilation, steps * n_sub, sub, LANES), out_dtype)
        o_spec = pl.BlockSpec((1, n // LANES, dilation, n_sub, sub, LANES),
                              lambda i, j: (i, 0, 0, j, 0, 0))
    else:
        assert not permute and sum(widths) == n
        out_shape = tuple(jax.ShapeDtypeStruct((b, s, wd), out_dtype) for wd in widths)
        o_spec = tuple(pl.BlockSpec((1, rows, wd), lambda i, j: (i, j, 0)) for wd in widths)
    out = pl.pallas_call(
        functools.partial(_proj_kernel, permute=permute, tiled_out=tiled_out,
                          silu_widths=silu_widths or (0,) * len(widths or ()),
                          softplus_tail=softplus_tail is not None),
        grid=(b, steps),
        in_specs=in_specs,
        out_specs=o_spec,
        out_shape=out_shape,
        compiler_params=pltpu.CompilerParams(
            dimension_semantics=("parallel", "parallel"), vmem_limit_bytes=VMEM_LIMIT),
        name=name,
    )(*args)
    return out.reshape(b, n // LANES, s, LANES) if tiled_out else out


def _bucket_tiles():
    qi = np.arange(BLK)[:, None]
    kj = np.arange(2 * BLK)[None, :]
    delta = qi + BLK - kj
    max_exact = NUM_BUCKETS // 2
    out = []
    for window, dil in PATTERNS:
        valid = (delta >= 0) & (delta <= window // dil)
        dist = np.maximum(delta, 0) * dil
        d_f = np.maximum(dist, 1).astype(np.float32)
        large = max_exact + (np.log(d_f / max_exact) / np.float32(math.log(MAX_DISTANCE / max_exact))
                             * (NUM_BUCKETS - max_exact)).astype(np.int32)
        large = np.minimum(large, NUM_BUCKETS - 1)
        bucket = np.where(dist < max_exact, dist, large)
        out.append(np.where(valid, bucket, -1).astype(np.int32))
    return np.stack(out)


def _bias_kernel(tbl_ref, bucket_ref, o_ref):
    g = pl.program_id(0)
    bucket = bucket_ref[0]
    o_ref[...] = jnp.full(o_ref.shape, -jnp.inf, F32)
    for bkt in range(NUM_BUCKETS):
        hit = bucket == bkt
        for h in range(HEADS_PER_GROUP):
            o_ref[0, h] = jnp.where(hit, LOG2E * tbl_ref[bkt, g * HEADS_PER_GROUP + h], o_ref[0, h])


def _bias_tiles(rel_bias):
    buckets = jnp.asarray(_bucket_tiles())
    return pl.pallas_call(
        _bias_kernel,
        grid=(N_GROUPS,),
        in_specs=[pl.BlockSpec(memory_space=pltpu.SMEM),
                  pl.BlockSpec((1, BLK, 2 * BLK), lambda g: (g, 0, 0))],
        out_specs=pl.BlockSpec((1, HEADS_PER_GROUP, BLK, 2 * BLK), lambda g: (g, 0, 0, 0)),
        out_shape=jax.ShapeDtypeStruct((N_GROUPS, HEADS_PER_GROUP, BLK, 2 * BLK), F32),
        name="bias_tiles",
    )(rel_bias, buckets)


ATT_PAIRS = 2
MERGE_GROUP = 1
ATT_DEPTH = 3


def _head_stack(t):
    lane = lax.broadcasted_iota(jnp.int32, t.shape, 1)
    zero = jnp.zeros_like(t)
    return jnp.concatenate([jnp.where(lane < HEAD_DIM, t, zero),
                            jnp.where(lane >= HEAD_DIM, t, zero)], axis=0)


def _attn_kernel(q1, k1, v1, q2, k2, v2, q3, k3, v3, bias_ref, gate_ref, o_ref,
                 m_scr, l_scr, acc_scr, out_scr, *, seq):
    left = lax.broadcasted_iota(jnp.int32, (BLK, LANES), 1) < HEAD_DIM
    ones_rows = jnp.ones((2 * BLK, LANES), BF16)
    r_idx = lax.broadcasted_iota(jnp.int32, (2 * BLK, LANES), 0)
    c_idx = lax.broadcasted_iota(jnp.int32, (2 * BLK, LANES), 1)
    ones_heads = jnp.where((r_idx < BLK) == (c_idx < HEAD_DIM), 1.0, 0.0).astype(BF16)

    merge_dil = PATTERNS[MERGE_GROUP][1]
    merge_sub = seq // merge_dil
    jobs = []
    for pp in range(ATT_PAIRS):
        for g in (2, 0, MERGE_GROUP):
            dil = PATTERNS[g][1]
            sub = seq // dil
            for r in range(dil):
                for n in range(sub // BLK):
                    if g == 0:
                        rows = (pl.ds(n * BLK, BLK),)
                    elif g == MERGE_GROUP:
                        rows = (pl.ds(r * sub + n * BLK, BLK),
                                pl.ds(n * BLK * dil + r, BLK, stride=dil))
                    else:
                        step = dil // merge_dil
                        rows = (pl.ds((r % merge_dil) * merge_sub + r // merge_dil, BLK, stride=step),)
                    jobs.append((pp, g, r * sub + n * BLK, n > 0, rows))
    group_refs = ((q1, k1, v1), (q2, k2, v2), (q3, k3, v3))

    def key_rows(ref, pp, row0, has_prev):
        if has_prev:
            return ref[0, pp, row0 - BLK:row0 + BLK, :]
        return ref[0, pp, row0:row0 + BLK, :]

    def scores_of(job):
        pp, g, row0, has_prev, _ = job
        q_ref, k_ref, _ = group_refs[g]
        q = q_ref[0, pp, row0:row0 + BLK, :]
        kk = key_rows(k_ref, pp, row0, has_prev)
        if has_prev:
            return _dot_nt(_head_stack(q), kk)
        return _dot_nt(q, _head_stack(kk))

    def softmax_of(job, s):
        pp, g, _, has_prev, _ = job
        es, ms = [], []
        for h in range(2):
            if has_prev:
                sh = s[h * BLK:(h + 1) * BLK, :] + bias_ref[g, 2 * pp + h]
            else:
                sh = s[:, h * BLK:(h + 1) * BLK] + bias_ref[g, 2 * pp + h, :, BLK:]
            mh = jnp.max(sh, axis=-1, keepdims=True)
            es.append(jnp.exp2(sh - mh).astype(BF16))
            ms.append(mh)
        return jnp.concatenate(es, axis=0 if has_prev else 1), jnp.where(left, ms[0], ms[1])

    def output_of(job, p, m_new):
        pp, g, row0, has_prev, rows = job
        vv = key_rows(group_refs[g][2], pp, row0, has_prev)
        if has_prev:
            pv = _dot(p, jnp.concatenate([vv, ones_rows], axis=1))
            acc = jnp.where(left, pv[:BLK, :LANES], pv[BLK:, :LANES])
            l_new = jnp.where(left, pv[:BLK, LANES:], pv[BLK:, LANES:])
        else:
            pv = _dot(p, jnp.concatenate([_head_stack(vv), ones_heads], axis=1))
            acc, l_new = pv[:, :LANES], pv[:, LANES:]
        if g != MERGE_GROUP:
            slot = 0 if g == 0 else 1
            m_scr[slot, pp, rows[0], :] = m_new
            l_scr[slot, pp, rows[0], :] = l_new
            acc_scr[slot, pp, rows[0], :] = acc
            return
        merged, natural = rows
        ms = [m_new, m_scr[0, pp, natural, :], m_scr[1, pp, merged, :]]
        ls = [l_new, l_scr[0, pp, natural, :], l_scr[1, pp, merged, :]]
        accs = [acc, acc_scr[0, pp, natural, :], acc_scr[1, pp, merged, :]]
        m_all = jnp.maximum(jnp.maximum(ms[0], ms[1]), ms[2])
        ws = [jnp.exp2(m - m_all) for m in ms]
        den = sum(w * l for w, l in zip(ws, ls))
        num = sum(w * a for w, a in zip(ws, accs))
        out_scr[pp, natural, :] = num / den

    pending = {u: scores_of(jobs[u]) for u in range(ATT_DEPTH)}
    for u, job in enumerate(jobs):
        p, m_new = softmax_of(job, pending.pop(u))
        if u + ATT_DEPTH < len(jobs):
            pending[u + ATT_DEPTH] = scores_of(jobs[u + ATT_DEPTH])
        output_of(job, p, m_new)

    for pp in range(ATT_PAIRS):
        gate = gate_ref[0, :, pp * LANES:(pp + 1) * LANES]
        o_ref[0, pp] = (out_scr[pp] * gate).astype(o_ref.dtype)


def _attention(qkv, bias, gate_dt):
    b, _, s, _ = qkv[0].shape
    steps = HEAD_PAIRS // ATT_PAIRS
    in_specs, args = [], []
    for arr in qkv:
        for part in range(3):
            in_specs.append(pl.BlockSpec((1, ATT_PAIRS, s, LANES),
                                         lambda i, j, part=part: (i, part * steps + j, 0, 0)))
            args.append(arr)
    in_specs.append(pl.BlockSpec((N_GROUPS, 2 * ATT_PAIRS, BLK, 2 * BLK), lambda i, j: (0, j, 0, 0)))
    in_specs.append(pl.BlockSpec((1, s, ATT_PAIRS * LANES), lambda i, j: (i, 0, j)))
    return pl.pallas_call(
        functools.partial(_attn_kernel, seq=s),
        grid=(b, steps),
        in_specs=in_specs,
        out_specs=pl.BlockSpec((1, ATT_PAIRS, s, LANES), lambda i, j: (i, j, 0, 0)),
        out_shape=jax.ShapeDtypeStruct((b, HEAD_PAIRS, s, LANES), BF16),
        scratch_shapes=[pltpu.VMEM((N_GROUPS - 1, ATT_PAIRS, s, LANES), F32)] * 3
        + [pltpu.VMEM((ATT_PAIRS, s, LANES), F32)],
        compiler_params=pltpu.CompilerParams(
            dimension_semantics=("parallel", "parallel"), vmem_limit_bytes=VMEM_LIMIT),
        name="dilated_attention",
    )(*args, bias, gate_dt)


TIME_STRIDE = 4
TIME_SPAN = TIME_STRIDE * SUBLANES
HALO = SUBLANES


def _strided_rows(ref, base):
    pieces = [ref[pl.ds(base + TIME_SPAN * grp + i, SUBLANES, stride=TIME_STRIDE), :]
              for grp in range(CHUNK // TIME_SPAN) for i in range(TIME_STRIDE)]
    return jnp.concatenate(pieces, axis=0)


def _strided_time_of(idx):
    q = idx // SUBLANES
    return TIME_SPAN * (q // TIME_STRIDE) + q % TIME_STRIDE + TIME_STRIDE * (idx % SUBLANES)


def _silu(v):
    h = 0.5 * v
    return h * jnp.tanh(h) + h


def _ssd_kernel(xbc_ref, z_ref, dt_ref, convw_ref, convb_ref, alog_ref, dskip_ref,
                normw_ref, expand_ref, o_ref, state_scr, xpad_scr, y_scr):
    @pl.when(pl.program_id(1) == 0)
    def _():
        state_scr[...] = jnp.zeros_like(state_scr)
        xpad_scr[:, :, 0:HALO, :] = jnp.zeros((SSD_BATCH, CONV_DIM // LANES, HALO, LANES), F32)

    t_row = _strided_time_of(lax.broadcasted_iota(jnp.int32, (CHUNK, CHUNK), 0))
    t_col = _strided_time_of(lax.broadcasted_iota(jnp.int32, (CHUNK, CHUNK), 1))
    causal = t_row >= t_col
    tril = jnp.where(causal, 1.0, 0.0).astype(BF16)
    neg_rate = -LOG2E * jnp.exp(alog_ref[...])

    def one_chunk(k, carry):
        for bb in range(SSD_BATCH):
            _ssd_chunk(bb, pl.multiple_of(k * CHUNK, CHUNK), causal, tril, neg_rate,
                       xbc_ref, z_ref, dt_ref, convw_ref, convb_ref, dskip_ref, normw_ref,
                       expand_ref, o_ref, state_scr, xpad_scr, y_scr)
        return carry

    lax.fori_loop(0, xbc_ref.shape[1] // CHUNK, one_chunk, 0)


def _ssd_chunk(bb, row0, causal, tril, neg_rate, xbc_ref, z_ref, dt_ref, convw_ref, convb_ref,
               dskip_ref, normw_ref, expand_ref, o_ref, state_scr, xpad_scr, y_scr):
    x_slabs = D_INNER // LANES
    rows = pl.ds(row0, CHUNK)

    def conv_silu(j):
        lanes = slice(j * LANES, (j + 1) * LANES)
        xpad_scr[bb, j, HALO:HALO + CHUNK, :] = xbc_ref[bb, rows, lanes]
        conv = convb_ref[:, lanes]
        for k in range(CONV_WIDTH):
            tap = _strided_rows(xpad_scr.at[bb, j], HALO - (CONV_WIDTH - 1) + k)
            conv = conv + convw_ref[k:k + 1, lanes] * tap
        xpad_scr[bb, j, 0:HALO, :] = xbc_ref[bb, pl.ds(row0 + CHUNK - HALO, HALO), lanes]
        return _silu(conv)

    dt = _strided_rows(dt_ref.at[bb], row0)
    a = dt * neg_rate
    hi, mid, lo = _split3(a)
    acs = _dot(tril, hi) + _dot(tril, mid) + _dot(tril, lo)
    src_t = (acs - jnp.log2(dt)).T
    acs_end = acs[CHUNK - 1:CHUNK, :]

    def two_pieces(v):
        hi2 = v.astype(BF16)
        return hi2, (v - hi2.astype(F32)).astype(BF16)

    decay_in_p = two_pieces(jnp.exp2(acs))
    w_out_p = two_pieces(dt * jnp.exp2(acs_end - acs))

    slabs_per_group = GROUP_LANES // LANES
    for g in range(SSM_GROUPS):
        gl = slice(g * GROUP_LANES, (g + 1) * GROUP_LANES)
        expand = expand_ref[:, gl]
        decay_in, w_out = (_dot(hi2, expand) + _dot(lo2, expand)
                           for hi2, lo2 in (decay_in_p, w_out_p))
        chunk_decay = decay_in[CHUNK - 1:CHUNK, :]
        xs = jnp.concatenate([conv_silu(g * slabs_per_group + j) for j in range(slabs_per_group)],
                             axis=1)
        bg32 = conv_silu(x_slabs + g)
        bg = bg32.astype(BF16)
        cg = conv_silu(x_slabs + SSM_GROUPS + g).astype(BF16)
        x_b = xs.astype(BF16)
        x_out = (xs * w_out).astype(BF16)
        cb = _dot_nt(cg, bg)
        prev = state_scr[bb, g]
        y_off = _dot(cg, prev.astype(BF16)) * decay_in
        state_scr[bb, g] = prev * chunk_decay + _dot(bg32.T.astype(BF16), x_out)
        for j in range(HEADS_PER_SSM_GROUP // 2):
            gs = []
            for hh in (g * HEADS_PER_SSM_GROUP + 2 * j, g * HEADS_PER_SSM_GROUP + 2 * j + 1):
                seg = acs[:, hh:hh + 1] - src_t[hh:hh + 1, :]
                lmat = jnp.exp2(jnp.where(causal, seg, -jnp.inf))
                gs.append((cb * lmat).astype(BF16))
            slab = g * slabs_per_group + j
            pair = slice(j * LANES, (j + 1) * LANES)
            y_diag = _dot(jnp.concatenate(gs, axis=1), _head_stack(x_b[:, pair]))
            y = (y_diag + y_off[:, pair]
                 + dskip_ref[:, slab * LANES:(slab + 1) * LANES] * xs[:, pair])
            for q in range(CHUNK // SUBLANES):
                start = TIME_SPAN * (q // TIME_STRIDE) + q % TIME_STRIDE
                y_scr[bb, slab, pl.ds(start, SUBLANES, stride=TIME_STRIDE), :] = (
                    y[q * SUBLANES:(q + 1) * SUBLANES, :])

        yg = jnp.concatenate([y_scr[bb, g * slabs_per_group + j] for j in range(slabs_per_group)],
                             axis=1)
        u = yg * z_ref[bb, rows, gl]
        ms = jnp.mean(jnp.square(u), axis=-1, keepdims=True)
        o_ref[bb, rows, gl] =(u * lax.rsqrt(ms + RMS_EPS) * normw_ref[:, gl]).astype(o_ref.dtype)


def _ssd(xbc, z, gate_dt, conv_w, conv_b, a_log, d_skip, norm_w):
    b, s, _ = xbc.shape
    assert b % SSD_BATCH == 0 and s % SSD_ROWS == 0
    alog = jnp.pad(a_log, (0, LANES - SSM_HEADS)).reshape(1, LANES)
    dskip = jnp.repeat(d_skip, D_INNER // SSM_HEADS).reshape(1, D_INNER)
    expand = np.zeros((LANES, D_INNER), np.float32)
    expand[np.arange(D_INNER) // (D_INNER // SSM_HEADS), np.arange(D_INNER)] = 1.0
    expand = jnp.asarray(expand, BF16)
    dt_tile = (GATT_END - V_END) // LANES

    def whole(shape):
        return pl.BlockSpec(shape, lambda i, j: (0,) * len(shape))

    return pl.pallas_call(
        _ssd_kernel,
        grid=(b // SSD_BATCH, s // SSD_ROWS),
        in_specs=[pl.BlockSpec((SSD_BATCH, SSD_ROWS, CONV_DIM), lambda i, j: (i, j, 0)),
                  pl.BlockSpec((SSD_BATCH, SSD_ROWS, D_INNER), lambda i, j: (i, j, 0)),
                  pl.BlockSpec((SSD_BATCH, SSD_ROWS, LANES), lambda i, j: (i, j, dt_tile)),
                  whole((CONV_WIDTH, CONV_DIM)), whole((1, CONV_DIM)),
                  whole((1, LANES)), whole((1, D_INNER)),
                  whole((1, D_INNER)), whole((LANES, D_INNER))],
        out_specs=pl.BlockSpec((SSD_BATCH, SSD_ROWS, D_INNER), lambda i, j: (i, j, 0)),
        out_shape=jax.ShapeDtypeStruct((b, s, D_INNER), BF16),
        scratch_shapes=[pltpu.VMEM((SSD_BATCH, SSM_GROUPS, D_STATE, GROUP_LANES), F32),
                        pltpu.VMEM((SSD_BATCH, CONV_DIM // LANES, HALO + CHUNK, LANES), F32),
                        pltpu.VMEM((SSD_BATCH, D_INNER // LANES, CHUNK, LANES), F32)],
        compiler_params=pltpu.CompilerParams(
            dimension_semantics=("parallel", "arbitrary"), vmem_limit_bytes=VMEM_LIMIT),
        name="ssd_mixer",
    )(xbc, z, gate_dt, conv_w, conv_b.reshape(1, CONV_DIM), alog, dskip,
      norm_w.reshape(1, D_INNER), expand)


def _out_kernel(att_ref, ssm_ref, gm_ref, gp_ref, p_ref, x_ref, wb_ref, wo_ref, wp_ref,
                bg_ref, lng_ref, lnb_ref, o_ref, *, alpha):
    subs = [slice(t * OUT_SUBROWS, (t + 1) * OUT_SUBROWS) for t in range(OUT_ROWS // OUT_SUBROWS)]
    branch = []
    for rs in subs:
        o_att = jnp.concatenate([att_ref[0, j, rs, :] for j in range(HEAD_PAIRS)], axis=1)
        branch.append((_dot(o_att, wb_ref[:ATT_OUT, :]), _dot(ssm_ref[0, rs, :], wb_ref[ATT_OUT:, :]),
                       _dot(p_ref[0, rs, :].astype(BF16), wp_ref[...])))
    mixes = []
    for rs, (y_a, y_b, _) in zip(subs, branch):
        merged = (_sigmoid_t(gm_ref[0, rs, :D_MODEL] + bg_ref[0:1, :]) * y_a
                  + _sigmoid_t(gm_ref[0, rs, D_MODEL:] + bg_ref[1:2, :]) * y_b)
        mixes.append(_dot(merged.astype(BF16), wo_ref[...]))
    for rs, (_, _, pp), mix in zip(subs, branch, mixes):
        ple = _sigmoid_t(gp_ref[0, rs, :] + bg_ref[2:3, :]) * pp
        h = alpha * x_ref[0, rs, :] + mix + ple
        mu = jnp.mean(h, axis=-1, keepdims=True)
        var = jnp.mean(jnp.square(h - mu), axis=-1, keepdims=True)
        o_ref[0, rs, :] = (h - mu) * lax.rsqrt(var + LN_EPS) * lng_ref[...] + lnb_ref[...]


def _output(o_att, y_ssm, g_merge, g_ple, p_i, x, w_branch, w_out, w_ple, b_gate, ln_g, ln_b, alpha):
    b, s, d = x.shape
    rows = OUT_ROWS

    def whole(shape):
        return pl.BlockSpec(shape, lambda i, j: (0,) * len(shape), pipeline_mode=pl.Buffered(1))

    def tile(n):
        return pl.BlockSpec((1, rows, n), lambda i, j: (i, j, 0))

    return pl.pallas_call(
        functools.partial(_out_kernel, alpha=alpha),
        grid=(b, s // rows),
        in_specs=[pl.BlockSpec((1, HEAD_PAIRS, rows, LANES), lambda i, j: (i, 0, j, 0)),
                  tile(D_INNER), tile(2 * D_MODEL), tile(D_MODEL), tile(PLE_DIM), tile(D_MODEL),
                  whole((ATT_OUT + D_INNER, D_MODEL)), whole((D_MODEL, D_MODEL)),
                  whole((PLE_DIM, D_MODEL)), whole((3, D_MODEL)), whole((1, D_MODEL)),
                  whole((1, D_MODEL))],
        out_specs=tile(D_MODEL),
        out_shape=jax.ShapeDtypeStruct((b, s, d), F32),
        compiler_params=pltpu.CompilerParams(
            dimension_semantics=("parallel", "parallel"), vmem_limit_bytes=VMEM_LIMIT),
        name="merge_out_ln",
    )(o_att, y_ssm, g_merge, g_ple, p_i, x, w_branch.astype(BF16), w_out.astype(BF16),
      w_ple.astype(BF16), b_gate, ln_g.reshape(1, d), ln_b.reshape(1, d))


def _layer(x, p_i, w_in, b_gate, conv_w, conv_b, dt_bias, a_log, d_skip, ssm_norm_w,
           w_branch, w_out, w_ple, ln_g, ln_b, bias, alpha):
    row_scale = jnp.ones((w_in.shape[1], 1), F32).at[:Q_END].set(LOG2E * HEAD_DIM ** -0.5)
    wt = (jnp.swapaxes(w_in, 0, 1) * row_scale).astype(BF16)
    group_w = HEADS_PER_GROUP * HEAD_DIM
    qkv = []
    for g, (_, dil) in enumerate(PATTERNS):
        wg = jnp.concatenate([wt[part + g * group_w:part + (g + 1) * group_w]
                              for part in (0, Q_END, K_END)], axis=0)
        qkv.append(_project(x, wg, dilation=dil, tiled_out=True, out_dtype=BF16, rows=QKV_ROWS,
                            name=f"proj_qkv{g}"))
    w_gates = jnp.concatenate(
        [wt[V_END:GATT_END], jnp.pad(wt[XBC_END:DT_END], ((0, LANES - SSM_HEADS), (0, 0))),
         wt[GMERGE_END:]], axis=0)
    dt_bias_tile = jnp.pad(dt_bias, (0, LANES - SSM_HEADS)).reshape(1, LANES)
    gate_dt, g_ple = _project(x, w_gates, widths=(ATT_OUT + LANES, D_MODEL),
                              silu_widths=(ATT_OUT, 0), softplus_tail=dt_bias_tile,
                              rows=QKV_ROWS, name="proj_gates")
    w_z_gm = jnp.concatenate([wt[GATT_END:Z_END], wt[DT_END:GMERGE_END]], axis=0)
    z_gate, g_merge = _project(x, w_z_gm, widths=(D_INNER, 2 * D_MODEL),
                               silu_widths=(D_INNER, 0), name="proj_z_gmerge")
    (xbc,) = _project(x, wt[Z_END:XBC_END], widths=(CONV_DIM,), name="proj_xbc")

    o_att = _attention(qkv, bias, gate_dt)
    y_ssm = _ssd(xbc, z_gate, gate_dt, conv_w, conv_b, a_log, d_skip, ssm_norm_w)
    return _output(o_att, y_ssm, g_merge, g_ple, p_i, x, w_branch, w_out, w_ple, b_gate,
                   ln_g, ln_b, alpha)


def kernel(x, p, w_in, b_gate, conv_w, conv_b, dt_bias, a_log, d_skip, ssm_norm_w, w_branch, w_out,
           w_ple, ln_g, ln_b, rel_bias):
    depth = w_in.shape[0]
    alpha = (2.0 * depth) ** 0.25
    bias = _bias_tiles(rel_bias)
    for i in range(depth):
        x = _layer(x, p[i], w_in[i], b_gate[i], conv_w[i], conv_b[i], dt_bias[i], a_log[i],
                   d_skip[i], ssm_norm_w[i], w_branch[i], w_out[i], w_ple[i], ln_g[i], ln_b[i],
                   bias, alpha)
    return x
```

```python
import functools
import math

import numpy as np
import jax
import jax.numpy as jnp
from jax import lax
from jax.experimental import pallas as pl
from jax.experimental.pallas import tpu as pltpu

F32 = jnp.float32
BF16 = jnp.bfloat16

LANES = 128
SUBLANES = 8
V7X_VMEM_BYTES = 64 * 1024 * 1024
VMEM_LIMIT = (V7X_VMEM_BYTES * 3) // 4

D_MODEL = 1024
HEAD_DIM = 64
HEADS_PER_GROUP = 12
PATTERNS = ((128, 1), (512, 4), (2048, 16))
N_GROUPS = 3
ATT_HEADS = N_GROUPS * HEADS_PER_GROUP
ATT_QKV = ATT_HEADS * HEAD_DIM
ATT_OUT = HEADS_PER_GROUP * HEAD_DIM
BLK = 128
NUM_BUCKETS = 32
MAX_DISTANCE = 2048
D_INNER = 2048
SSM_HEADS = 32
SSM_GROUPS = 4
D_STATE = 128
CONV_WIDTH = 4
CONV_DIM = D_INNER + 2 * SSM_GROUPS * D_STATE
CHUNK = 128
PLE_DIM = 256
LN_EPS = 1e-5
RMS_EPS = 1e-5
LOG2E = math.log2(math.e)
Q_END = ATT_QKV
K_END = Q_END + ATT_QKV
V_END = K_END + ATT_QKV
GATT_END = V_END + ATT_OUT
Z_END = GATT_END + D_INNER
XBC_END = Z_END + CONV_DIM
DT_END = XBC_END + SSM_HEADS
GMERGE_END = DT_END + 2 * D_MODEL

HEAD_PAIRS = HEADS_PER_GROUP // 2
GROUP_LANES = D_INNER // SSM_GROUPS
HEADS_PER_SSM_GROUP = SSM_HEADS // SSM_GROUPS
PROJ_ROWS = 512
QKV_ROWS = 1024
PERM_ROWS = 256
SSD_ROWS = CHUNK
SSD_BATCH = 4
OUT_ROWS = 512
OUT_SUBROWS = 256


def _sigmoid_t(v):
    return 0.5 * jnp.tanh(0.5 * v) + 0.5


def _split3(v):
    hi = v.astype(BF16)
    r1 = v - hi.astype(F32)
    mid = r1.astype(BF16)
    lo = (r1 - mid.astype(F32)).astype(BF16)
    return hi, mid, lo


def _dot(a, b):
    return jnp.dot(a, b, preferred_element_type=F32)


def _dot_nt(a, b):
    return lax.dot_general(a, b, (((1,), (1,)), ((), ())), preferred_element_type=F32)


def _softplus(v):
    return jnp.maximum(v, 0.0) + jnp.log1p(jnp.exp(-jnp.abs(v)))


def _proj_kernel(*refs, permute, tiled_out, silu_widths, softplus_tail):
    tail_ref = None
    if permute:
        x_ref, perm_ref, w_ref, *o_refs = refs
    elif softplus_tail:
        x_ref, w_ref, tail_ref, *o_refs = refs
    else:
        x_ref, w_ref, *o_refs = refs
    lhs = x_ref[0].astype(BF16)
    if permute:
        lhs = jnp.concatenate(
            [_dot(perm_ref[...], lhs[t * PERM_ROWS:(t + 1) * PERM_ROWS, :]).astype(BF16)
             for t in range(lhs.shape[0] // PERM_ROWS)], axis=0)
    res = _dot_nt(lhs, w_ref[...])
    if tiled_out:
        (o_ref,) = o_refs
        _, n_tiles, n_classes, n_sub, sub, _ = o_ref.shape
        sub_rows = n_classes * sub
        for c in range(n_tiles):
            for t in range(n_sub):
                tile = res[t * sub_rows:(t + 1) * sub_rows, c * LANES:(c + 1) * LANES]
                o_ref[0, c, :, t] = tile.reshape(n_classes, sub, LANES).astype(o_ref.dtype)
    else:
        off = 0
        for o_ref, n_silu in zip(o_refs, silu_widths):
            width = o_ref.shape[-1]
            if n_silu:
                o_ref[0, :, :n_silu] = _silu(res[:, off:off + n_silu]).astype(o_ref.dtype)
            if n_silu < width:
                tail = res[:, off + n_silu:off + width]
                if tail_ref is not None and o_ref is o_refs[0]:
                    tail = _softplus(tail + tail_ref[...])
                o_ref[0, :, n_silu:] = tail.astype(o_ref.dtype)
            off += width


def _class_permutation(dilation):
    sub = PERM_ROWS // dilation
    dst = np.arange(PERM_ROWS)
    src = dilation * (dst % sub) + dst // sub
    perm = np.zeros((PERM_ROWS, PERM_ROWS), np.float32)
    perm[dst, src] = 1.0
    return jnp.asarray(perm, BF16)


def _project(x, w, *, name, dilation=1, tiled_out=False, widths=None, silu_widths=None,
             softplus_tail=None, out_dtype=F32, rows=PROJ_ROWS):
    b, s, d = x.shape
    n = w.shape[0]
    steps = s // rows
    permute = dilation > 1
    in_specs = [pl.BlockSpec((1, rows, d), lambda i, j: (i, j, 0))]
    args = [x]
    if permute:
        in_specs.append(pl.BlockSpec((PERM_ROWS, PERM_ROWS), lambda i, j: (0, 0)))
        args.append(_class_permutation(dilation))
    in_specs.append(pl.BlockSpec((n, d), lambda i, j: (0, 0)))
    args.append(w)
    if softplus_tail is not None:
        assert not permute and not tiled_out
        in_specs.append(pl.BlockSpec(softplus_tail.shape, lambda i, j: (0, 0)))
        args.append(softplus_tail)
    if tiled_out:
        n_sub = rows // PERM_ROWS if permute else 1
        sub = rows // (n_sub * dilation)
        out_shape = jax.ShapeDtypeStruct((b, n // LANES, dilation, steps * n_sub, sub, LANES), out_dtype)
        o_spec = pl.BlockSpec((1, n // LANES, dilation, n_sub, sub, LANES),
                              lambda i, j: (i, 0, 0, j, 0, 0))
    else:
        assert not permute and sum(widths) == n
        out_shape = tuple(jax.ShapeDtypeStruct((b, s, wd), out_dtype) for wd in widths)
        o_spec = tuple(pl.BlockSpec((1, rows, wd), lambda i, j: (i, j, 0)) for wd in widths)
    out = pl.pallas_call(
        functools.partial(_proj_kernel, permute=permute, tiled_out=tiled_out,
                          silu_widths=silu_widths or (0,) * len(widths or ()),
                          softplus_tail=softplus_tail is not None),
        grid=(b, steps),
        in_specs=in_specs,
        out_specs=o_spec,
        out_shape=out_shape,
        compiler_params=pltpu.CompilerParams(
            dimension_semantics=("parallel", "parallel"), vmem_limit_bytes=VMEM_LIMIT),
        name=name,
    )(*args)
    return out.reshape(b, n // LANES, s, LANES) if tiled_out else out


def _bucket_tiles():
    qi = np.arange(BLK)[:, None]
    kj = np.arange(2 * BLK)[None, :]
    delta = qi + BLK - kj
    max_exact = NUM_BUCKETS // 2
    out = []
    for window, dil in PATTERNS:
        valid = (delta >= 0) & (delta <= window // dil)
        dist = np.maximum(delta, 0) * dil
        d_f = np.maximum(dist, 1).astype(np.float32)
        large = max_exact + (np.log(d_f / max_exact) / np.float32(math.log(MAX_DISTANCE / max_exact))
                             * (NUM_BUCKETS - max_exact)).astype(np.int32)
        large = np.minimum(large, NUM_BUCKETS - 1)
        bucket = np.where(dist < max_exact, dist, large)
        out.append(np.where(valid, bucket, -1).astype(np.int32))
    return np.stack(out)


def _bias_kernel(tbl_ref, bucket_ref, o_ref):
    g = pl.program_id(0)
    bucket = bucket_ref[0]
    o_ref[...] = jnp.full(o_ref.shape, -jnp.inf, F32)
    for bkt in range(NUM_BUCKETS):
        hit = bucket == bkt
        for h in range(HEADS_PER_GROUP):
            o_ref[0, h] = jnp.where(hit, LOG2E * tbl_ref[bkt, g * HEADS_PER_GROUP + h], o_ref[0, h])


def _bias_tiles(rel_bias):
    buckets = jnp.asarray(_bucket_tiles())
    return pl.pallas_call(
        _bias_kernel,
        grid=(N_GROUPS,),
        in_specs=[pl.BlockSpec(memory_space=pltpu.SMEM),
                  pl.BlockSpec((1, BLK, 2 * BLK), lambda g: (g, 0, 0))],
        out_specs=pl.BlockSpec((1, HEADS_PER_GROUP, BLK, 2 * BLK), lambda g: (g, 0, 0, 0)),
        out_shape=jax.ShapeDtypeStruct((N_GROUPS, HEADS_PER_GROUP, BLK, 2 * BLK), F32),
        name="bias_tiles",
    )(rel_bias, buckets)


ATT_PAIRS = 2
MERGE_GROUP = 1
ATT_DEPTH = 3


def _head_stack(t):
    lane = lax.broadcasted_iota(jnp.int32, t.shape, 1)
    zero = jnp.zeros_like(t)
    return jnp.concatenate([jnp.where(lane < HEAD_DIM, t, zero),
                            jnp.where(lane >= HEAD_DIM, t, zero)], axis=0)


def _attn_kernel(q1, k1, v1, q2, k2, v2, q3, k3, v3, bias_ref, gate_ref, o_ref,
                 m_scr, l_scr, acc_scr, out_scr, *, seq):
    left = lax.broadcasted_iota(jnp.int32, (BLK, LANES), 1) < HEAD_DIM
    ones_rows = jnp.ones((2 * BLK, LANES), BF16)
    r_idx = lax.broadcasted_iota(jnp.int32, (2 * BLK, LANES), 0)
    c_idx = lax.broadcasted_iota(jnp.int32, (2 * BLK, LANES), 1)
    ones_heads = jnp.where((r_idx < BLK) == (c_idx < HEAD_DIM), 1.0, 0.0).astype(BF16)

    merge_dil = PATTERNS[MERGE_GROUP][1]
    merge_sub = seq // merge_dil
    jobs = []
    for pp in range(ATT_PAIRS):
        for g in (2, 0, MERGE_GROUP):
            dil = PATTERNS[g][1]
            sub = seq // dil
            for r in range(dil):
                for n in range(sub // BLK):
                    if g == 0:
                        rows = (pl.ds(n * BLK, BLK),)
                    elif g == MERGE_GROUP:
                        rows = (pl.ds(r * sub + n * BLK, BLK),
                                pl.ds(n * BLK * dil + r, BLK, stride=dil))
                    else:
                        step = dil // merge_dil
                        rows = (pl.ds((r % merge_dil) * merge_sub + r // merge_dil, BLK, stride=step),)
                    jobs.append((pp, g, r * sub + n * BLK, n > 0, rows))
    group_refs = ((q1, k1, v1), (q2, k2, v2), (q3, k3, v3))

    def key_rows(ref, pp, row0, has_prev):
        if has_prev:
            return ref[0, pp, row0 - BLK:row0 + BLK, :]
        return ref[0, pp, row0:row0 + BLK, :]

    def scores_of(job):
        pp, g, row0, has_prev, _ = job
        q_ref, k_ref, _ = group_refs[g]
        q = q_ref[0, pp, row0:row0 + BLK, :]
        kk = key_rows(k_ref, pp, row0, has_prev)
        if has_prev:
            return _dot_nt(_head_stack(q), kk)
        return _dot_nt(q, _head_stack(kk))

    def softmax_of(job, s):
        pp, g, _, has_prev, _ = job
        es, ms = [], []
        for h in range(2):
            if has_prev:
                sh = s[h * BLK:(h + 1) * BLK, :] + bias_ref[g, 2 * pp + h]
            else:
                sh = s[:, h * BLK:(h + 1) * BLK] + bias_ref[g, 2 * pp + h, :, BLK:]
            mh = jnp.max(sh, axis=-1, keepdims=True)
            es.append(jnp.exp2(sh - mh).astype(BF16))
            ms.append(mh)
        return jnp.concatenate(es, axis=0 if has_prev else 1), jnp.where(left, ms[0], ms[1])

    def output_of(job, p, m_new):
        pp, g, row0, has_prev, rows = job
        vv = key_rows(group_refs[g][2], pp, row0, has_prev)
        if has_prev:
            pv = _dot(p, jnp.concatenate([vv, ones_rows], axis=1))
            acc = jnp.where(left, pv[:BLK, :LANES], pv[BLK:, :LANES])
            l_new = jnp.where(left, pv[:BLK, LANES:], pv[BLK:, LANES:])
        else:
            pv = _dot(p, jnp.concatenate([_head_stack(vv), ones_heads], axis=1))
            acc, l_new = pv[:, :LANES], pv[:, LANES:]
        if g != MERGE_GROUP:
            slot = 0 if g == 0 else 1
            m_scr[slot, pp, rows[0], :] = m_new
            l_scr[slot, pp, rows[0], :] = l_new
            acc_scr[slot, pp, rows[0], :] = acc
            return
        merged, natural = rows
        ms = [m_new, m_scr[0, pp, natural, :], m_scr[1, pp, merged, :]]
        ls = [l_new, l_scr[0, pp, natural, :], l_scr[1, pp, merged, :]]
        accs = [acc, acc_scr[0, pp, natural, :], acc_scr[1, pp, merged, :]]
        m_all = jnp.maximum(jnp.maximum(ms[0], ms[1]), ms[2])
        ws = [jnp.exp2(m - m_all) for m in ms]
        den = sum(w * l for w, l in zip(ws, ls))
        num = sum(w * a for w, a in zip(ws, accs))
        out_scr[pp, natural, :] = num / den

    pending = {u: scores_of(jobs[u]) for u in range(ATT_DEPTH)}
    for u, job in enumerate(jobs):
        p, m_new = softmax_of(job, pending.pop(u))
        if u + ATT_DEPTH < len(jobs):
            pending[u + ATT_DEPTH] = scores_of(jobs[u + ATT_DEPTH])
        output_of(job, p, m_new)

    for pp in range(ATT_PAIRS):
        gate = gate_ref[0, :, pp * LANES:(pp + 1) * LANES]
        o_ref[0, pp] = (out_scr[pp] * gate).astype(o_ref.dtype)


def _attention(qkv, bias, gate_dt):
    b, _, s, _ = qkv[0].shape
    steps = HEAD_PAIRS // ATT_PAIRS
    in_specs, args = [], []
    for arr in qkv:
        for part in range(3):
            in_specs.append(pl.BlockSpec((1, ATT_PAIRS, s, LANES),
                                         lambda i, j, part=part: (i, part * steps + j, 0, 0)))
            args.append(arr)
    in_specs.append(pl.BlockSpec((N_GROUPS, 2 * ATT_PAIRS, BLK, 2 * BLK), lambda i, j: (0, j, 0, 0)))
    in_specs.append(pl.BlockSpec((1, s, ATT_PAIRS * LANES), lambda i, j: (i, 0, j)))
    return pl.pallas_call(
        functools.partial(_attn_kernel, seq=s),
        grid=(b, steps),
        in_specs=in_specs,
        out_specs=pl.BlockSpec((1, ATT_PAIRS, s, LANES), lambda i, j: (i, j, 0, 0)),
        out_shape=jax.ShapeDtypeStruct((b, HEAD_PAIRS, s, LANES), BF16),
        scratch_shapes=[pltpu.VMEM((N_GROUPS - 1, ATT_PAIRS, s, LANES), F32)] * 3
        + [pltpu.VMEM((ATT_PAIRS, s, LANES), F32)],
        compiler_params=pltpu.CompilerParams(
            dimension_semantics=("parallel", "parallel"), vmem_limit_bytes=VMEM_LIMIT),
        name="dilated_attention",
    )(*args, bias, gate_dt)


TIME_STRIDE = 4
TIME_SPAN = TIME_STRIDE * SUBLANES
HALO = SUBLANES


def _strided_rows(ref, base):
    pieces = [ref[pl.ds(base + TIME_SPAN * grp + i, SUBLANES, stride=TIME_STRIDE), :]
              for grp in range(CHUNK // TIME_SPAN) for i in range(TIME_STRIDE)]
    return jnp.concatenate(pieces, axis=0)


def _strided_time_of(idx):
    q = idx // SUBLANES
    return TIME_SPAN * (q // TIME_STRIDE) + q % TIME_STRIDE + TIME_STRIDE * (idx % SUBLANES)


def _silu(v):
    h = 0.5 * v
    return h * jnp.tanh(h) + h


def _ssd_kernel(xbc_ref, z_ref, dt_ref, convw_ref, convb_ref, alog_ref, dskip_ref,
                normw_ref, expand_ref, o_ref, state_scr, xpad_scr, y_scr):
    @pl.when(pl.program_id(1) == 0)
    def _():
        state_scr[...] = jnp.zeros_like(state_scr)
        xpad_scr[:, :, 0:HALO, :] = jnp.zeros((SSD_BATCH, CONV_DIM // LANES, HALO, LANES), F32)

    t_row = _strided_time_of(lax.broadcasted_iota(jnp.int32, (CHUNK, CHUNK), 0))
    t_col = _strided_time_of(lax.broadcasted_iota(jnp.int32, (CHUNK, CHUNK), 1))
    causal = t_row >= t_col
    tril = jnp.where(causal, 1.0, 0.0).astype(BF16)
    neg_rate = -LOG2E * jnp.exp(alog_ref[...])

    def one_chunk(k, carry):
        for bb in range(SSD_BATCH):
            _ssd_chunk(bb, pl.multiple_of(k * CHUNK, CHUNK), causal, tril, neg_rate,
                       xbc_ref, z_ref, dt_ref, convw_ref, convb_ref, dskip_ref, normw_ref,
                       expand_ref, o_ref, state_scr, xpad_scr, y_scr)
        return carry

    lax.fori_loop(0, xbc_ref.shape[1] // CHUNK, one_chunk, 0)


def _ssd_chunk(bb, row0, causal, tril, neg_rate, xbc_ref, z_ref, dt_ref, convw_ref, convb_ref,
               dskip_ref, normw_ref, expand_ref, o_ref, state_scr, xpad_scr, y_scr):
    x_slabs = D_INNER // LANES
    rows = pl.ds(row0, CHUNK)

    def conv_silu(j):
        lanes = slice(j * LANES, (j + 1) * LANES)
        xpad_scr[bb, j, HALO:HALO + CHUNK, :] = xbc_ref[bb, rows, lanes]
        conv = convb_ref[:, lanes]
        for k in range(CONV_WIDTH):
            tap = _strided_rows(xpad_scr.at[bb, j], HALO - (CONV_WIDTH - 1) + k)
            conv = conv + convw_ref[k:k + 1, lanes] * tap
        xpad_scr[bb, j, 0:HALO, :] = xbc_ref[bb, pl.ds(row0 + CHUNK - HALO, HALO), lanes]
        return _silu(conv)

    dt = _strided_rows(dt_ref.at[bb], row0)
    a = dt * neg_rate
    hi, mid, lo = _split3(a)
    acs = _dot(tril, hi) + _dot(tril, mid) + _dot(tril, lo)
    src_t = (acs - jnp.log2(dt)).T
    acs_end = acs[CHUNK - 1:CHUNK, :]

    def two_pieces(v):
        hi2 = v.astype(BF16)
        return hi2, (v - hi2.astype(F32)).astype(BF16)

    decay_in_p = two_pieces(jnp.exp2(acs))
    w_out_p = two_pieces(dt * jnp.exp2(acs_end - acs))

    slabs_per_group = GROUP_LANES // LANES
    for g in range(SSM_GROUPS):
        gl = slice(g * GROUP_LANES, (g + 1) * GROUP_LANES)
        expand = expand_ref[:, gl]
        decay_in, w_out = (_dot(hi2, expand) + _dot(lo2, expand)
                           for hi2, lo2 in (decay_in_p, w_out_p))
        chunk_decay = decay_in[CHUNK - 1:CHUNK, :]
        xs = jnp.concatenate([conv_silu(g * slabs_per_group + j) for j in range(slabs_per_group)],
                             axis=1)
        bg32 = conv_silu(x_slabs + g)
        bg = bg32.astype(BF16)
        cg = conv_silu(x_slabs + SSM_GROUPS + g).astype(BF16)
        x_b = xs.astype(BF16)
        x_out = x_b * w_out.astype(BF16)
        cb = _dot_nt(cg, bg).astype(BF16)
        prev = state_scr[bb, g]
        y_off = _dot(cg, prev.astype(BF16)) * decay_in
        state_scr[bb, g] = prev * chunk_decay + _dot(bg32.T.astype(BF16), x_out)
        for j in range(HEADS_PER_SSM_GROUP // 2):
            gs = []
            for hh in (g * HEADS_PER_SSM_GROUP + 2 * j, g * HEADS_PER_SSM_GROUP + 2 * j + 1):
                seg = acs[:, hh:hh + 1] - src_t[hh:hh + 1, :]
                lmat = jnp.exp2(jnp.where(causal, seg, -jnp.inf))
                gs.append(cb * lmat.astype(BF16))
            slab = g * slabs_per_group + j
            pair = slice(j * LANES, (j + 1) * LANES)
            y_diag = _dot(jnp.concatenate(gs, axis=1), _head_stack(x_b[:, pair]))
            y = (y_diag + y_off[:, pair]
                 + dskip_ref[:, slab * LANES:(slab + 1) * LANES] * xs[:, pair])
            for q in range(CHUNK // SUBLANES):
                start = TIME_SPAN * (q // TIME_STRIDE) + q % TIME_STRIDE
                y_scr[bb, slab, pl.ds(start, SUBLANES, stride=TIME_STRIDE), :] = (
                    y[q * SUBLANES:(q + 1) * SUBLANES, :])

        yg = jnp.concatenate([y_scr[bb, g * slabs_per_group + j] for j in range(slabs_per_group)],
                             axis=1)
        u = yg * z_ref[bb, rows, gl]
        ms = jnp.mean(jnp.square(u), axis=-1, keepdims=True)
        o_ref[bb, rows, gl] =(u * lax.rsqrt(ms + RMS_EPS) * normw_ref[:, gl]).astype(o_ref.dtype)


def _ssd(xbc, z, gate_dt, conv_w, conv_b, a_log, d_skip, norm_w):
    b, s, _ = xbc.shape
    assert b % SSD_BATCH == 0 and s % SSD_ROWS == 0
    alog = jnp.pad(a_log, (0, LANES - SSM_HEADS)).reshape(1, LANES)
    dskip = jnp.repeat(d_skip, D_INNER // SSM_HEADS).reshape(1, D_INNER)
    expand = np.zeros((LANES, D_INNER), np.float32)
    expand[np.arange(D_INNER) // (D_INNER // SSM_HEADS), np.arange(D_INNER)] = 1.0
    expand = jnp.asarray(expand, BF16)
    dt_tile = (GATT_END - V_END) // LANES

    def whole(shape):
        return pl.BlockSpec(shape, lambda i, j: (0,) * len(shape))

    return pl.pallas_call(
        _ssd_kernel,
        grid=(b // SSD_BATCH, s // SSD_ROWS),
        in_specs=[pl.BlockSpec((SSD_BATCH, SSD_ROWS, CONV_DIM), lambda i, j: (i, j, 0)),
                  pl.BlockSpec((SSD_BATCH, SSD_ROWS, D_INNER), lambda i, j: (i, j, 0)),
                  pl.BlockSpec((SSD_BATCH, SSD_ROWS, LANES), lambda i, j: (i, j, dt_tile)),
                  whole((CONV_WIDTH, CONV_DIM)), whole((1, CONV_DIM)),
                  whole((1, LANES)), whole((1, D_INNER)),
                  whole((1, D_INNER)), whole((LANES, D_INNER))],
        out_specs=pl.BlockSpec((SSD_BATCH, SSD_ROWS, D_INNER), lambda i, j: (i, j, 0)),
        out_shape=jax.ShapeDtypeStruct((b, s, D_INNER), BF16),
        scratch_shapes=[pltpu.VMEM((SSD_BATCH, SSM_GROUPS, D_STATE, GROUP_LANES), F32),
                        pltpu.VMEM((SSD_BATCH, CONV_DIM // LANES, HALO + CHUNK, LANES), F32),
                        pltpu.VMEM((SSD_BATCH, D_INNER // LANES, CHUNK, LANES), F32)],
        compiler_params=pltpu.CompilerParams(
            dimension_semantics=("parallel", "arbitrary"), vmem_limit_bytes=VMEM_LIMIT),
        name="ssd_mixer",
    )(xbc, z, gate_dt, conv_w, conv_b.reshape(1, CONV_DIM), alog, dskip,
      norm_w.reshape(1, D_INNER), expand)


def _out_kernel(att_ref, ssm_ref, gm_ref, gp_ref, p_ref, x_ref, wb_ref, wo_ref, wp_ref,
                bg_ref, lng_ref, lnb_ref, o_ref, *, alpha):
    subs = [slice(t * OUT_SUBROWS, (t + 1) * OUT_SUBROWS) for t in range(OUT_ROWS // OUT_SUBROWS)]
    branch = []
    for rs in subs:
        o_att = jnp.concatenate([att_ref[0, j, rs, :] for j in range(HEAD_PAIRS)], axis=1)
        branch.append((_dot(o_att, wb_ref[:ATT_OUT, :]), _dot(ssm_ref[0, rs, :], wb_ref[ATT_OUT:, :]),
                       _dot(p_ref[0, rs, :].astype(BF16), wp_ref[...])))
    mixes = []
    for rs, (y_a, y_b, _) in zip(subs, branch):
        merged = (_sigmoid_t(gm_ref[0, rs, :D_MODEL] + bg_ref[0:1, :]) * y_a
                  + _sigmoid_t(gm_ref[0, rs, D_MODEL:] + bg_ref[1:2, :]) * y_b)
        mixes.append(_dot(merged.astype(BF16), wo_ref[...]))
    for rs, (_, _, pp), mix in zip(subs, branch, mixes):
        ple = _sigmoid_t(gp_ref[0, rs, :] + bg_ref[2:3, :]) * pp
        h = alpha * x_ref[0, rs, :] + mix + ple
        mu = jnp.mean(h, axis=-1, keepdims=True)
        var = jnp.mean(jnp.square(h - mu), axis=-1, keepdims=True)
        o_ref[0, rs, :] = (h - mu) * lax.rsqrt(var + LN_EPS) * lng_ref[...] + lnb_ref[...]


def _output(o_att, y_ssm, g_merge, g_ple, p_i, x, w_branch, w_out, w_ple, b_gate, ln_g, ln_b, alpha):
    b, s, d = x.shape
    rows = OUT_ROWS

    def whole(shape):
        return pl.BlockSpec(shape, lambda i, j: (0,) * len(shape), pipeline_mode=pl.Buffered(1))

    def tile(n):
        return pl.BlockSpec((1, rows, n), lambda i, j: (i, j, 0))

    return pl.pallas_call(
        functools.partial(_out_kernel, alpha=alpha),
        grid=(b, s // rows),
        in_specs=[pl.BlockSpec((1, HEAD_PAIRS, rows, LANES), lambda i, j: (i, 0, j, 0)),
                  tile(D_INNER), tile(2 * D_MODEL), tile(D_MODEL), tile(PLE_DIM), tile(D_MODEL),
                  whole((ATT_OUT + D_INNER, D_MODEL)), whole((D_MODEL, D_MODEL)),
                  whole((PLE_DIM, D_MODEL)), whole((3, D_MODEL)), whole((1, D_MODEL)),
                  whole((1, D_MODEL))],
        out_specs=tile(D_MODEL),
        out_shape=jax.ShapeDtypeStruct((b, s, d), F32),
        compiler_params=pltpu.CompilerParams(
            dimension_semantics=("parallel", "parallel"), vmem_limit_bytes=VMEM_LIMIT),
        name="merge_out_ln",
    )(o_att, y_ssm, g_merge, g_ple, p_i, x, w_branch.astype(BF16), w_out.astype(BF16),
      w_ple.astype(BF16), b_gate, ln_g.reshape(1, d), ln_b.reshape(1, d))


def _layer(x, p_i, w_in, b_gate, conv_w, conv_b, dt_bias, a_log, d_skip, ssm_norm_w,
           w_branch, w_out, w_ple, ln_g, ln_b, bias, alpha):
    row_scale = jnp.ones((w_in.shape[1], 1), F32).at[:Q_END].set(LOG2E * HEAD_DIM ** -0.5)
    wt = (jnp.swapaxes(w_in, 0, 1) * row_scale).astype(BF16)
    group_w = HEADS_PER_GROUP * HEAD_DIM
    qkv = []
    for g, (_, dil) in enumerate(PATTERNS):
        wg = jnp.concatenate([wt[part + g * group_w:part + (g + 1) * group_w]
                              for part in (0, Q_END, K_END)], axis=0)
        qkv.append(_project(x, wg, dilation=dil, tiled_out=True, out_dtype=BF16, rows=QKV_ROWS,
                            name=f"proj_qkv{g}"))
    w_gates = jnp.concatenate(
        [wt[V_END:GATT_END], jnp.pad(wt[XBC_END:DT_END], ((0, LANES - SSM_HEADS), (0, 0))),
         wt[GMERGE_END:]], axis=0)
    dt_bias_tile = jnp.pad(dt_bias, (0, LANES - SSM_HEADS)).reshape(1, LANES)
    gate_dt, g_ple = _project(x, w_gates, widths=(ATT_OUT + LANES, D_MODEL),
                              silu_widths=(ATT_OUT, 0), softplus_tail=dt_bias_tile,
                              rows=QKV_ROWS, name="proj_gates")
    w_z_gm = jnp.concatenate([wt[GATT_END:Z_END], wt[DT_END:GMERGE_END]], axis=0)
    z_gate, g_merge = _project(x, w_z_gm, widths=(D_INNER, 2 * D_MODEL),
                               silu_widths=(D_INNER, 0), name="proj_z_gmerge")
    (xbc,) = _project(x, wt[Z_END:XBC_END], widths=(CONV_DIM,), name="proj_xbc")

    o_att = _attention(qkv, bias, gate_dt)
    y_ssm = _ssd(xbc, z_gate, gate_dt, conv_w, conv_b, a_log, d_skip, ssm_norm_w)
    return _output(o_att, y_ssm, g_merge, g_ple, p_i, x, w_branch, w_out, w_ple, b_gate,
                   ln_g, ln_b, alpha)


def kernel(x, p, w_in, b_gate, conv_w, conv_b, dt_bias, a_log, d_skip, ssm_norm_w, w_branch, w_out,
           w_ple, ln_g, ln_b, rel_bias):
    depth = w_in.shape[0]
    alpha = (2.0 * depth) ** 0.25
    bias = _bias_tiles(rel_bias)
    for i in range(depth):
        x = _layer(x, p[i], w_in[i], b_gate[i], conv_w[i], conv_b[i], dt_bias[i], a_log[i],
                   d_skip[i], ssm_norm_w[i], w_branch[i], w_out[i], w_ple[i], ln_g[i], ln_b[i],
                   bias, alpha)
    return x
```
